```python
import math
import jax, jax.numpy as jnp
from jax import lax
import numpy as np

D_MODEL = 1024
BATCH = 4
SEQ = 8192
DEPTH = 1
DEC_BATCH = 16
DEC_SEQ = 16
PAST_LEN = 2048

CHUNK = 64
Q_BLOCK = 128
FOX_HEADS = 8
FOX_HEAD_DIM = 64
FOX_WIDTH = FOX_HEADS * FOX_HEAD_DIM
GDN_HEADS = 8
GDN_DK = 64
GDN_DV = 64
GDN_KW = GDN_HEADS * GDN_DK
GDN_VW = GDN_HEADS * GDN_DV
GDN_CONV_DIM = 2 * GDN_KW + GDN_VW
CONV_WIDTH = 4
D_FF = 4 * D_MODEL
IN_WIDTH = 3 * FOX_WIDTH + FOX_HEADS + GDN_CONV_DIM + 2 * GDN_HEADS + GDN_VW + 2 * D_MODEL
EPS = 1e-6

kernel_name = 'hybrid_fox_gdn_stream_step'

F32 = jnp.float32


def rmsnorm(x, gain):
    xf = x.astype(F32)
    y = xf * lax.rsqrt(jnp.mean(xf * xf, axis=-1, keepdims=True) + EPS)
    return (y * gain.astype(F32)).astype(x.dtype)


def l2norm(x):
    xf = x.astype(F32)
    return xf * lax.rsqrt(jnp.sum(xf * xf, axis=-1, keepdims=True) + EPS)


def short_conv(x_ext, w):
    t = x_ext.shape[1] - (CONV_WIDTH - 1)
    y = x_ext[:, 0:t] * w[0]
    for i in range(1, CONV_WIDTH):
        y = y + x_ext[:, i:i + t] * w[i]
    return jax.nn.silu(y)


def in_project(h, w_in, forget_bias, a_log, dt_bias):
    b, t = h.shape[0], h.shape[1]
    sizes = (FOX_WIDTH, FOX_WIDTH, FOX_WIDTH, FOX_HEADS, GDN_CONV_DIM, GDN_HEADS, GDN_HEADS, GDN_VW, D_MODEL, D_MODEL)
    points = [int(p) for p in np.cumsum(sizes)[:-1]]
    z = jnp.einsum('btd,de->bte', h, w_in)
    fq, fk, fv, ff, gqkv, ga, gb, gg, gate_a, gate_b = jnp.split(z, points, axis=-1)
    fq = fq.reshape(b, t, FOX_HEADS, FOX_HEAD_DIM)
    fk = fk.reshape(b, t, FOX_HEADS, FOX_HEAD_DIM)
    fv = fv.reshape(b, t, FOX_HEADS, FOX_HEAD_DIM)
    logf = jax.nn.log_sigmoid(ff.astype(F32) + forget_bias.astype(F32))
    g = -jnp.exp(a_log.astype(F32)) * jax.nn.softplus(ga.astype(F32) + dt_bias.astype(F32))
    beta = jax.nn.sigmoid(gb.astype(F32))
    return fq, fk, fv, logf, gqkv, g, beta, gg, gate_a, gate_b


def gdn_heads(conv_out):
    b, t = conv_out.shape[0], conv_out.shape[1]
    q = conv_out[..., :GDN_KW].reshape(b, t, GDN_HEADS, GDN_DK)
    k = conv_out[..., GDN_KW:2 * GDN_KW].reshape(b, t, GDN_HEADS, GDN_DK)
    v = conv_out[..., 2 * GDN_KW:].reshape(b, t, GDN_HEADS, GDN_DV)
    return l2norm(q) * (GDN_DK ** -0.5), l2norm(k), v.astype(F32)


def fox_prompt(q, k, v, logf):
    b, s = q.shape[0], q.shape[1]
    nblk = s // Q_BLOCK
    scale = FOX_HEAD_DIM ** -0.5
    ft = jnp.cumsum(logf, axis=1).transpose(0, 2, 1)
    q_blocks = jnp.moveaxis(q.reshape(b, nblk, Q_BLOCK, FOX_HEADS, FOX_HEAD_DIM), 1, 0)
    f_blocks = jnp.moveaxis(ft.reshape(b, FOX_HEADS, nblk, Q_BLOCK), 2, 0)
    kpos = jnp.arange(s)

    def one_block(args):
        blk, q_blk, f_blk = args
        sc = jnp.einsum('bqhd,bkhd->bhqk', q_blk, k).astype(F32) * scale
        sc = sc + f_blk[..., :, None] - ft[..., None, :]
        qpos = blk * Q_BLOCK + jnp.arange(Q_BLOCK)
        sc = jnp.where(kpos[None, :] <= qpos[:, None], sc, -jnp.inf)
        p = jax.nn.softmax(sc, axis=-1)
        return jnp.einsum('bhqk,bkhd->bqhd', p.astype(v.dtype), v)

    out = lax.map(one_block, (jnp.arange(nblk), q_blocks, f_blocks))
    return jnp.moveaxis(out, 0, 1).reshape(b, s, FOX_HEADS, FOX_HEAD_DIM)


def fox_sample(q, k, v, logf, past):
    t = q.shape[1]
    scale = FOX_HEAD_DIM ** -0.5
    ft = jnp.cumsum(logf, axis=1).transpose(0, 2, 1)
    sc = jnp.einsum('bqhd,bkhd->bhqk', q, k).astype(F32) * scale
    sc = sc + ft[..., past:, None] - ft[..., None, :]
    mask = jnp.arange(past + t)[None, :] <= (past + jnp.arange(t))[:, None]
    p = jax.nn.softmax(jnp.where(mask, sc, -jnp.inf), axis=-1)
    return jnp.einsum('bhqk,bkhd->bqhd', p.astype(v.dtype), v)


def gdn_chunk(state, q, k, v, g, beta):
    q = q.transpose(0, 2, 1, 3)
    k = k.transpose(0, 2, 1, 3)
    v = v.transpose(0, 2, 1, 3)
    g = g.transpose(0, 2, 1)
    beta = beta.transpose(0, 2, 1)
    c = q.shape[2]
    gc = jnp.cumsum(g, axis=-1)
    pos = jnp.arange(c)
    incl = pos[:, None] >= pos[None, :]
    strict = pos[:, None] > pos[None, :]
    decay = jnp.exp(jnp.where(incl, gc[..., :, None] - gc[..., None, :], -jnp.inf))
    kb = k * beta[..., None]
    lower = jnp.einsum('bhik,bhjk->bhij', kb, k) * jnp.where(strict, decay, 0.0)
    rhs = jnp.concatenate([v * beta[..., None], kb * jnp.exp(gc)[..., None]], axis=-1)
    sol = lax.linalg.triangular_solve(lower, rhs, left_side=True, lower=True, unit_diagonal=True)
    u, w = sol[..., :GDN_DV], sol[..., GDN_DV:]
    v_new = u - jnp.einsum('bhck,bhkv->bhcv', w, state)
    o = (jnp.einsum('bhck,bhkv->bhcv', q * jnp.exp(gc)[..., None], state)
         + jnp.einsum('bhij,bhjv->bhiv', jnp.einsum('bhik,bhjk->bhij', q, k) * decay, v_new))
    g_last = gc[..., -1:]
    new_state = (state * jnp.exp(g_last)[..., None]
                 + jnp.einsum('bhck,bhcv->bhkv', k * jnp.exp(g_last - gc)[..., None], v_new))
    return new_state, o.transpose(0, 2, 1, 3)


def gdn_prompt(q, k, v, g, beta):
    b, s = q.shape[0], q.shape[1]
    n = s // CHUNK
    to_chunks = lambda a: jnp.moveaxis(a.reshape((b, n, CHUNK) + a.shape[2:]), 1, 0)
    xs = (to_chunks(q), to_chunks(k), to_chunks(v), to_chunks(g), to_chunks(beta))
    s0 = jnp.zeros((b, GDN_HEADS, GDN_DK, GDN_DV), F32)

    def step(st, args):
        return gdn_chunk(st, *args)

    s_fin, o = lax.scan(step, s0, xs)
    return s_fin, jnp.moveaxis(o, 0, 1).reshape(b, s, GDN_HEADS, GDN_DV)


def merge(fox_o, gdn_o, gg, gate_a, gate_b, norm_g, w_pa, w_pb, w_out):
    b, t = fox_o.shape[0], fox_o.shape[1]
    ya = fox_o.reshape(b, t, FOX_WIDTH) @ w_pa
    o = rmsnorm(gdn_o, norm_g) * jax.nn.silu(gg.reshape(b, t, GDN_HEADS, GDN_DV).astype(F32))
    yb = o.reshape(b, t, GDN_VW).astype(ya.dtype) @ w_pb
    m = jax.nn.sigmoid(gate_a) * ya + jax.nn.sigmoid(gate_b) * yb
    return m @ w_out


def mixer_prompt(h, w_in, forget_bias, conv_w, a_log, dt_bias, norm_g, w_pa, w_pb, w_out):
    fq, fk, fv, logf, gqkv, g, beta, gg, gate_a, gate_b = in_project(h, w_in, forget_bias, a_log, dt_bias)
    fox_o = fox_prompt(fq, fk, fv, logf)
    conv_ext = jnp.pad(gqkv, ((0, 0), (CONV_WIDTH - 1, 0), (0, 0)))
    q, k, v = gdn_heads(short_conv(conv_ext, conv_w))
    s_fin, gdn_o = gdn_prompt(q, k, v, g, beta)
    y = merge(fox_o, gdn_o, gg, gate_a, gate_b, norm_g, w_pa, w_pb, w_out)
    return y, (fk, fv, logf, s_fin, conv_ext[:, -(CONV_WIDTH - 1):])


def mixer_sample(h, k_cache, v_cache, logf_cache, s_cache, conv_cache,
                 w_in, forget_bias, conv_w, a_log, dt_bias, norm_g, w_pa, w_pb, w_out):
    fq, fk, fv, logf, gqkv, g, beta, gg, gate_a, gate_b = in_project(h, w_in, forget_bias, a_log, dt_bias)
    past = k_cache.shape[1]
    k_cat = jnp.concatenate([k_cache.astype(fk.dtype), fk], axis=1)
    v_cat = jnp.concatenate([v_cache.astype(fv.dtype), fv], axis=1)
    lf_cat = jnp.concatenate([logf_cache.astype(F32), logf], axis=1)
    fox_o = fox_sample(fq, k_cat, v_cat, lf_cat, past)
    conv_ext = jnp.concatenate([conv_cache.astype(gqkv.dtype), gqkv], axis=1)
    q, k, v = gdn_heads(short_conv(conv_ext, conv_w))
    s_new, gdn_o = gdn_chunk(s_cache.astype(F32), q, k, v, g, beta)
    y = merge(fox_o, gdn_o, gg, gate_a, gate_b, norm_g, w_pa, w_pb, w_out)
    return y, (fk, fv, logf, s_new, conv_ext[:, -(CONV_WIDTH - 1):])


def channel_mixer(x, pre_g, post_g, w_up, w_down):
    h = rmsnorm(x, pre_g)
    u = jnp.square(jax.nn.relu(h @ w_up))
    return x + rmsnorm(u @ w_down, post_g)


def setup_inputs(seed: int = 0) -> dict:
    key = jax.random.key(seed)
    ks = jax.random.split(key, 24)

    def nrm(k, shape, scale):
        return jax.random.normal(k, shape, F32) * scale

    def gain(k, shape):
        return 1.0 + 0.02 * jax.random.normal(k, shape, F32)

    x_prompt = nrm(ks[0], (BATCH, SEQ, D_MODEL), 1.0)
    x_sample = nrm(ks[1], (DEC_BATCH, DEC_SEQ, D_MODEL), 1.0)
    cache_fox_k = nrm(ks[2], (DEPTH, DEC_BATCH, PAST_LEN, FOX_HEADS, FOX_HEAD_DIM), 1.0)
    cache_fox_v = nrm(ks[3], (DEPTH, DEC_BATCH, PAST_LEN, FOX_HEADS, FOX_HEAD_DIM), 1.0)
    cache_fox_logf = jax.nn.log_sigmoid(
        jax.random.uniform(ks[4], (DEPTH, DEC_BATCH, PAST_LEN, FOX_HEADS), F32, 1.0, 5.0)
        + nrm(ks[5], (DEPTH, DEC_BATCH, PAST_LEN, FOX_HEADS), 1.0))
    state_gdn = nrm(ks[6], (DEPTH, DEC_BATCH, GDN_HEADS, GDN_DK, GDN_DV), 0.1)
    state_gdn_conv = nrm(ks[7], (DEPTH, DEC_BATCH, CONV_WIDTH - 1, GDN_CONV_DIM), 1.0)
    w_in = nrm(ks[8], (DEPTH, D_MODEL, IN_WIDTH), D_MODEL ** -0.5)
    fox_forget_bias = jax.random.uniform(ks[9], (DEPTH, FOX_HEADS), F32, 1.0, 5.0)
    gdn_conv_w = nrm(ks[10], (DEPTH, CONV_WIDTH, GDN_CONV_DIM), CONV_WIDTH ** -0.5)
    gdn_a_log = jnp.log(jax.random.uniform(ks[11], (DEPTH, GDN_HEADS), F32, 1.0, 16.0))
    dt = jnp.exp(jax.random.uniform(ks[12], (DEPTH, GDN_HEADS), F32, math.log(1e-3), math.log(1e-1)))
    gdn_dt_bias = dt + jnp.log(-jnp.expm1(-dt))
    gdn_norm_g = gain(ks[13], (DEPTH, GDN_DV))
    w_proj_fox = nrm(ks[14], (DEPTH, FOX_WIDTH, D_MODEL), FOX_WIDTH ** -0.5)
    w_proj_gdn = nrm(ks[15], (DEPTH, GDN_VW, D_MODEL), GDN_VW ** -0.5)
    w_out = nrm(ks[16], (DEPTH, D_MODEL, D_MODEL), D_MODEL ** -0.5)
    norm_mix_pre = gain(ks[17], (DEPTH, D_MODEL))
    norm_mix_post = gain(ks[18], (DEPTH, D_MODEL))
    norm_mlp_pre = gain(ks[19], (DEPTH, D_MODEL))
    norm_mlp_post = gain(ks[20], (DEPTH, D_MODEL))
    w_up = nrm(ks[21], (DEPTH, D_MODEL, D_FF), D_MODEL ** -0.5)
    w_down = nrm(ks[22], (DEPTH, D_FF, D_MODEL), D_FF ** -0.5)
    return {'x_prompt': x_prompt, 'x_sample': x_sample,
            'cache_fox_k': cache_fox_k, 'cache_fox_v': cache_fox_v, 'cache_fox_logf': cache_fox_logf,
            'state_gdn': state_gdn, 'state_gdn_conv': state_gdn_conv,
            'w_in': w_in, 'fox_forget_bias': fox_forget_bias, 'gdn_conv_w': gdn_conv_w,
            'gdn_a_log': gdn_a_log, 'gdn_dt_bias': gdn_dt_bias, 'gdn_norm_g': gdn_norm_g,
            'w_proj_fox': w_proj_fox, 'w_proj_gdn': w_proj_gdn, 'w_out': w_out,
            'norm_mix_pre': norm_mix_pre, 'norm_mix_post': norm_mix_post,
            'norm_mlp_pre': norm_mlp_pre, 'norm_mlp_post': norm_mlp_post,
            'w_up': w_up, 'w_down': w_down}


def reference(x_prompt, x_sample, cache_fox_k, cache_fox_v, cache_fox_logf, state_gdn, state_gdn_conv,
              w_in, fox_forget_bias, gdn_conv_w, gdn_a_log, gdn_dt_bias, gdn_norm_g,
              w_proj_fox, w_proj_gdn, w_out, norm_mix_pre, norm_mix_post, norm_mlp_pre, norm_mlp_post,
              w_up, w_down):
    y_p, y_s = x_prompt, x_sample
    st_p, st_s = [], []
    for l in range(DEPTH):
        mw = (w_in[l], fox_forget_bias[l], gdn_conv_w[l], gdn_a_log[l], gdn_dt_bias[l], gdn_norm_g[l],
              w_proj_fox[l], w_proj_gdn[l], w_out[l])
        mix, sp = mixer_prompt(rmsnorm(y_p, norm_mix_pre[l]), *mw)
        y_p = y_p + rmsnorm(mix, norm_mix_post[l])
        y_p = channel_mixer(y_p, norm_mlp_pre[l], norm_mlp_post[l], w_up[l], w_down[l])
        mix, ss = mixer_sample(rmsnorm(y_s, norm_mix_pre[l]), cache_fox_k[l], cache_fox_v[l], cache_fox_logf[l],
                               state_gdn[l], state_gdn_conv[l], *mw)
        y_s = y_s + rmsnorm(mix, norm_mix_post[l])
        y_s = channel_mixer(y_s, norm_mlp_pre[l], norm_mlp_post[l], w_up[l], w_down[l])
        st_p.append(sp)
        st_s.append(ss)
    fk_p, fv_p, lf_p, sg_p, cv_p = [jnp.stack(a) for a in zip(*st_p)]
    fk_s, fv_s, lf_s, sg_s, cv_s = [jnp.stack(a) for a in zip(*st_s)]
    return (y_p, y_s, fk_p, fv_p, lf_p, sg_p, cv_p, fk_s, fv_s, lf_s, sg_s, cv_s)
```

```python
import functools

import jax
import jax.numpy as jnp
import numpy as np
from jax import lax
from jax.experimental import pallas as pl
from jax.experimental.pallas import tpu as pltpu

F32 = jnp.float32
BF16 = jnp.bfloat16

D_MODEL = 1024
HEADS = 8
HEAD_DIM = 64
WIDTH = HEADS * HEAD_DIM
CONV_DIM = 3 * WIDTH
CONV_WIDTH = 4
D_FF = 4 * D_MODEL
EPS = 1e-6

LANES = 128
SUBLANES = 8
TOKEN_TILE = 256
ATTN_TILE = 256
FF_TILE = 1024
GDN_CHUNK = 64
VMEM_LIMIT = 56 * 1024 * 1024

C_Q, C_K, C_V, C_GQKV, C_GG, C_GA, C_GB, C_SMALL = 0, 512, 1024, 1536, 3072, 3584, 4608, 5632
IN_COLS = C_SMALL + LANES
L_LOGF, L_G, L_BETA = 0, 8, 16


def _split3(x):
    hi = x.astype(BF16)
    r = x - hi.astype(F32)
    mid = r.astype(BF16)
    lo = (r - mid.astype(F32)).astype(BF16)
    return hi, mid, lo


def _dot(a, b):
    return jnp.dot(a, b, preferred_element_type=F32)


def _dot_nt(a, b):
    return lax.dot_general(a, b, (((1,), (1,)), ((), ())), preferred_element_type=F32)


def _dot_tn(a, b):
    return lax.dot_general(a, b, (((0,), (0,)), ((), ())), preferred_element_type=F32)


def _sel_dot(sel, x, terms=3):
    parts = _split3(x)[:terms]
    out = _dot(sel, parts[0])
    for p in parts[1:]:
        out = out + _dot(sel, p)
    return out


def _dot_sel(x, sel, terms=3):
    parts = _split3(x)[:terms]
    out = _dot(parts[0], sel)
    for p in parts[1:]:
        out = out + _dot(p, sel)
    return out


def _sigmoid(x):
    return 1.0 / (1.0 + jnp.exp(-x))


def _rms(x, gain):
    ms = jnp.mean(x * x, axis=-1, keepdims=True)
    return x * lax.rsqrt(ms + EPS) * gain


def _const_spec(shape):
    nd = len(shape)
    return pl.BlockSpec(shape, lambda *_: (0,) * nd)


def _inproj_kernel(x_ref, gain_ref, w_ref, padd_ref, palog_ref, tri_ref,
                   q_ref, k_ref, vt_ref, fk_ref, fv_ref, gqkv_ref, sgg_ref, sga_ref, sgb_ref,
                   small_ref, cum_ref, carry_ref, *, tiles_per_seg):
    i = pl.program_id(0)
    h = _rms(x_ref[...], gain_ref[...]).astype(BF16)

    def mm(c0, width):
        return _dot(h, w_ref[:, c0:c0 + width])

    zq = mm(C_Q, WIDTH)
    q_ref[...] = (zq * (HEAD_DIM ** -0.5)).astype(BF16)
    zk = mm(C_K, WIDTH)
    fk_ref[...] = zk
    k_ref[...] = zk.astype(BF16)
    zv = mm(C_V, WIDTH)
    fv_ref[...] = zv
    vt_ref[0] = zv.T.astype(BF16)
    for c in range(3):
        gqkv_ref[:, c * WIDTH:(c + 1) * WIDTH] = mm(C_GQKV + c * WIDTH, WIDTH)
    zg = mm(C_GG, WIDTH)
    sgg_ref[...] = (zg * _sigmoid(zg)).astype(BF16)
    for c in range(2):
        sga_ref[:, c * WIDTH:(c + 1) * WIDTH] = _sigmoid(mm(C_GA + c * WIDTH, WIDTH)).astype(BF16)
        sgb_ref[:, c * WIDTH:(c + 1) * WIDTH] = _sigmoid(mm(C_GB + c * WIDTH, WIDTH)).astype(BF16)

    zs = mm(C_SMALL, LANES)
    t = zs + padd_ref[...]
    l1p = jnp.log1p(jnp.exp(-jnp.abs(t)))
    logf = jnp.minimum(t, 0.0) - l1p
    g = -jnp.exp(palog_ref[...]) * (jnp.maximum(t, 0.0) + l1p)
    beta = _sigmoid(zs)
    lane = lax.broadcasted_iota(jnp.int32, zs.shape, 1)
    small = jnp.where(lane < L_G, logf, jnp.where(lane < L_BETA, g, jnp.where(lane < L_BETA + HEADS, beta, 0.0)))
    small_ref[...] = small

    cum = _sel_dot(tri_ref[...], small)
    if tiles_per_seg > 1:
        @pl.when(i % tiles_per_seg == 0)
        def _():
            carry_ref[...] = jnp.zeros_like(carry_ref)
        cum = cum + carry_ref[0:1, :]
        carry_ref[...] = jnp.broadcast_to(cum[-1:, :], carry_ref.shape)
    cum_ref[...] = cum


def _in_proj(x, gain, w_r, padd, palog, seg_len):
    t_total = x.shape[0]
    tm = TOKEN_TILE
    assert t_total % tm == 0
    nt = t_total // tm
    if seg_len >= tm:
        assert seg_len % tm == 0
        tiles_per_seg = seg_len // tm
        r = np.arange(tm)
        tri = (r[None, :] <= r[:, None])
    else:
        assert tm % seg_len == 0
        tiles_per_seg = 1
        r = np.arange(tm)
        tri = (r[None, :] <= r[:, None]) & ((r[None, :] // seg_len) == (r[:, None] // seg_len))
    tri = jnp.asarray(tri, BF16)

    row = lambda w: pl.BlockSpec((tm, w), lambda i: (i, 0))
    out_shape = (
        jax.ShapeDtypeStruct((t_total, WIDTH), BF16),
        jax.ShapeDtypeStruct((t_total, WIDTH), BF16),
        jax.ShapeDtypeStruct((nt, WIDTH, tm), BF16),
        jax.ShapeDtypeStruct((t_total, WIDTH), F32),
        jax.ShapeDtypeStruct((t_total, WIDTH), F32),
        jax.ShapeDtypeStruct((t_total, CONV_DIM), F32),
        jax.ShapeDtypeStruct((t_total, WIDTH), BF16),
        jax.ShapeDtypeStruct((t_total, D_MODEL), BF16),
        jax.ShapeDtypeStruct((t_total, D_MODEL), BF16),
        jax.ShapeDtypeStruct((t_total, LANES), F32),
        jax.ShapeDtypeStruct((t_total, LANES), F32),
    )
    out_specs = (row(WIDTH), row(WIDTH), pl.BlockSpec((1, WIDTH, tm), lambda i: (i, 0, 0)),
                 row(WIDTH), row(WIDTH), row(CONV_DIM), row(WIDTH), row(D_MODEL), row(D_MODEL),
                 row(LANES), row(LANES))
    return pl.pallas_call(
        functools.partial(_inproj_kernel, tiles_per_seg=tiles_per_seg),
        grid=(nt,),
        in_specs=[row(D_MODEL), _const_spec((1, D_MODEL)), _const_spec((D_MODEL, IN_COLS)),
                  _const_spec((1, LANES)), _const_spec((1, LANES)), _const_spec((tm, tm))],
        out_specs=out_specs,
        out_shape=out_shape,
        scratch_shapes=[pltpu.VMEM((SUBLANES, LANES), F32)],
        compiler_params=pltpu.CompilerParams(dimension_semantics=("arbitrary",), vmem_limit_bytes=VMEM_LIMIT),
        name="in_proj",
    )(x, gain, w_r, padd, palog, tri)


def _fox_prompt_kernel(q_ref, k_ref, vt_ref, cum_ref, o_ref, bias_ref, *, seq, tile):
    p = pl.program_id(1)
    qi = pl.program_id(2)
    rows = 512

    @pl.when(qi == 0)
    def _prep():
        def fill(r, carry):
            c = cum_ref[pl.ds(r * rows, rows), :]
            lane = lax.broadcasted_iota(jnp.int32, c.shape, 1)
            for hh in range(2):
                col = jnp.sum(jnp.where(lane == 2 * p + hh, c, 0.0), axis=-1, keepdims=True)
                bias_ref[hh, pl.ds(r * rows, rows), :] = jnp.broadcast_to(-col, c.shape)
            return carry
        lax.fori_loop(0, seq // rows, fill, 0)

    q2 = q_ref[...]
    lane_q = lax.broadcasted_iota(jnp.int32, q2.shape, 1)
    qh = [jnp.where((lane_q >= HEAD_DIM * hh) & (lane_q < HEAD_DIM * (hh + 1)), q2, jnp.zeros_like(q2))
          for hh in range(2)]
    reps = tile // LANES

    def block(j, carry, masked):
        kj = k_ref[pl.ds(j * tile, tile), :]
        new = []
        for hh in range(2):
            m, l, acc = carry[hh]
            s_t = _dot_nt(kj, qh[hh])
            b = bias_ref[hh, pl.ds(j * tile, tile), :]
            s_t = s_t + jnp.concatenate([b] * reps, axis=1)
            if masked:
                kr = lax.broadcasted_iota(jnp.int32, s_t.shape, 0)
                qc = lax.broadcasted_iota(jnp.int32, s_t.shape, 1)
                s_t = jnp.where(kr <= qc, s_t, -1e30)
            m_new = jnp.maximum(m, jnp.max(s_t, axis=0, keepdims=True))
            alpha = jnp.exp(m - m_new)
            p_t = jnp.exp(s_t - m_new)
            l = alpha * l + jnp.sum(p_t, axis=0, keepdims=True)
            vj = vt_ref[j, HEAD_DIM * hh:HEAD_DIM * (hh + 1), :]
            acc = alpha * acc + _dot(vj, p_t.astype(BF16))
            new.append((m_new, l, acc))
        return tuple(new)

    init = tuple((jnp.full((1, tile), -1e30, F32), jnp.zeros((1, tile), F32), jnp.zeros((HEAD_DIM, tile), F32))
                 for _ in range(2))
    carry = lax.fori_loop(0, qi, lambda j, c: block(j, c, False), init)
    carry = block(qi, carry, True)
    o_t = jnp.concatenate([acc / l for (_, l, acc) in carry], axis=0)
    o_ref[...] = o_t.T.astype(BF16)


def _fox_prompt(q, k, vt, cum, batch, seq):
    tile = ATTN_TILE
    assert seq % tile == 0 and seq % 512 == 0 and tile == TOKEN_TILE
    nq = seq // tile
    pairs = HEADS // 2
    return pl.pallas_call(
        functools.partial(_fox_prompt_kernel, seq=seq, tile=tile),
        grid=(batch, pairs, nq),
        in_specs=[pl.BlockSpec((tile, LANES), lambda b, p, i: (b * nq + i, p)),
                  pl.BlockSpec((seq, LANES), lambda b, p, i: (b, p)),
                  pl.BlockSpec((nq, LANES, tile), lambda b, p, i: (b, p, 0)),
                  pl.BlockSpec((seq, LANES), lambda b, p, i: (b, 0))],
        out_specs=pl.BlockSpec((tile, LANES), lambda b, p, i: (b * nq + i, p)),
        out_shape=jax.ShapeDtypeStruct((batch * seq, WIDTH), BF16),
        scratch_shapes=[pltpu.VMEM((2, seq, LANES), F32)],
        compiler_params=pltpu.CompilerParams(dimension_semantics=("arbitrary",) * 3, vmem_limit_bytes=VMEM_LIMIT),
        name="fox_prompt",
    )(q, k, vt, cum)


def _fox_sample_kernel(q_ref, kn_ref, vn_ref, small_ref, kc_ref, vc_ref, lfc_ref, tri_ref, trin_ref, o_ref,
                       *, past, steps):
    blk = tri_ref.shape[0]
    carry = jnp.zeros((1, HEADS), F32)
    cums = []
    for r in range(past // blk):
        c = _sel_dot(tri_ref[...], lfc_ref[0, r * blk:(r + 1) * blk, :]) + carry
        cums.append(c)
        carry = c[-1:, :]
    cum_c = jnp.concatenate(cums, axis=0)
    cum_n = _sel_dot(trin_ref[...], small_ref[:, L_LOGF:L_LOGF + HEADS]) + carry

    q = q_ref[...]
    kc = kc_ref[0].astype(BF16)
    vc = vc_ref[0].astype(BF16)
    kn = kn_ref[...].astype(BF16)
    vn = vn_ref[...].astype(BF16)
    lane_c = lax.broadcasted_iota(jnp.int32, (past, LANES), 1)
    lane_n = lax.broadcasted_iota(jnp.int32, (steps, LANES), 1)
    kr = lax.broadcasted_iota(jnp.int32, (steps, steps), 0)
    qc = lax.broadcasted_iota(jnp.int32, (steps, steps), 1)
    ones_c = jnp.ones((past, LANES), BF16)
    ones_n = jnp.ones((steps, LANES), BF16)
    slabs = []
    for pr in range(HEADS // 2):
        ls = slice(pr * LANES, (pr + 1) * LANES)
        q_s, kc_s, kn_s, vc_s, vn_s = q[:, ls], kc[:, ls], kn[:, ls], vc[:, ls], vn[:, ls]
        o_s = jnp.zeros((steps, LANES), F32)
        for hh in range(2):
            hd = 2 * pr + hh
            in_c = (lane_c >= HEAD_DIM * hh) & (lane_c < HEAD_DIM * (hh + 1))
            in_n = (lane_n >= HEAD_DIM * hh) & (lane_n < HEAD_DIM * (hh + 1))
            qm = jnp.where(in_n, q_s, jnp.zeros_like(q_s))
            s1 = _dot_nt(kc_s, qm) - cum_c[:, hd:hd + 1]
            s2 = _dot_nt(kn_s, qm) - cum_n[:, hd:hd + 1]
            s2 = jnp.where(kr <= qc, s2, -1e30)
            m = jnp.maximum(jnp.max(s1, axis=0, keepdims=True), jnp.max(s2, axis=0, keepdims=True))
            p1 = jnp.exp(s1 - m).astype(BF16)
            p2 = jnp.exp(s2 - m).astype(BF16)
            num = _dot_tn(p1, jnp.where(in_c, vc_s, jnp.zeros_like(vc_s))) + \
                _dot_tn(p2, jnp.where(in_n, vn_s, jnp.zeros_like(vn_s)))
            den = _dot_tn(p1, ones_c) + _dot_tn(p2, ones_n)
            o_s = o_s + num / den
        slabs.append(o_s)
    o_ref[...] = jnp.concatenate(slabs, axis=1).astype(BF16)


def _fox_sample(q, kn, vn, small, kc, vc, lfc, batch, steps, past):
    blk = 256
    assert past % blk == 0
    r = np.arange(blk)
    tri = jnp.asarray(r[None, :] <= r[:, None], BF16)
    rn = np.arange(steps)
    trin = jnp.asarray(rn[None, :] <= rn[:, None], BF16)
    row = lambda w: pl.BlockSpec((steps, w), lambda b: (b, 0))
    return pl.pallas_call(
        functools.partial(_fox_sample_kernel, past=past, steps=steps),
        grid=(batch,),
        in_specs=[row(WIDTH), row(WIDTH), row(WIDTH), row(LANES),
                  pl.BlockSpec((1, past, WIDTH), lambda b: (b, 0, 0)),
                  pl.BlockSpec((1, past, WIDTH), lambda b: (b, 0, 0)),
                  pl.BlockSpec((1, past, HEADS), lambda b: (b, 0, 0)),
                  _const_spec((blk, blk)), _const_spec((steps, steps))],
        out_specs=row(WIDTH),
        out_shape=jax.ShapeDtypeStruct((batch * steps, WIDTH), BF16),
        compiler_params=pltpu.CompilerParams(dimension_semantics=("arbitrary",), vmem_limit_bytes=VMEM_LIMIT),
        name="fox_sample",
    )(q, kn, vn, small, kc, vc, lfc, tri, trin)


def _gdn_kernel(gq_ref, small_ref, hist_ref, s0_ref, convw_ref, tri_ref, hsum_ref, hexp_g_ref, hexp_b_ref,
                o_ref, sfin_ref, xbuf, state, *, chunk):
    c_idx = pl.program_id(1)
    n_chunks = pl.num_programs(1)
    hist_rows = SUBLANES

    @pl.when(c_idx == 0)
    def _():
        xbuf[0:hist_rows, :] = hist_ref[0]
        state[...] = s0_ref[0]

    xbuf[hist_rows:hist_rows + chunk, :] = gq_ref[...]
    w = convw_ref[...]
    base = hist_rows - (CONV_WIDTH - 1)
    y = xbuf[base:base + chunk, :] * w[0:1, :]
    for i in range(1, CONV_WIDTH):
        y = y + xbuf[base + i:base + i + chunk, :] * w[i:i + 1, :]
    xbuf[0:hist_rows, :] = xbuf[chunk:chunk + hist_rows, :]
    y = y * _sigmoid(y)

    hsum = hsum_ref[...]
    hexp_g = hexp_g_ref[...]
    hexp_b = hexp_b_ref[...]

    def l2n(x):
        inv = lax.rsqrt(_dot_sel(x * x, hsum, terms=2) + EPS)
        return x * _dot_sel(inv, hexp_g, terms=2)

    q = l2n(y[:, 0:WIDTH]) * (HEAD_DIM ** -0.5)
    k = l2n(y[:, WIDTH:2 * WIDTH])
    v = y[:, 2 * WIDTH:3 * WIDTH]

    small = small_ref[...]
    gc = _sel_dot(tri_ref[...], small)
    gc_x = _dot_sel(gc, hexp_g)
    beta_x = _dot_sel(small, hexp_b, terms=2)
    egc_x = jnp.exp(gc_x)
    glast_x = gc_x[chunk - 1:chunk, :]
    eglast_x = jnp.exp(glast_x)
    kb = k * beta_x
    vb = v * beta_x
    kbe = kb * egc_x
    qe = q * egc_x
    kdec = k * jnp.exp(glast_x - gc_x)

    pad_rows = LANES - chunk
    gc_sq = jnp.concatenate([gc, jnp.zeros((pad_rows, LANES), F32)], axis=0) if pad_rows else gc
    gc_t = gc_sq.T

    ri = lax.broadcasted_iota(jnp.int32, (chunk, chunk), 0)
    ci = lax.broadcasted_iota(jnp.int32, (chunk, chunk), 1)
    eye = jnp.where(ri == ci, 1.0, 0.0).astype(F32)
    n_double = int(np.log2(chunk)) - 1
    assert 2 ** (n_double + 1) == chunk

    outs = []
    for hd in range(HEADS):
        sl = slice(hd * HEAD_DIM, (hd + 1) * HEAD_DIM)
        gcol = gc[:, L_G + hd:L_G + hd + 1]
        grow = gc_t[L_G + hd:L_G + hd + 1, 0:chunk]
        dec = jnp.where(ri >= ci, jnp.exp(gcol - grow), 0.0)
        k_h = k[:, sl].astype(BF16)
        a = _dot_nt(kb[:, sl].astype(BF16), k_h) * jnp.where(ri > ci, dec, 0.0)
        pw = -a
        tinv = eye + pw
        for _ in range(n_double):
            pw_b = pw.astype(BF16)
            pw = _dot(pw_b, pw_b)
            tinv = tinv + _dot(tinv.astype(BF16), pw.astype(BF16))
        rhs = jnp.concatenate([vb[:, sl], kbe[:, sl]], axis=1).astype(BF16)
        uw = _dot(tinv.astype(BF16), rhs)
        u, wm = uw[:, 0:HEAD_DIM], uw[:, HEAD_DIM:2 * HEAD_DIM]
        s_h = state[hd]
        s_b = s_h.astype(BF16)
        v_new = u - _dot(wm.astype(BF16), s_b)
        v_new_b = v_new.astype(BF16)
        qk = _dot_nt(q[:, sl].astype(BF16), k_h) * dec
        o_h = _dot(qe[:, sl].astype(BF16), s_b) + _dot(qk.astype(BF16), v_new_b)
        state[hd] = s_h * eglast_x[:, sl] + _dot_tn(kdec[:, sl].astype(BF16), v_new_b)
        ms = jnp.mean(o_h * o_h, axis=-1, keepdims=True)
        outs.append(o_h * lax.rsqrt(ms + EPS))
    o_ref[...] = jnp.concatenate(outs, axis=1)

    @pl.when(c_idx == n_chunks - 1)
    def _():
        sfin_ref[0] = state[...]


def _gdn(gqkv, small, hist, s0, conv_w, batch, seq, chunk):
    assert seq % chunk == 0 and chunk % SUBLANES == 0 and chunk <= LANES
    nc = seq // chunk
    r = np.arange(chunk)
    tri = jnp.asarray(r[None, :] <= r[:, None], BF16)
    head_of = np.arange(WIDTH) // HEAD_DIM
    hsum = jnp.asarray(head_of[:, None] == (np.arange(LANES)[None, :] - L_G), BF16)
    hexp_g = jnp.asarray((np.arange(LANES)[:, None] - L_G) == head_of[None, :], BF16)
    hexp_b = jnp.asarray((np.arange(LANES)[:, None] - L_BETA) == head_of[None, :], BF16)
    conv_w8 = jnp.concatenate([conv_w, jnp.zeros((SUBLANES - CONV_WIDTH, CONV_DIM), F32)], axis=0)
    return pl.pallas_call(
        functools.partial(_gdn_kernel, chunk=chunk),
        grid=(batch, nc),
        in_specs=[pl.BlockSpec((chunk, CONV_DIM), lambda b, c: (b * nc + c, 0)),
                  pl.BlockSpec((chunk, LANES), lambda b, c: (b * nc + c, 0)),
                  pl.BlockSpec((1, SUBLANES, CONV_DIM), lambda b, c: (b, 0, 0)),
                  pl.BlockSpec((1, HEADS, HEAD_DIM, HEAD_DIM), lambda b, c: (b, 0, 0, 0)),
                  _const_spec((SUBLANES, CONV_DIM)), _const_spec((chunk, chunk)),
                  _const_spec((WIDTH, LANES)), _const_spec((LANES, WIDTH)), _const_spec((LANES, WIDTH))],
        out_specs=(pl.BlockSpec((chunk, WIDTH), lambda b, c: (b * nc + c, 0)),
                   pl.BlockSpec((1, HEADS, HEAD_DIM, HEAD_DIM), lambda b, c: (b, 0, 0, 0))),
        out_shape=(jax.ShapeDtypeStruct((batch * seq, WIDTH), F32),
                   jax.ShapeDtypeStruct((batch, HEADS, HEAD_DIM, HEAD_DIM), F32)),
        scratch_shapes=[pltpu.VMEM((chunk + SUBLANES, CONV_DIM), F32),
                        pltpu.VMEM((HEADS, HEAD_DIM, HEAD_DIM), F32)],
        compiler_params=pltpu.CompilerParams(dimension_semantics=("arbitrary", "arbitrary"),
                                             vmem_limit_bytes=VMEM_LIMIT),
        name="gdn",
    )(gqkv, small, hist, s0, conv_w8, tri, hsum, hexp_g, hexp_b)


def _post_kernel(x_ref, fox_ref, gdn_ref, sgg_ref, sga_ref, sgb_ref,
                 wpa_ref, wpb_ref, wout_ref, wup_ref, wdown_ref,
                 ng_ref, gpost_ref, gpre2_ref, gpost2_ref, y_ref):
    ya = _dot(fox_ref[...], wpa_ref[...])
    o = gdn_ref[...] * ng_ref[...] * sgg_ref[...].astype(F32)
    yb = _dot(o.astype(BF16), wpb_ref[...])
    m = sga_ref[...].astype(F32) * ya + sgb_ref[...].astype(F32) * yb
    mix = _dot(m.astype(BF16), wout_ref[...])
    y1 = x_ref[...] + _rms(mix, gpost_ref[...])
    h2 = _rms(y1, gpre2_ref[...]).astype(BF16)
    acc = jnp.zeros(y1.shape, F32)
    for c in range(D_FF // FF_TILE):
        u = jnp.maximum(_dot(h2, wup_ref[:, c * FF_TILE:(c + 1) * FF_TILE]), 0.0)
        acc = acc + _dot((u * u).astype(BF16), wdown_ref[c * FF_TILE:(c + 1) * FF_TILE, :])
    y_ref[...] = y1 + _rms(acc, gpost2_ref[...])


def _post(x, fox, gdn_o, sgg, sga, sgb, w_pa, w_pb, w_out, w_up, w_down, ng, gpost, gpre2, gpost2):
    t_total = x.shape[0]
    tm = TOKEN_TILE
    row = lambda w: pl.BlockSpec((tm, w), lambda i: (i, 0))
    return pl.pallas_call(
        _post_kernel,
        grid=(t_total // tm,),
        in_specs=[row(D_MODEL), row(WIDTH), row(WIDTH), row(WIDTH), row(D_MODEL), row(D_MODEL),
                  _const_spec((WIDTH, D_MODEL)), _const_spec((WIDTH, D_MODEL)), _const_spec((D_MODEL, D_MODEL)),
                  _const_spec((D_MODEL, D_FF)), _const_spec((D_FF, D_MODEL)),
                  _const_spec((1, WIDTH)), _const_spec((1, D_MODEL)), _const_spec((1, D_MODEL)),
                  _const_spec((1, D_MODEL))],
        out_specs=row(D_MODEL),
        out_shape=jax.ShapeDtypeStruct((t_total, D_MODEL), F32),
        compiler_params=pltpu.CompilerParams(dimension_semantics=("arbitrary",), vmem_limit_bytes=VMEM_LIMIT),
        name="post",
    )(x, fox, gdn_o, sgg, sga, sgb, w_pa, w_pb, w_out, w_up, w_down, ng, gpost, gpre2, gpost2)


def _rearrange_w_in(w_in):
    o_ff = 3 * WIDTH
    o_gqkv = o_ff + HEADS
    o_ga = o_gqkv + CONV_DIM
    o_gb = o_ga + HEADS
    o_gg = o_gb + HEADS
    o_gate_a = o_gg + WIDTH
    o_gate_b = o_gate_a + D_MODEL
    cols = [w_in[:, 0:o_ff], w_in[:, o_gqkv:o_ga], w_in[:, o_gg:o_gate_a], w_in[:, o_gate_a:o_gate_b],
            w_in[:, o_gate_b:o_gate_b + D_MODEL], w_in[:, o_ff:o_gqkv], w_in[:, o_ga:o_gb], w_in[:, o_gb:o_gg],
            jnp.zeros((D_MODEL, LANES - 3 * HEADS), w_in.dtype)]
    return jnp.concatenate(cols, axis=1).astype(BF16)


def _lane_row(*pieces):
    v = jnp.concatenate([p.astype(F32) for p in pieces])
    return jnp.concatenate([v, jnp.zeros((LANES - v.shape[0],), F32)])[None, :]


def _pad_hist(conv_cache):
    b = conv_cache.shape[0]
    return jnp.concatenate([jnp.zeros((b, SUBLANES - (CONV_WIDTH - 1), CONV_DIM), F32), conv_cache.astype(F32)], axis=1)


def kernel(x_prompt, x_sample, cache_fox_k, cache_fox_v, cache_fox_logf, state_gdn, state_gdn_conv,
           w_in, fox_forget_bias, gdn_conv_w, gdn_a_log, gdn_dt_bias, gdn_norm_g,
           w_proj_fox, w_proj_gdn, w_out, norm_mix_pre, norm_mix_post, norm_mlp_pre, norm_mlp_post,
           w_up, w_down):
    depth = w_in.shape[0]
    bp, sp, _ = x_prompt.shape
    bs, ss, _ = x_sample.shape
    past = cache_fox_k.shape[2]
    y_p = x_prompt.reshape(bp * sp, D_MODEL)
    y_s = x_sample.reshape(bs * ss, D_MODEL)
    st_p, st_s = [], []
    for l in range(depth):
        w_r = _rearrange_w_in(w_in[l])
        padd = _lane_row(fox_forget_bias[l], gdn_dt_bias[l])
        palog = _lane_row(jnp.zeros((HEADS,), F32), gdn_a_log[l])
        gain = norm_mix_pre[l][None, :]
        post_w = (w_proj_fox[l].astype(BF16), w_proj_gdn[l].astype(BF16), w_out[l].astype(BF16),
                  w_up[l].astype(BF16), w_down[l].astype(BF16),
                  jnp.tile(gdn_norm_g[l], HEADS)[None, :], norm_mix_post[l][None, :],
                  norm_mlp_pre[l][None, :], norm_mlp_post[l][None, :])

        q, k, vt, fk, fv, gqkv, sgg, sga, sgb, small, cum = _in_proj(y_p, gain, w_r, padd, palog, sp)
        fox = _fox_prompt(q, k, vt, cum, bp, sp)
        gdn_o, s_fin = _gdn(gqkv, small, jnp.zeros((bp, SUBLANES, CONV_DIM), F32),
                            jnp.zeros((bp, HEADS, HEAD_DIM, HEAD_DIM), F32), gdn_conv_w[l], bp, sp, GDN_CHUNK)
        y_p_new = _post(y_p, fox, gdn_o, sgg, sga, sgb, *post_w)
        st_p.append((fk.reshape(bp, sp, HEADS, HEAD_DIM), fv.reshape(bp, sp, HEADS, HEAD_DIM),
                     small[:, L_LOGF:L_LOGF + HEADS].reshape(bp, sp, HEADS), s_fin,
                     gqkv.reshape(bp, sp, CONV_DIM)[:, sp - (CONV_WIDTH - 1):, :]))
        y_p = y_p_new

        q, k, vt, fk, fv, gqkv, sgg, sga, sgb, small, cum = _in_proj(y_s, gain, w_r, padd, palog, ss)
        fox = _fox_sample(q, fk, fv, small,
                          cache_fox_k[l].reshape(bs, past, WIDTH), cache_fox_v[l].reshape(bs, past, WIDTH),
                          cache_fox_logf[l], bs, ss, past)
        gdn_o, s_new = _gdn(gqkv, small, _pad_hist(state_gdn_conv[l]), state_gdn[l].astype(F32),
                            gdn_conv_w[l], bs, ss, ss)
        y_s_new = _post(y_s, fox, gdn_o, sgg, sga, sgb, *post_w)
        conv_ext = jnp.concatenate([state_gdn_conv[l].astype(F32), gqkv.reshape(bs, ss, CONV_DIM)], axis=1)
        st_s.append((fk.reshape(bs, ss, HEADS, HEAD_DIM), fv.reshape(bs, ss, HEADS, HEAD_DIM),
                     small[:, L_LOGF:L_LOGF + HEADS].reshape(bs, ss, HEADS), s_new,
                     conv_ext[:, conv_ext.shape[1] - (CONV_WIDTH - 1):, :]))
        y_s = y_s_new

    fk_p, fv_p, lf_p, sg_p, cv_p = [jnp.stack(a) for a in zip(*st_p)]
    fk_s, fv_s, lf_s, sg_s, cv_s = [jnp.stack(a) for a in zip(*st_s)]
    return (y_p.reshape(bp, sp, D_MODEL), y_s.reshape(bs, ss, D_MODEL),
            fk_p, fv_p, lf_p, sg_p, cv_p, fk_s, fv_s, lf_s, sg_s, cv_s)
```

```python
import functools

import jax
import jax.numpy as jnp
import numpy as np
from jax import lax
from jax.experimental import pallas as pl
from jax.experimental.pallas import tpu as pltpu

F32 = jnp.float32
BF16 = jnp.bfloat16

D_MODEL = 1024
HEADS = 8
HEAD_DIM = 64
WIDTH = HEADS * HEAD_DIM
CONV_DIM = 3 * WIDTH
CONV_WIDTH = 4
D_FF = 4 * D_MODEL
EPS = 1e-6
LOG2E = 1.4426950408889634

LANES = 128
SUBLANES = 8
TOKEN_TILE = 256
ATTN_TILE = 256
KEY_TILE = 512
FF_TILE = 1024
GDN_CHUNK = 64
VMEM_LIMIT = 56 * 1024 * 1024

C_Q, C_K, C_V, C_GQKV, C_GG, C_GA, C_GB, C_SMALL = 0, 512, 1024, 1536, 3072, 3584, 4608, 5632
IN_COLS = C_SMALL + LANES
L_LOGF, L_G, L_BETA = 0, 8, 16


def _split3(x):
    hi = x.astype(BF16)
    r = x - hi.astype(F32)
    mid = r.astype(BF16)
    lo = (r - mid.astype(F32)).astype(BF16)
    return hi, mid, lo


def _dot(a, b):
    return jnp.dot(a, b, preferred_element_type=F32)


def _dot_nt(a, b):
    return lax.dot_general(a, b, (((1,), (1,)), ((), ())), preferred_element_type=F32)


def _dot_tn(a, b):
    return lax.dot_general(a, b, (((0,), (0,)), ((), ())), preferred_element_type=F32)


def _sel_dot(sel, x, terms=3):
    parts = _split3(x)[:terms]
    out = _dot(sel, parts[0])
    for p in parts[1:]:
        out = out + _dot(sel, p)
    return out


def _dot_sel(x, sel, terms=3):
    parts = _split3(x)[:terms]
    out = _dot(parts[0], sel)
    for p in parts[1:]:
        out = out + _dot(p, sel)
    return out


def _sigmoid(x):
    return 1.0 / (1.0 + jnp.exp(-x))


def _rms(x, gain):
    ms = jnp.mean(x * x, axis=-1, keepdims=True)
    return x * lax.rsqrt(ms + EPS) * gain


def _const_spec(shape):
    nd = len(shape)
    return pl.BlockSpec(shape, lambda *_: (0,) * nd)


def _inproj_kernel(x_ref, gain_ref, w_ref, padd_ref, palog_ref, tri_ref,
                   q_ref, k_ref, vt_ref, fk_ref, fv_ref, gqkv_ref, sgg_ref, sga_ref, sgb_ref,
                   small_ref, cum_ref, carry_ref, *, tiles_per_seg):
    i = pl.program_id(0)
    h = _rms(x_ref[...], gain_ref[...]).astype(BF16)

    def mm(c0, width):
        return _dot(h, w_ref[:, c0:c0 + width])

    zq = mm(C_Q, WIDTH)
    q_ref[...] = (zq * (LOG2E * HEAD_DIM ** -0.5)).astype(BF16)
    zk = mm(C_K, WIDTH)
    fk_ref[...] = zk
    k_ref[...] = zk.astype(BF16)
    zv = mm(C_V, WIDTH)
    fv_ref[...] = zv
    vt_ref[0] = zv.T.astype(BF16)
    for c in range(3):
        gqkv_ref[:, c * WIDTH:(c + 1) * WIDTH] = mm(C_GQKV + c * WIDTH, WIDTH)
    zg = mm(C_GG, WIDTH)
    sgg_ref[...] = (zg * _sigmoid(zg)).astype(BF16)
    for c in range(2):
        sga_ref[:, c * WIDTH:(c + 1) * WIDTH] = _sigmoid(mm(C_GA + c * WIDTH, WIDTH)).astype(BF16)
        sgb_ref[:, c * WIDTH:(c + 1) * WIDTH] = _sigmoid(mm(C_GB + c * WIDTH, WIDTH)).astype(BF16)

    zs = mm(C_SMALL, LANES)
    t = zs + padd_ref[...]
    l1p = jnp.log1p(jnp.exp(-jnp.abs(t)))
    logf = jnp.minimum(t, 0.0) - l1p
    g = -jnp.exp(palog_ref[...]) * (jnp.maximum(t, 0.0) + l1p)
    beta = _sigmoid(zs)
    lane = lax.broadcasted_iota(jnp.int32, zs.shape, 1)
    small = jnp.where(lane < L_G, logf, jnp.where(lane < L_BETA, g, jnp.where(lane < L_BETA + HEADS, beta, 0.0)))
    small_ref[...] = small

    cum = _sel_dot(tri_ref[...], small)
    if tiles_per_seg > 1:
        @pl.when(i % tiles_per_seg == 0)
        def _():
            carry_ref[...] = jnp.zeros_like(carry_ref)
        cum = cum + carry_ref[0:1, :]
        carry_ref[...] = jnp.broadcast_to(cum[-1:, :], carry_ref.shape)
    cum_ref[...] = cum


def _in_proj(x, gain, w_r, padd, palog, seg_len):
    t_total = x.shape[0]
    tm = TOKEN_TILE
    assert t_total % tm == 0
    nt = t_total // tm
    if seg_len >= tm:
        assert seg_len % tm == 0
        tiles_per_seg = seg_len // tm
        r = np.arange(tm)
        tri = (r[None, :] <= r[:, None])
    else:
        assert tm % seg_len == 0
        tiles_per_seg = 1
        r = np.arange(tm)
        tri = (r[None, :] <= r[:, None]) & ((r[None, :] // seg_len) == (r[:, None] // seg_len))
    tri = jnp.asarray(tri, BF16)

    row = lambda w: pl.BlockSpec((tm, w), lambda i: (i, 0))
    out_shape = (
        jax.ShapeDtypeStruct((t_total, WIDTH), BF16),
        jax.ShapeDtypeStruct((t_total, WIDTH), BF16),
        jax.ShapeDtypeStruct((nt, WIDTH, tm), BF16),
        jax.ShapeDtypeStruct((t_total, WIDTH), F32),
        jax.ShapeDtypeStruct((t_total, WIDTH), F32),
        jax.ShapeDtypeStruct((t_total, CONV_DIM), F32),
        jax.ShapeDtypeStruct((t_total, WIDTH), BF16),
        jax.ShapeDtypeStruct((t_total, D_MODEL), BF16),
        jax.ShapeDtypeStruct((t_total, D_MODEL), BF16),
        jax.ShapeDtypeStruct((t_total, LANES), F32),
        jax.ShapeDtypeStruct((t_total, LANES), F32),
    )
    out_specs = (row(WIDTH), row(WIDTH), pl.BlockSpec((1, WIDTH, tm), lambda i: (i, 0, 0)),
                 row(WIDTH), row(WIDTH), row(CONV_DIM), row(WIDTH), row(D_MODEL), row(D_MODEL),
                 row(LANES), row(LANES))
    return pl.pallas_call(
        functools.partial(_inproj_kernel, tiles_per_seg=tiles_per_seg),
        grid=(nt,),
        in_specs=[row(D_MODEL), _const_spec((1, D_MODEL)), _const_spec((D_MODEL, IN_COLS)),
                  _const_spec((1, LANES)), _const_spec((1, LANES)), _const_spec((tm, tm))],
        out_specs=out_specs,
        out_shape=out_shape,
        scratch_shapes=[pltpu.VMEM((SUBLANES, LANES), F32)],
        compiler_params=pltpu.CompilerParams(dimension_semantics=("arbitrary",), vmem_limit_bytes=VMEM_LIMIT),
        name="in_proj",
    )(x, gain, w_r, padd, palog, tri)


def _fox_prompt_kernel(q_ref, k_ref, vt_ref, cum_ref, o_ref, bias_ref, s_ref, p_ref, *, seq, tq, tk):
    p = pl.program_id(1)
    qi = pl.program_id(2)
    rows = 512
    sub = tk // tq

    @pl.when(qi == 0)
    def _prep():
        def fill(r, carry):
            c = cum_ref[pl.ds(r * rows, rows), :]
            lane = lax.broadcasted_iota(jnp.int32, c.shape, 1)
            slab = jnp.zeros(c.shape, F32)
            for hh in range(2):
                col = jnp.sum(jnp.where(lane == 2 * p + hh, c, 0.0), axis=-1, keepdims=True) * (-LOG2E)
                for t, part in enumerate(_split3(col)):
                    slab = jnp.where(lane == 3 * hh + t, part.astype(F32), slab)
            bias_ref[pl.ds(r * rows, rows), :] = slab.astype(BF16)
            return carry
        lax.fori_loop(0, seq // rows, fill, 0)

    q2 = q_ref[...]
    lane_q = lax.broadcasted_iota(jnp.int32, q2.shape, 1)
    qh = []
    for hh in range(2):
        q_m = jnp.where((lane_q >= HEAD_DIM * hh) & (lane_q < HEAD_DIM * (hh + 1)), q2, jnp.zeros_like(q2))
        ones = jnp.where((lane_q >= 3 * hh) & (lane_q < 3 * hh + 3), 1.0, 0.0).astype(BF16)
        qh.append(jnp.concatenate([q_m, ones], axis=1))

    def scores_to(slot, j):
        kj = jnp.concatenate([k_ref[pl.ds(j * tk, tk), :], bias_ref[pl.ds(j * tk, tk), :]], axis=1)
        bms = []
        for hh in range(2):
            s_t = _dot_nt(kj, qh[hh])
            s_ref[slot, hh] = s_t
            bms.append(jnp.max(s_t, axis=0, keepdims=True))
        return tuple(bms)

    def add_values(slot, j, acc_pair, a_pair):
        out = []
        for hh in range(2):
            acc = a_pair[hh] * acc_pair[hh]
            for t in range(sub):
                acc = acc + _dot(vt_ref[j * sub + t, HEAD_DIM * hh:HEAD_DIM * (hh + 1), :],
                                 p_ref[slot, hh, t * tq:(t + 1) * tq, :])
            out.append(acc)
        return tuple(out)

    def rescale(m_pair, bm_pair):
        m_new = tuple(jnp.maximum(m_pair[hh], bm_pair[hh]) for hh in range(2))
        return m_new, tuple(jnp.exp2(m_pair[hh] - m_new[hh]) for hh in range(2))

    def trip(j, rd, carry):
        m_pair, l_pair, acc_pair, a_prev, bm_pair = carry
        wr = 1 - rd
        acc_pair = add_values(wr, jnp.maximum(j - 1, 0), acc_pair, a_prev)
        m_new, a_pair = rescale(m_pair, bm_pair)
        l_new = []
        for hh in range(2):
            p_t = jnp.exp2(s_ref[rd, hh] - m_new[hh])
            l_new.append(a_pair[hh] * l_pair[hh] + jnp.sum(p_t, axis=0, keepdims=True))
            p_ref[rd, hh] = p_t.astype(BF16)
        bm_next = scores_to(wr, j + 1)
        return m_new, tuple(l_new), acc_pair, a_pair, bm_next

    n_full = (qi * tq) // tk

    def last(rd, carry):
        m_pair, l_pair, acc_pair, a_prev, _ = carry
        acc_pair = add_values(1 - rd, jnp.maximum(n_full - 1, 0), acc_pair, a_prev)
        kr = lax.broadcasted_iota(jnp.int32, (tk, tq), 0)
        qc = lax.broadcasted_iota(jnp.int32, (tk, tq), 1)
        visible = kr <= qc + (qi * tq - n_full * tk)
        s_pair = tuple(jnp.where(visible, s_ref[rd, hh], -1e30) for hh in range(2))
        m_new, a_pair = rescale(m_pair, tuple(jnp.max(s_t, axis=0, keepdims=True) for s_t in s_pair))
        l_fin = []
        for hh in range(2):
            p_t = jnp.exp2(s_pair[hh] - m_new[hh])
            l_fin.append(a_pair[hh] * l_pair[hh] + jnp.sum(p_t, axis=0, keepdims=True))
            p_ref[rd, hh] = p_t.astype(BF16)
        acc_pair = add_values(rd, n_full, acc_pair, a_pair)
        return jnp.concatenate([acc_pair[hh] / l_fin[hh] for hh in range(2)], axis=0)

    two = lambda f: (f(), f())
    p_ref[1] = jnp.zeros(p_ref.shape[1:], BF16)
    init = (two(lambda: jnp.full((1, tq), -1e30, F32)), two(lambda: jnp.zeros((1, tq), F32)),
            two(lambda: jnp.zeros((HEAD_DIM, tq), F32)), two(lambda: jnp.ones((1, tq), F32)), scores_to(0, 0))
    carry = lax.fori_loop(
        0, n_full, lambda j, c: lax.cond(j % 2 == 0, lambda c: trip(j, 0, c), lambda c: trip(j, 1, c), c), init)
    o_t = lax.cond(n_full % 2 == 0, lambda c: last(0, c), lambda c: last(1, c), carry)
    o_ref[...] = o_t.T.astype(BF16)


def _fox_prompt(q, k, vt, cum, batch, seq):
    tq, tk = ATTN_TILE, KEY_TILE
    assert seq % tk == 0 and tk % tq == 0 and tq == TOKEN_TILE
    nq = seq // tq
    pairs = HEADS // 2
    return pl.pallas_call(
        functools.partial(_fox_prompt_kernel, seq=seq, tq=tq, tk=tk),
        grid=(batch, pairs, nq),
        in_specs=[pl.BlockSpec((tq, LANES), lambda b, p, i: (b * nq + i, p)),
                  pl.BlockSpec((seq, LANES), lambda b, p, i: (b, p)),
                  pl.BlockSpec((nq, LANES, tq), lambda b, p, i: (b, p, 0)),
                  pl.BlockSpec((seq, LANES), lambda b, p, i: (b, 0))],
        out_specs=pl.BlockSpec((tq, LANES), lambda b, p, i: (b * nq + i, p)),
        out_shape=jax.ShapeDtypeStruct((batch * seq, WIDTH), BF16),
        scratch_shapes=[pltpu.VMEM((seq, LANES), BF16), pltpu.VMEM((2, 2, tk, tq), F32),
                        pltpu.VMEM((2, 2, tk, tq), BF16)],
        compiler_params=pltpu.CompilerParams(dimension_semantics=("arbitrary",) * 3, vmem_limit_bytes=VMEM_LIMIT),
        name="fox_prompt",
    )(q, k, vt, cum)


def _fox_sample_kernel(q_ref, kn_ref, vn_ref, small_ref, kc_ref, vc_ref, lfc_ref, tri_ref, trin_ref, o_ref,
                       *, past, steps):
    blk = tri_ref.shape[0]
    carry = jnp.zeros((1, HEADS), F32)
    cums = []
    for r in range(past // blk):
        c = _sel_dot(tri_ref[...], lfc_ref[0, r * blk:(r + 1) * blk, :]) + carry
        cums.append(c)
        carry = c[-1:, :]
    cum_c = jnp.concatenate(cums, axis=0)
    cum_n = _sel_dot(trin_ref[...], small_ref[:, L_LOGF:L_LOGF + HEADS]) + carry

    q = q_ref[...]
    kc = kc_ref[0].astype(BF16)
    vc = vc_ref[0].astype(BF16)
    kn = kn_ref[...].astype(BF16)
    vn = vn_ref[...].astype(BF16)
    lane_c = lax.broadcasted_iota(jnp.int32, (past, LANES), 1)
    lane_n = lax.broadcasted_iota(jnp.int32, (steps, LANES), 1)
    kr = lax.broadcasted_iota(jnp.int32, (steps, steps), 0)
    qc = lax.broadcasted_iota(jnp.int32, (steps, steps), 1)
    ones_c = jnp.ones((past, LANES), BF16)
    ones_n = jnp.ones((steps, LANES), BF16)
    slabs = []
    for pr in range(HEADS // 2):
        ls = slice(pr * LANES, (pr + 1) * LANES)
        q_s, kc_s, kn_s, vc_s, vn_s = q[:, ls], kc[:, ls], kn[:, ls], vc[:, ls], vn[:, ls]
        o_s = jnp.zeros((steps, LANES), F32)
        for hh in range(2):
            hd = 2 * pr + hh
            in_c = (lane_c >= HEAD_DIM * hh) & (lane_c < HEAD_DIM * (hh + 1))
            in_n = (lane_n >= HEAD_DIM * hh) & (lane_n < HEAD_DIM * (hh + 1))
            qm = jnp.where(in_n, q_s, jnp.zeros_like(q_s))
            s1 = _dot_nt(kc_s, qm) - LOG2E * cum_c[:, hd:hd + 1]
            s2 = _dot_nt(kn_s, qm) - LOG2E * cum_n[:, hd:hd + 1]
            s2 = jnp.where(kr <= qc, s2, -1e30)
            m = jnp.maximum(jnp.max(s1, axis=0, keepdims=True), jnp.max(s2, axis=0, keepdims=True))
            p1 = jnp.exp2(s1 - m).astype(BF16)
            p2 = jnp.exp2(s2 - m).astype(BF16)
            num = _dot_tn(p1, jnp.where(in_c, vc_s, jnp.zeros_like(vc_s))) + \
                _dot_tn(p2, jnp.where(in_n, vn_s, jnp.zeros_like(vn_s)))
            den = _dot_tn(p1, ones_c) + _dot_tn(p2, ones_n)
            o_s = o_s + num / den
        slabs.append(o_s)
    o_ref[...] = jnp.concatenate(slabs, axis=1).astype(BF16)


def _fox_sample(q, kn, vn, small, kc, vc, lfc, batch, steps, past):
    blk = 256
    assert past % blk == 0
    r = np.arange(blk)
    tri = jnp.asarray(r[None, :] <= r[:, None], BF16)
    rn = np.arange(steps)
    trin = jnp.asarray(rn[None, :] <= rn[:, None], BF16)
    row = lambda w: pl.BlockSpec((steps, w), lambda b: (b, 0))
    return pl.pallas_call(
        functools.partial(_fox_sample_kernel, past=past, steps=steps),
        grid=(batch,),
        in_specs=[row(WIDTH), row(WIDTH), row(WIDTH), row(LANES),
                  pl.BlockSpec((1, past, WIDTH), lambda b: (b, 0, 0)),
                  pl.BlockSpec((1, past, WIDTH), lambda b: (b, 0, 0)),
                  pl.BlockSpec((1, past, HEADS), lambda b: (b, 0, 0)),
                  _const_spec((blk, blk)), _const_spec((steps, steps))],
        out_specs=row(WIDTH),
        out_shape=jax.ShapeDtypeStruct((batch * steps, WIDTH), BF16),
        compiler_params=pltpu.CompilerParams(dimension_semantics=("arbitrary",), vmem_limit_bytes=VMEM_LIMIT),
        name="fox_sample",
    )(q, kn, vn, small, kc, vc, lfc, tri, trin)


def _gdn_kernel(gq_ref, small_ref, hist_ref, s0_ref, convw_ref, tri_ref, hsum_ref, hexp_g_ref, hexp_b_ref,
                o_ref, sfin_ref, xbuf, state, *, chunk):
    c_idx = pl.program_id(1)
    n_chunks = pl.num_programs(1)
    hist_rows = SUBLANES

    @pl.when(c_idx == 0)
    def _():
        xbuf[0:hist_rows, :] = hist_ref[0]
        state[...] = s0_ref[0]

    xbuf[hist_rows:hist_rows + chunk, :] = gq_ref[...]
    w = convw_ref[...]
    base = hist_rows - (CONV_WIDTH - 1)
    y = xbuf[base:base + chunk, :] * w[0:1, :]
    for i in range(1, CONV_WIDTH):
        y = y + xbuf[base + i:base + i + chunk, :] * w[i:i + 1, :]
    xbuf[0:hist_rows, :] = xbuf[chunk:chunk + hist_rows, :]
    y = y * _sigmoid(y)

    hsum = hsum_ref[...]
    hexp_g = hexp_g_ref[...]
    hexp_b = hexp_b_ref[...]

    def l2n(x):
        inv = lax.rsqrt(_dot_sel(x * x, hsum, terms=2) + EPS)
        return x * _dot_sel(inv, hexp_g, terms=2)

    q = l2n(y[:, 0:WIDTH]) * (HEAD_DIM ** -0.5)
    k = l2n(y[:, WIDTH:2 * WIDTH])
    v = y[:, 2 * WIDTH:3 * WIDTH]

    small = small_ref[...]
    gc = _sel_dot(tri_ref[...], small)
    gc_x = _dot_sel(gc, hexp_g)
    beta_x = _dot_sel(small, hexp_b, terms=2)
    egc_x = jnp.exp(gc_x)
    glast_x = gc_x[chunk - 1:chunk, :]
    eglast_x = jnp.exp(glast_x)
    kb = k * beta_x
    vb = v * beta_x
    kbe = kb * egc_x
    qe = q * egc_x
    kdec = k * jnp.exp(glast_x - gc_x)

    pad_rows = LANES - chunk
    gc_sq = jnp.concatenate([gc, jnp.zeros((pad_rows, LANES), F32)], axis=0) if pad_rows else gc
    gc_t = gc_sq.T

    ri = lax.broadcasted_iota(jnp.int32, (chunk, chunk), 0)
    ci = lax.broadcasted_iota(jnp.int32, (chunk, chunk), 1)
    eye = jnp.where(ri == ci, 1.0, 0.0).astype(F32)
    n_double = int(np.log2(chunk)) - 1
    assert 2 ** (n_double + 1) == chunk

    outs = []
    for hd in range(HEADS):
        sl = slice(hd * HEAD_DIM, (hd + 1) * HEAD_DIM)
        gcol = gc[:, L_G + hd:L_G + hd + 1]
        grow = gc_t[L_G + hd:L_G + hd + 1, 0:chunk]
        dec = jnp.where(ri >= ci, jnp.exp(gcol - grow), 0.0)
        k_h = k[:, sl].astype(BF16)
        a = _dot_nt(kb[:, sl].astype(BF16), k_h) * jnp.where(ri > ci, dec, 0.0)
        pw = -a
        tinv = eye + pw
        for _ in range(n_double):
            pw_b = pw.astype(BF16)
            pw = _dot(pw_b, pw_b)
            tinv = tinv + _dot(tinv.astype(BF16), pw.astype(BF16))
        rhs = jnp.concatenate([vb[:, sl], kbe[:, sl]], axis=1).astype(BF16)
        uw = _dot(tinv.astype(BF16), rhs)
        u, wm = uw[:, 0:HEAD_DIM], uw[:, HEAD_DIM:2 * HEAD_DIM]
        s_h = state[hd]
        s_b = s_h.astype(BF16)
        v_new = u - _dot(wm.astype(BF16), s_b)
        v_new_b = v_new.astype(BF16)
        qk = _dot_nt(q[:, sl].astype(BF16), k_h) * dec
        o_h = _dot(qe[:, sl].astype(BF16), s_b) + _dot(qk.astype(BF16), v_new_b)
        state[hd] = s_h * eglast_x[:, sl] + _dot_tn(kdec[:, sl].astype(BF16), v_new_b)
        outs.append(o_h)
    o_ref[...] = jnp.concatenate(outs, axis=1)

    @pl.when(c_idx == n_chunks - 1)
    def _():
        sfin_ref[0] = state[...]


def _gdn(gqkv, small, hist, s0, conv_w, batch, seq, chunk):
    assert seq % chunk == 0 and chunk % SUBLANES == 0 and chunk <= LANES
    nc = seq // chunk
    r = np.arange(chunk)
    tri = jnp.asarray(r[None, :] <= r[:, None], BF16)
    head_of = np.arange(WIDTH) // HEAD_DIM
    hsum = jnp.asarray(head_of[:, None] == (np.arange(LANES)[None, :] - L_G), BF16)
    hexp_g = jnp.asarray((np.arange(LANES)[:, None] - L_G) == head_of[None, :], BF16)
    hexp_b = jnp.asarray((np.arange(LANES)[:, None] - L_BETA) == head_of[None, :], BF16)
    conv_w8 = jnp.concatenate([conv_w, jnp.zeros((SUBLANES - CONV_WIDTH, CONV_DIM), F32)], axis=0)
    return pl.pallas_call(
        functools.partial(_gdn_kernel, chunk=chunk),
        grid=(batch, nc),
        in_specs=[pl.BlockSpec((chunk, CONV_DIM), lambda b, c: (b * nc + c, 0)),
                  pl.BlockSpec((chunk, LANES), lambda b, c: (b * nc + c, 0)),
                  pl.BlockSpec((1, SUBLANES, CONV_DIM), lambda b, c: (b, 0, 0)),
                  pl.BlockSpec((1, HEADS, HEAD_DIM, HEAD_DIM), lambda b, c: (b, 0, 0, 0)),
                  _const_spec((SUBLANES, CONV_DIM)), _const_spec((chunk, chunk)),
                  _const_spec((WIDTH, LANES)), _const_spec((LANES, WIDTH)), _const_spec((LANES, WIDTH))],
        out_specs=(pl.BlockSpec((chunk, WIDTH), lambda b, c: (b * nc + c, 0)),
                   pl.BlockSpec((1, HEADS, HEAD_DIM, HEAD_DIM), lambda b, c: (b, 0, 0, 0))),
        out_shape=(jax.ShapeDtypeStruct((batch * seq, WIDTH), F32),
                   jax.ShapeDtypeStruct((batch, HEADS, HEAD_DIM, HEAD_DIM), F32)),
        scratch_shapes=[pltpu.VMEM((chunk + SUBLANES, CONV_DIM), F32),
                        pltpu.VMEM((HEADS, HEAD_DIM, HEAD_DIM), F32)],
        compiler_params=pltpu.CompilerParams(dimension_semantics=("arbitrary", "arbitrary"),
                                             vmem_limit_bytes=VMEM_LIMIT),
        name="gdn",
    )(gqkv, small, hist, s0, conv_w8, tri, hsum, hexp_g, hexp_b)


GROUP = 4
GROUP_W = GROUP * HEAD_DIM


def _stack_heads(x, lane):
    return jnp.concatenate([jnp.where(lane == hh, x, jnp.zeros_like(x)) for hh in range(GROUP)], axis=0)


def _fold_heads(x, rows):
    out = x[0:rows]
    for hh in range(1, GROUP):
        out = out + x[hh * rows:(hh + 1) * rows]
    return out


def _gdn_local_kernel(gq_ref, prev_ref, small_ref, convw_ref, tri_ref, hsum_ref, hexp_g_ref, hexp_b_ref,
                      u_ref, w_ref, qe_ref, kdec_ref, qkw_ref, egl_ref, xbuf, *, chunk):
    i = pl.program_id(1)
    tm = gq_ref.shape[0]
    hist_rows = SUBLANES
    prev = prev_ref[...]
    xbuf[0:hist_rows, :] = jnp.where(i == 0, jnp.zeros_like(prev), prev)
    xbuf[hist_rows:hist_rows + tm, :] = gq_ref[...]
    w = convw_ref[...]
    base = hist_rows - (CONV_WIDTH - 1)
    y = xbuf[base:base + tm, :] * w[0:1, :]
    for t in range(1, CONV_WIDTH):
        y = y + xbuf[base + t:base + t + tm, :] * w[t:t + 1, :]
    y = y * _sigmoid(y)

    hsum = hsum_ref[...]
    hexp_g = hexp_g_ref[...]
    hexp_b = hexp_b_ref[...]

    def l2n(x):
        inv = lax.rsqrt(_dot_sel(x * x, hsum, terms=2) + EPS)
        return x * _dot_sel(inv, hexp_g, terms=2)

    q = l2n(y[:, 0:WIDTH]) * (HEAD_DIM ** -0.5)
    k = l2n(y[:, WIDTH:2 * WIDTH])
    v = y[:, 2 * WIDTH:3 * WIDTH]

    small = small_ref[...]
    gc = _sel_dot(tri_ref[...], small)
    gc_x = _dot_sel(gc, hexp_g)
    beta_x = _dot_sel(small, hexp_b, terms=2)
    n_chunks = tm // chunk
    lasts = [gc_x[(c + 1) * chunk - 1:(c + 1) * chunk, :] for c in range(n_chunks)]
    glast_x = jnp.concatenate([jnp.broadcast_to(r, (chunk, WIDTH)) for r in lasts], axis=0)
    egc_x = jnp.exp(gc_x)
    kb = k * beta_x
    vb = (v * beta_x).astype(BF16)
    kbe = (kb * egc_x).astype(BF16)
    kb = kb.astype(BF16)
    qe_ref[...] = (q * egc_x).astype(BF16)
    kdec_ref[...] = (k * jnp.exp(glast_x - gc_x)).astype(BF16)
    for c in range(n_chunks):
        egl_ref[c] = jnp.broadcast_to(jnp.exp(lasts[c]), (SUBLANES, WIDTH))
    q = q.astype(BF16)
    k = k.astype(BF16)

    rows = GROUP * chunk
    lane = lax.broadcasted_iota(jnp.int32, (chunk, GROUP_W), 1)
    head_of_lane = lane // HEAD_DIM
    ri = lax.broadcasted_iota(jnp.int32, (rows, rows), 0)
    ci = lax.broadcasted_iota(jnp.int32, (rows, rows), 1)
    same = (ri // chunk) == (ci // chunk)
    incl = same & (ri >= ci)
    strict = same & (ri > ci)
    eye = jnp.where(ri == ci, 1.0, 0.0).astype(F32)
    n_terms = 3
    n_double = int(np.log2(chunk)) - 1

    for c in range(n_chunks):
        rs = slice(c * chunk, (c + 1) * chunk)
        for g in range(HEADS // GROUP):
            ls = slice(g * GROUP_W, (g + 1) * GROUP_W)
            stack = lambda x: _stack_heads(x[rs, ls], head_of_lane)
            k4 = stack(k)
            left, right = [], []
            for hh in range(GROUP):
                col = gc[rs, L_G + g * GROUP + hh:L_G + g * GROUP + hh + 1]
                l_blk = jnp.where((lane >= n_terms) & (lane < 2 * n_terms), 1.0, 0.0).astype(F32)
                r_blk = jnp.where(lane < n_terms, 1.0, 0.0).astype(F32)
                for t, part in enumerate(_split3(col)):
                    l_blk = jnp.where(lane == t, part.astype(F32), l_blk)
                    r_blk = jnp.where(lane == n_terms + t, -part.astype(F32), r_blk)
                left.append(l_blk)
                right.append(r_blk)
            diff = _dot_nt(jnp.concatenate(left, axis=0).astype(BF16), jnp.concatenate(right, axis=0).astype(BF16))
            decay = jnp.exp(jnp.where(incl, diff, -1e30))
            aq = _dot_nt(jnp.concatenate([stack(kb), stack(q)], axis=0), k4)
            a = aq[0:rows] * jnp.where(strict, decay, 0.0)
            qk = aq[rows:2 * rows] * decay
            pw = -a
            tinv = eye + pw
            pw_b = pw.astype(BF16)
            pw = _dot(pw_b, pw_b)
            for it in range(n_double):
                pw_b = pw.astype(BF16)
                if it < n_double - 1:
                    both = _dot(jnp.concatenate([pw_b, tinv.astype(BF16)], axis=0), pw_b)
                    pw = both[0:rows]
                    tinv = tinv + both[rows:2 * rows]
                else:
                    tinv = tinv + _dot(tinv.astype(BF16), pw_b)
            uw = _dot(tinv.astype(BF16), jnp.concatenate([stack(vb), stack(kbe)], axis=1))
            u_ref[rs, ls] = _fold_heads(uw[:, 0:GROUP_W], chunk).astype(BF16)
            w_ref[rs, ls] = _fold_heads(uw[:, GROUP_W:2 * GROUP_W], chunk).astype(BF16)
            qkw_ref[rs, ls] = _fold_heads(qk, chunk).astype(BF16)


def _gdn_scan_kernel(u_ref, w_ref, qe_ref, kdec_ref, qkw_ref, egl_ref, o_ref, sfin_ref, state, *, chunk):
    c_idx = pl.program_id(0)
    batch = u_ref.shape[0]

    @pl.when(c_idx == 0)
    def _():
        state[...] = jnp.zeros_like(state)

    rows = GROUP * chunk
    lane = lax.broadcasted_iota(jnp.int32, (chunk, GROUP_W), 1) // HEAD_DIM
    ri = lax.broadcasted_iota(jnp.int32, (GROUP_W, GROUP_W), 0)
    ci = lax.broadcasted_iota(jnp.int32, (GROUP_W, GROUP_W), 1)
    same = (ri // HEAD_DIM) == (ci // HEAD_DIM)
    for b in range(batch):
        for g in range(HEADS // GROUP):
            ls = slice(g * GROUP_W, (g + 1) * GROUP_W)
            s_f = state[b, g]
            s_b = s_f.astype(BF16)
            v_new = u_ref[b, :, ls].astype(F32) - _dot(w_ref[b, :, ls], s_b)
            v_new_b = v_new.astype(BF16)
            vn_bd = _stack_heads(v_new_b, lane)
            o_ref[b, :, ls] = _dot(jnp.concatenate([qe_ref[b, :, ls], qkw_ref[b, :, ls]], axis=1),
                                   jnp.concatenate([s_b, vn_bd], axis=0))
            kv = _dot_tn(kdec_ref[b, :, ls], v_new_b)
            state[b, g] = s_f * egl_ref[b, 0, 0:1, ls] + jnp.where(same, kv, 0.0)

    @pl.when(c_idx == pl.num_programs(0) - 1)
    def _():
        for b in range(batch):
            for hd in range(HEADS):
                g, hh = divmod(hd, GROUP)
                sl = slice(hh * HEAD_DIM, (hh + 1) * HEAD_DIM)
                sfin_ref[b, hd] = state[b, g, sl, sl]


def _gdn_long(gqkv, small, conv_w, batch, seq, chunk):
    tm = TOKEN_TILE
    assert seq % tm == 0 and tm % chunk == 0 and GROUP * chunk == GROUP_W
    nt = seq // tm
    nc = seq // chunk
    r = np.arange(tm)
    tri = jnp.asarray((r[None, :] <= r[:, None]) & ((r[None, :] // chunk) == (r[:, None] // chunk)), BF16)
    head_of = np.arange(WIDTH) // HEAD_DIM
    hsum = jnp.asarray(head_of[:, None] == (np.arange(LANES)[None, :] - L_G), BF16)
    hexp_g = jnp.asarray((np.arange(LANES)[:, None] - L_G) == head_of[None, :], BF16)
    hexp_b = jnp.asarray((np.arange(LANES)[:, None] - L_BETA) == head_of[None, :], BF16)
    conv_w8 = jnp.concatenate([conv_w, jnp.zeros((SUBLANES - CONV_WIDTH, CONV_DIM), F32)], axis=0)
    row = lambda w: pl.BlockSpec((tm, w), lambda b, i: (b * nt + i, 0))
    per_tile = tm // chunk
    tok = jax.ShapeDtypeStruct((batch * seq, WIDTH), BF16)
    u, w, qe, kdec, qkw, egl = pl.pallas_call(
        functools.partial(_gdn_local_kernel, chunk=chunk),
        grid=(batch, nt),
        in_specs=[row(CONV_DIM),
                  pl.BlockSpec((SUBLANES, CONV_DIM),
                               lambda b, i: (jnp.maximum((b * nt + i) * (tm // SUBLANES) - 1, 0), 0)),
                  row(LANES), _const_spec((SUBLANES, CONV_DIM)), _const_spec((tm, tm)),
                  _const_spec((WIDTH, LANES)), _const_spec((LANES, WIDTH)), _const_spec((LANES, WIDTH))],
        out_specs=(row(WIDTH), row(WIDTH), row(WIDTH), row(WIDTH), row(WIDTH),
                   pl.BlockSpec((per_tile, SUBLANES, WIDTH), lambda b, i: (b * nt + i, 0, 0))),
        out_shape=(tok, tok, tok, tok, tok, jax.ShapeDtypeStruct((batch * nc, SUBLANES, WIDTH), F32)),
        scratch_shapes=[pltpu.VMEM((tm + SUBLANES, CONV_DIM), F32)],
        compiler_params=pltpu.CompilerParams(dimension_semantics=("arbitrary", "arbitrary"),
                                             vmem_limit_bytes=VMEM_LIMIT),
        name="gdn_local",
    )(gqkv, gqkv, small, conv_w8, tri, hsum, hexp_g, hexp_b)

    blk = pl.BlockSpec((batch, chunk, WIDTH), lambda c: (0, c, 0))
    as3 = lambda a: a.reshape(batch, seq, WIDTH)
    o, s_fin = pl.pallas_call(
        functools.partial(_gdn_scan_kernel, chunk=chunk),
        grid=(nc,),
        in_specs=[blk, blk, blk, blk, blk,
                  pl.BlockSpec((batch, 1, SUBLANES, WIDTH), lambda c: (0, c, 0, 0))],
        out_specs=(blk, pl.BlockSpec((batch, HEADS, HEAD_DIM, HEAD_DIM), lambda c: (0, 0, 0, 0))),
        out_shape=(jax.ShapeDtypeStruct((batch, seq, WIDTH), F32),
                   jax.ShapeDtypeStruct((batch, HEADS, HEAD_DIM, HEAD_DIM), F32)),
        scratch_shapes=[pltpu.VMEM((batch, HEADS // GROUP, GROUP_W, GROUP_W), F32)],
        compiler_params=pltpu.CompilerParams(dimension_semantics=("arbitrary",), vmem_limit_bytes=VMEM_LIMIT),
        name="gdn_scan",
    )(as3(u), as3(w), as3(qe), as3(kdec), as3(qkw), egl.reshape(batch, nc, SUBLANES, WIDTH))
    return o.reshape(batch * seq, WIDTH), s_fin


def _post_kernel(x_ref, fox_ref, gdn_ref, sgg_ref, sga_ref, sgb_ref,
                 wpa_ref, wpb_ref, wout_ref, wup_ref, wdown_ref,
                 ng_ref, gpost_ref, gpre2_ref, gpost2_ref, hsum_ref, hexp_ref, y_ref):
    ya = _dot(fox_ref[...], wpa_ref[...])
    o = gdn_ref[...]
    ms = _dot_sel(o * o, hsum_ref[...], terms=2) * (1.0 / HEAD_DIM)
    o = o * _dot_sel(lax.rsqrt(ms + EPS), hexp_ref[...], terms=2) * ng_ref[...] * sgg_ref[...].astype(F32)
    yb = _dot(o.astype(BF16), wpb_ref[...])
    m = sga_ref[...].astype(F32) * ya + sgb_ref[...].astype(F32) * yb
    mix = _dot(m.astype(BF16), wout_ref[...])
    y1 = x_ref[...] + _rms(mix, gpost_ref[...])
    h2 = _rms(y1, gpre2_ref[...]).astype(BF16)
    acc = jnp.zeros(y1.shape, F32)
    for c in range(D_FF // FF_TILE):
        u = jnp.maximum(_dot(h2, wup_ref[:, c * FF_TILE:(c + 1) * FF_TILE]), 0.0)
        acc = acc + _dot((u * u).astype(BF16), wdown_ref[c * FF_TILE:(c + 1) * FF_TILE, :])
    y_ref[...] = y1 + _rms(acc, gpost2_ref[...])


def _post(x, fox, gdn_o, sgg, sga, sgb, w_pa, w_pb, w_out, w_up, w_down, ng, gpost, gpre2, gpost2):
    t_total = x.shape[0]
    tm = TOKEN_TILE
    row = lambda w: pl.BlockSpec((tm, w), lambda i: (i, 0))
    head_of = np.arange(WIDTH) // HEAD_DIM
    hsum = jnp.asarray(head_of[:, None] == (np.arange(LANES)[None, :] - L_G), BF16)
    hexp = jnp.asarray((np.arange(LANES)[:, None] - L_G) == head_of[None, :], BF16)
    return pl.pallas_call(
        _post_kernel,
        grid=(t_total // tm,),
        in_specs=[row(D_MODEL), row(WIDTH), row(WIDTH), row(WIDTH), row(D_MODEL), row(D_MODEL),
                  _const_spec((WIDTH, D_MODEL)), _const_spec((WIDTH, D_MODEL)), _const_spec((D_MODEL, D_MODEL)),
                  _const_spec((D_MODEL, D_FF)), _const_spec((D_FF, D_MODEL)),
                  _const_spec((1, WIDTH)), _const_spec((1, D_MODEL)), _const_spec((1, D_MODEL)),
                  _const_spec((1, D_MODEL)), _const_spec((WIDTH, LANES)), _const_spec((LANES, WIDTH))],
        out_specs=row(D_MODEL),
        out_shape=jax.ShapeDtypeStruct((t_total, D_MODEL), F32),
        compiler_params=pltpu.CompilerParams(dimension_semantics=("arbitrary",), vmem_limit_bytes=VMEM_LIMIT),
        name="post",
    )(x, fox, gdn_o, sgg, sga, sgb, w_pa, w_pb, w_out, w_up, w_down, ng, gpost, gpre2, gpost2, hsum, hexp)


def _rearrange_w_in(w_in):
    o_ff = 3 * WIDTH
    o_gqkv = o_ff + HEADS
    o_ga = o_gqkv + CONV_DIM
    o_gb = o_ga + HEADS
    o_gg = o_gb + HEADS
    o_gate_a = o_gg + WIDTH
    o_gate_b = o_gate_a + D_MODEL
    cols = [w_in[:, 0:o_ff], w_in[:, o_gqkv:o_ga], w_in[:, o_gg:o_gate_a], w_in[:, o_gate_a:o_gate_b],
            w_in[:, o_gate_b:o_gate_b + D_MODEL], w_in[:, o_ff:o_gqkv], w_in[:, o_ga:o_gb], w_in[:, o_gb:o_gg],
            jnp.zeros((D_MODEL, LANES - 3 * HEADS), w_in.dtype)]
    return jnp.concatenate(cols, axis=1).astype(BF16)


def _lane_row(*pieces):
    v = jnp.concatenate([p.astype(F32) for p in pieces])
    return jnp.concatenate([v, jnp.zeros((LANES - v.shape[0],), F32)])[None, :]


def _pad_hist(conv_cache):
    b = conv_cache.shape[0]
    return jnp.concatenate([jnp.zeros((b, SUBLANES - (CONV_WIDTH - 1), CONV_DIM), F32), conv_cache.astype(F32)], axis=1)


def kernel(x_prompt, x_sample, cache_fox_k, cache_fox_v, cache_fox_logf, state_gdn, state_gdn_conv,
           w_in, fox_forget_bias, gdn_conv_w, gdn_a_log, gdn_dt_bias, gdn_norm_g,
           w_proj_fox, w_proj_gdn, w_out, norm_mix_pre, norm_mix_post, norm_mlp_pre, norm_mlp_post,
           w_up, w_down):
    depth = w_in.shape[0]
    bp, sp, _ = x_prompt.shape
    bs, ss, _ = x_sample.shape
    past = cache_fox_k.shape[2]
    y_p = x_prompt.reshape(bp * sp, D_MODEL)
    y_s = x_sample.reshape(bs * ss, D_MODEL)
    st_p, st_s = [], []
    for l in range(depth):
        w_r = _rearrange_w_in(w_in[l])
        padd = _lane_row(fox_forget_bias[l], gdn_dt_bias[l])
        palog = _lane_row(jnp.zeros((HEADS,), F32), gdn_a_log[l])
        gain = norm_mix_pre[l][None, :]
        post_w = (w_proj_fox[l].astype(BF16), w_proj_gdn[l].astype(BF16), w_out[l].astype(BF16),
                  w_up[l].astype(BF16), w_down[l].astype(BF16),
                  jnp.tile(gdn_norm_g[l], HEADS)[None, :], norm_mix_post[l][None, :],
                  norm_mlp_pre[l][None, :], norm_mlp_post[l][None, :])

        q, k, vt, fk, fv, gqkv, sgg, sga, sgb, small, cum = _in_proj(y_p, gain, w_r, padd, palog, sp)
        fox = _fox_prompt(q, k, vt, cum, bp, sp)
        gdn_o, s_fin = _gdn_long(gqkv, small, gdn_conv_w[l], bp, sp, GDN_CHUNK)
        y_p_new = _post(y_p, fox, gdn_o, sgg, sga, sgb, *post_w)
        st_p.append((fk.reshape(bp, sp, HEADS, HEAD_DIM), fv.reshape(bp, sp, HEADS, HEAD_DIM),
                     small[:, L_LOGF:L_LOGF + HEADS].reshape(bp, sp, HEADS), s_fin,
                     gqkv.reshape(bp, sp, CONV_DIM)[:, sp - (CONV_WIDTH - 1):, :]))
        y_p = y_p_new

        q, k, vt, fk, fv, gqkv, sgg, sga, sgb, small, cum = _in_proj(y_s, gain, w_r, padd, palog, ss)
        fox = _fox_sample(q, fk, fv, small,
                          cache_fox_k[l].reshape(bs, past, WIDTH), cache_fox_v[l].reshape(bs, past, WIDTH),
                          cache_fox_logf[l], bs, ss, past)
        gdn_o, s_new = _gdn(gqkv, small, _pad_hist(state_gdn_conv[l]), state_gdn[l].astype(F32),
                            gdn_conv_w[l], bs, ss, ss)
        y_s_new = _post(y_s, fox, gdn_o, sgg, sga, sgb, *post_w)
        conv_ext = jnp.concatenate([state_gdn_conv[l].astype(F32), gqkv.reshape(bs, ss, CONV_DIM)], axis=1)
        st_s.append((fk.reshape(bs, ss, HEADS, HEAD_DIM), fv.reshape(bs, ss, HEADS, HEAD_DIM),
                     small[:, L_LOGF:L_LOGF + HEADS].reshape(bs, ss, HEADS), s_new,
                     conv_ext[:, conv_ext.shape[1] - (CONV_WIDTH - 1):, :]))
        y_s = y_s_new

    fk_p, fv_p, lf_p, sg_p, cv_p = [jnp.stack(a) for a in zip(*st_p)]
    fk_s, fv_s, lf_s, sg_s, cv_s = [jnp.stack(a) for a in zip(*st_s)]
    return (y_p.reshape(bp, sp, D_MODEL), y_s.reshape(bs, ss, D_MODEL),
            fk_p, fv_p, lf_p, sg_p, cv_p, fk_s, fv_s, lf_s, sg_s, cv_s)
```

```python
import functools

import jax
import jax.numpy as jnp
import numpy as np
from jax import lax
from jax.experimental import pallas as pl
from jax.experimental.pallas import tpu as pltpu

F32 = jnp.float32
BF16 = jnp.bfloat16

D_MODEL = 1024
HEADS = 8
HEAD_DIM = 64
WIDTH = HEADS * HEAD_DIM
CONV_DIM = 3 * WIDTH
CONV_WIDTH = 4
D_FF = 4 * D_MODEL
EPS = 1e-6
LOG2E = 1.4426950408889634

LANES = 128
SUBLANES = 8
TOKEN_TILE = 256
ATTN_TILE = 256
ATTN_HEADS = 4
KEY_TILE = 512
FF_TILE = 1024
GDN_CHUNK = 64
VMEM_LIMIT = 56 * 1024 * 1024

C_Q, C_K, C_V, C_GQKV, C_GG, C_GA, C_GB, C_SMALL = 0, 512, 1024, 1536, 3072, 3584, 4608, 5632
IN_COLS = C_SMALL + LANES
L_LOGF, L_G, L_BETA = 0, 8, 16


def _split3(x):
    hi = x.astype(BF16)
    r = x - hi.astype(F32)
    mid = r.astype(BF16)
    lo = (r - mid.astype(F32)).astype(BF16)
    return hi, mid, lo


def _dot(a, b):
    return jnp.dot(a, b, preferred_element_type=F32)


def _dot_nt(a, b):
    return lax.dot_general(a, b, (((1,), (1,)), ((), ())), preferred_element_type=F32)


def _dot_tn(a, b):
    return lax.dot_general(a, b, (((0,), (0,)), ((), ())), preferred_element_type=F32)


def _sel_dot(sel, x, terms=3):
    parts = _split3(x)[:terms]
    out = _dot(sel, parts[0])
    for p in parts[1:]:
        out = out + _dot(sel, p)
    return out


def _dot_sel(x, sel, terms=3):
    parts = _split3(x)[:terms]
    out = _dot(parts[0], sel)
    for p in parts[1:]:
        out = out + _dot(p, sel)
    return out


def _sigmoid(x):
    return 1.0 / (1.0 + jnp.exp(-x))


def _rms(x, gain):
    ms = jnp.mean(x * x, axis=-1, keepdims=True)
    return x * lax.rsqrt(ms + EPS) * gain


def _const_spec(shape):
    nd = len(shape)
    return pl.BlockSpec(shape, lambda *_: (0,) * nd)


def _inproj_kernel(x_ref, gain_ref, w_ref, padd_ref, palog_ref, tri_ref,
                   q_ref, k_ref, vt_ref, fk_ref, fv_ref, gqkv_ref, sgg_ref, sga_ref, sgb_ref,
                   small_ref, cum_ref, carry_ref, *, tiles_per_seg):
    i = pl.program_id(0)
    h = _rms(x_ref[...], gain_ref[...]).astype(BF16)

    def mm(c0, width):
        return _dot(h, w_ref[:, c0:c0 + width])

    zq = mm(C_Q, WIDTH)
    q_ref[...] = (zq * (LOG2E * HEAD_DIM ** -0.5)).astype(BF16)
    zk = mm(C_K, WIDTH)
    fk_ref[...] = zk
    k_ref[...] = zk.astype(BF16)
    zv = mm(C_V, WIDTH)
    fv_ref[...] = zv
    vt_ref[0] = zv.T.astype(BF16)
    for c in range(3):
        gqkv_ref[:, c * WIDTH:(c + 1) * WIDTH] = mm(C_GQKV + c * WIDTH, WIDTH)
    zg = mm(C_GG, WIDTH)
    sgg_ref[...] = (zg * _sigmoid(zg)).astype(BF16)
    for c in range(2):
        sga_ref[:, c * WIDTH:(c + 1) * WIDTH] = _sigmoid(mm(C_GA + c * WIDTH, WIDTH)).astype(BF16)
        sgb_ref[:, c * WIDTH:(c + 1) * WIDTH] = _sigmoid(mm(C_GB + c * WIDTH, WIDTH)).astype(BF16)

    zs = mm(C_SMALL, LANES)
    t = zs + padd_ref[...]
    l1p = jnp.log1p(jnp.exp(-jnp.abs(t)))
    logf = jnp.minimum(t, 0.0) - l1p
    g = -jnp.exp(palog_ref[...]) * (jnp.maximum(t, 0.0) + l1p)
    beta = _sigmoid(zs)
    lane = lax.broadcasted_iota(jnp.int32, zs.shape, 1)
    small = jnp.where(lane < L_G, logf, jnp.where(lane < L_BETA, g, jnp.where(lane < L_BETA + HEADS, beta, 0.0)))
    small_ref[...] = small

    cum = _sel_dot(tri_ref[...], small)
    if tiles_per_seg > 1:
        @pl.when(i % tiles_per_seg == 0)
        def _():
            carry_ref[...] = jnp.zeros_like(carry_ref)
        cum = cum + carry_ref[0:1, :]
        carry_ref[...] = jnp.broadcast_to(cum[-1:, :], carry_ref.shape)
    cum_ref[...] = cum


def _in_proj(x, gain, w_r, padd, palog, seg_len):
    t_total = x.shape[0]
    tm = TOKEN_TILE
    assert t_total % tm == 0
    nt = t_total // tm
    if seg_len >= tm:
        assert seg_len % tm == 0
        tiles_per_seg = seg_len // tm
        r = np.arange(tm)
        tri = (r[None, :] <= r[:, None])
    else:
        assert tm % seg_len == 0
        tiles_per_seg = 1
        r = np.arange(tm)
        tri = (r[None, :] <= r[:, None]) & ((r[None, :] // seg_len) == (r[:, None] // seg_len))
    tri = jnp.asarray(tri, BF16)

    row = lambda w: pl.BlockSpec((tm, w), lambda i: (i, 0))
    out_shape = (
        jax.ShapeDtypeStruct((t_total, WIDTH), BF16),
        jax.ShapeDtypeStruct((t_total, WIDTH), BF16),
        jax.ShapeDtypeStruct((nt, WIDTH, tm), BF16),
        jax.ShapeDtypeStruct((t_total, WIDTH), F32),
        jax.ShapeDtypeStruct((t_total, WIDTH), F32),
        jax.ShapeDtypeStruct((t_total, CONV_DIM), F32),
        jax.ShapeDtypeStruct((t_total, WIDTH), BF16),
        jax.ShapeDtypeStruct((t_total, D_MODEL), BF16),
        jax.ShapeDtypeStruct((t_total, D_MODEL), BF16),
        jax.ShapeDtypeStruct((t_total, LANES), F32),
        jax.ShapeDtypeStruct((t_total, LANES), F32),
    )
    out_specs = (row(WIDTH), row(WIDTH), pl.BlockSpec((1, WIDTH, tm), lambda i: (i, 0, 0)),
                 row(WIDTH), row(WIDTH), row(CONV_DIM), row(WIDTH), row(D_MODEL), row(D_MODEL),
                 row(LANES), row(LANES))
    return pl.pallas_call(
        functools.partial(_inproj_kernel, tiles_per_seg=tiles_per_seg),
        grid=(nt,),
        in_specs=[row(D_MODEL), _const_spec((1, D_MODEL)), _const_spec((D_MODEL, IN_COLS)),
                  _const_spec((1, LANES)), _const_spec((1, LANES)), _const_spec((tm, tm))],
        out_specs=out_specs,
        out_shape=out_shape,
        scratch_shapes=[pltpu.VMEM((SUBLANES, LANES), F32)],
        compiler_params=pltpu.CompilerParams(dimension_semantics=("arbitrary",), vmem_limit_bytes=VMEM_LIMIT),
        name="in_proj",
    )(x, gain, w_r, padd, palog, tri)


def _fox_prompt_kernel(q_ref, k_ref, vt_ref, cum_ref, o_ref, bias_ref, s_ref, p_ref, *, seq, tq, tk, nh):
    grp = pl.program_id(1)
    qi = pl.program_id(2)
    rows = 512
    vt_tile = vt_ref.shape[2]
    sub = tk // vt_tile
    heads = range(nh)

    @pl.when(qi == 0)
    def _prep():
        def fill(r, carry):
            c = cum_ref[pl.ds(r * rows, rows), :]
            lane = lax.broadcasted_iota(jnp.int32, c.shape, 1)
            slab = jnp.zeros(c.shape, F32)
            for hh in heads:
                col = jnp.sum(jnp.where(lane == nh * grp + hh, c, 0.0), axis=-1, keepdims=True) * (-LOG2E)
                for t, part in enumerate(_split3(col)):
                    slab = jnp.where(lane == 3 * hh + t, part.astype(F32), slab)
            bias_ref[pl.ds(r * rows, rows), :] = slab.astype(BF16)
            return carry
        lax.fori_loop(0, seq // rows, fill, 0)

    lane_q = lax.broadcasted_iota(jnp.int32, (tq, LANES), 1)
    qh = []
    for hh in heads:
        q2 = q_ref[:, (hh // 2) * LANES:(hh // 2 + 1) * LANES]
        half = hh % 2
        q_m = jnp.where((lane_q >= HEAD_DIM * half) & (lane_q < HEAD_DIM * (half + 1)), q2, jnp.zeros_like(q2))
        ones = jnp.where((lane_q >= 3 * hh) & (lane_q < 3 * hh + 3), 1.0, 0.0).astype(BF16)
        qh.append(jnp.concatenate([q_m, ones], axis=1))

    def scores_to(slot, j):
        bias = bias_ref[pl.ds(j * tk, tk), :]
        kjs = [jnp.concatenate([k_ref[pl.ds(j * tk, tk), sl * LANES:(sl + 1) * LANES], bias], axis=1)
               for sl in range(nh // 2)]
        s_ts = [_dot_nt(kjs[hh // 2], qh[hh]) for hh in heads]
        for hh in heads:
            s_ref[slot, hh] = s_ts[hh]
        return tuple(jnp.max(s_t, axis=0, keepdims=True) for s_t in s_ts)

    def add_values(slot, j, accs, alphas):
        pvs = [sum(_dot(vt_ref[j * sub + t, HEAD_DIM * hh:HEAD_DIM * (hh + 1), :],
                        p_ref[slot, hh, t * vt_tile:(t + 1) * vt_tile, :]) for t in range(sub)) for hh in heads]
        return tuple(alphas[hh] * accs[hh] + pvs[hh] for hh in heads)

    def rescale(ms, bms):
        m_new = tuple(jnp.maximum(ms[hh], bms[hh]) for hh in heads)
        return m_new, tuple(jnp.exp2(ms[hh] - m_new[hh]) for hh in heads)

    def probabilities(slot, hh, m_new, visible_from):
        s_t = s_ref[slot, hh]
        if visible_from is not None:
            s_t = jnp.where(visible_from, s_t, -1e30)
        p_t = jnp.exp2(s_t - m_new)
        p_ref[slot, hh] = p_t.astype(BF16)
        return jnp.sum(p_t, axis=0, keepdims=True)

    def trip(j, rd, carry):
        ms, ls, accs, a_prev, bms = carry
        wr = 1 - rd
        accs = add_values(wr, jnp.maximum(j - 1, 0), accs, a_prev)
        m_new, alphas = rescale(ms, bms)
        l_new = tuple(alphas[hh] * ls[hh] + probabilities(rd, hh, m_new[hh], None) for hh in heads)
        bm_next = scores_to(wr, j + 1)
        return m_new, l_new, accs, alphas, bm_next

    n_full = (qi * tq) // tk

    def last(rd, carry):
        ms, ls, accs, a_prev, _ = carry
        accs = add_values(1 - rd, jnp.maximum(n_full - 1, 0), accs, a_prev)
        kr = lax.broadcasted_iota(jnp.int32, (tk, tq), 0)
        qc = lax.broadcasted_iota(jnp.int32, (tk, tq), 1)
        visible = kr <= qc + (qi * tq - n_full * tk)
        bms = tuple(jnp.max(jnp.where(visible, s_ref[rd, hh], -1e30), axis=0, keepdims=True) for hh in heads)
        m_new, alphas = rescale(ms, bms)
        l_fin = tuple(alphas[hh] * ls[hh] + probabilities(rd, hh, m_new[hh], visible) for hh in heads)
        accs = add_values(rd, n_full, accs, alphas)
        return jnp.concatenate([accs[hh] / l_fin[hh] for hh in heads], axis=0)

    each = lambda f: tuple(f() for _ in heads)
    p_ref[1] = jnp.zeros(p_ref.shape[1:], BF16)
    init = (each(lambda: jnp.full((1, tq), -1e30, F32)), each(lambda: jnp.zeros((1, tq), F32)),
            each(lambda: jnp.zeros((HEAD_DIM, tq), F32)), each(lambda: jnp.ones((1, tq), F32)), scores_to(0, 0))
    carry = lax.fori_loop(
        0, n_full, lambda j, c: lax.cond(j % 2 == 0, lambda c: trip(j, 0, c), lambda c: trip(j, 1, c), c), init)
    o_t = lax.cond(n_full % 2 == 0, lambda c: last(0, c), lambda c: last(1, c), carry)
    o_ref[...] = o_t.T.astype(BF16)


def _fox_prompt(q, k, vt, cum, batch, seq):
    tq, tk, nh = ATTN_TILE, KEY_TILE, ATTN_HEADS
    assert seq % tk == 0 and tk % tq == 0 and tk % TOKEN_TILE == 0 and HEADS % nh == 0 and nh % 2 == 0
    nq = seq // tq
    width = nh * HEAD_DIM
    return pl.pallas_call(
        functools.partial(_fox_prompt_kernel, seq=seq, tq=tq, tk=tk, nh=nh),
        grid=(batch, HEADS // nh, nq),
        in_specs=[pl.BlockSpec((tq, width), lambda b, g, i: (b * nq + i, g)),
                  pl.BlockSpec((seq, width), lambda b, g, i: (b, g)),
                  pl.BlockSpec((seq // TOKEN_TILE, width, TOKEN_TILE), lambda b, g, i: (b, g, 0)),
                  pl.BlockSpec((seq, LANES), lambda b, g, i: (b, 0))],
        out_specs=pl.BlockSpec((tq, width), lambda b, g, i: (b * nq + i, g)),
        out_shape=jax.ShapeDtypeStruct((batch * seq, WIDTH), BF16),
        scratch_shapes=[pltpu.VMEM((seq, LANES), BF16), pltpu.VMEM((2, nh, tk, tq), F32),
                        pltpu.VMEM((2, nh, tk, tq), BF16)],
        compiler_params=pltpu.CompilerParams(dimension_semantics=("arbitrary",) * 3, vmem_limit_bytes=VMEM_LIMIT),
        name="fox_prompt",
    )(q, k, vt, cum)


def _fox_sample_kernel(q_ref, kn_ref, vn_ref, small_ref, kc_ref, vc_ref, lfc_ref, tri_ref, trin_ref, o_ref,
                       *, past, steps):
    blk = tri_ref.shape[0]
    carry = jnp.zeros((1, HEADS), F32)
    cums = []
    for r in range(past // blk):
        c = _sel_dot(tri_ref[...], lfc_ref[0, r * blk:(r + 1) * blk, :]) + carry
        cums.append(c)
        carry = c[-1:, :]
    cum_c = jnp.concatenate(cums, axis=0)
    cum_n = _sel_dot(trin_ref[...], small_ref[:, L_LOGF:L_LOGF + HEADS]) + carry

    q = q_ref[...]
    kc = kc_ref[0].astype(BF16)
    vc = vc_ref[0].astype(BF16)
    kn = kn_ref[...].astype(BF16)
    vn = vn_ref[...].astype(BF16)
    lane_c = lax.broadcasted_iota(jnp.int32, (past, LANES), 1)
    lane_n = lax.broadcasted_iota(jnp.int32, (steps, LANES), 1)
    kr = lax.broadcasted_iota(jnp.int32, (steps, steps), 0)
    qc = lax.broadcasted_iota(jnp.int32, (steps, steps), 1)
    ones_c = jnp.ones((past, LANES), BF16)
    ones_n = jnp.ones((steps, LANES), BF16)
    slabs = []
    for pr in range(HEADS // 2):
        ls = slice(pr * LANES, (pr + 1) * LANES)
        q_s, kc_s, kn_s, vc_s, vn_s = q[:, ls], kc[:, ls], kn[:, ls], vc[:, ls], vn[:, ls]
        o_s = jnp.zeros((steps, LANES), F32)
        for hh in range(2):
            hd = 2 * pr + hh
            in_c = (lane_c >= HEAD_DIM * hh) & (lane_c < HEAD_DIM * (hh + 1))
            in_n = (lane_n >= HEAD_DIM * hh) & (lane_n < HEAD_DIM * (hh + 1))
            qm = jnp.where(in_n, q_s, jnp.zeros_like(q_s))
            s1 = _dot_nt(kc_s, qm) - LOG2E * cum_c[:, hd:hd + 1]
            s2 = _dot_nt(kn_s, qm) - LOG2E * cum_n[:, hd:hd + 1]
            s2 = jnp.where(kr <= qc, s2, -1e30)
            m = jnp.maximum(jnp.max(s1, axis=0, keepdims=True), jnp.max(s2, axis=0, keepdims=True))
            p1 = jnp.exp2(s1 - m).astype(BF16)
            p2 = jnp.exp2(s2 - m).astype(BF16)
            num = _dot_tn(p1, jnp.where(in_c, vc_s, jnp.zeros_like(vc_s))) + \
                _dot_tn(p2, jnp.where(in_n, vn_s, jnp.zeros_like(vn_s)))
            den = _dot_tn(p1, ones_c) + _dot_tn(p2, ones_n)
            o_s = o_s + num / den
        slabs.append(o_s)
    o_ref[...] = jnp.concatenate(slabs, axis=1).astype(BF16)


def _fox_sample(q, kn, vn, small, kc, vc, lfc, batch, steps, past):
    blk = 256
    assert past % blk == 0
    r = np.arange(blk)
    tri = jnp.asarray(r[None, :] <= r[:, None], BF16)
    rn = np.arange(steps)
    trin = jnp.asarray(rn[None, :] <= rn[:, None], BF16)
    row = lambda w: pl.BlockSpec((steps, w), lambda b: (b, 0))
    return pl.pallas_call(
        functools.partial(_fox_sample_kernel, past=past, steps=steps),
        grid=(batch,),
        in_specs=[row(WIDTH), row(WIDTH), row(WIDTH), row(LANES),
                  pl.BlockSpec((1, past, WIDTH), lambda b: (b, 0, 0)),
                  pl.BlockSpec((1, past, WIDTH), lambda b: (b, 0, 0)),
                  pl.BlockSpec((1, past, HEADS), lambda b: (b, 0, 0)),
                  _const_spec((blk, blk)), _const_spec((steps, steps))],
        out_specs=row(WIDTH),
        out_shape=jax.ShapeDtypeStruct((batch * steps, WIDTH), BF16),
        compiler_params=pltpu.CompilerParams(dimension_semantics=("arbitrary",), vmem_limit_bytes=VMEM_LIMIT),
        name="fox_sample",
    )(q, kn, vn, small, kc, vc, lfc, tri, trin)


def _gdn_kernel(gq_ref, small_ref, hist_ref, s0_ref, convw_ref, tri_ref, hsum_ref, hexp_g_ref, hexp_b_ref,
                o_ref, sfin_ref, xbuf, state, *, chunk):
    c_idx = pl.program_id(1)
    n_chunks = pl.num_programs(1)
    hist_rows = SUBLANES

    @pl.when(c_idx == 0)
    def _():
        xbuf[0:hist_rows, :] = hist_ref[0]
        state[...] = s0_ref[0]

    xbuf[hist_rows:hist_rows + chunk, :] = gq_ref[...]
    w = convw_ref[...]
    base = hist_rows - (CONV_WIDTH - 1)
    y = xbuf[base:base + chunk, :] * w[0:1, :]
    for i in range(1, CONV_WIDTH):
        y = y + xbuf[base + i:base + i + chunk, :] * w[i:i + 1, :]
    xbuf[0:hist_rows, :] = xbuf[chunk:chunk + hist_rows, :]
    y = y * _sigmoid(y)

    hsum = hsum_ref[...]
    hexp_g = hexp_g_ref[...]
    hexp_b = hexp_b_ref[...]

    def l2n(x):
        inv = lax.rsqrt(_dot_sel(x * x, hsum, terms=2) + EPS)
        return x * _dot_sel(inv, hexp_g, terms=2)

    q = l2n(y[:, 0:WIDTH]) * (HEAD_DIM ** -0.5)
    k = l2n(y[:, WIDTH:2 * WIDTH])
    v = y[:, 2 * WIDTH:3 * WIDTH]

    small = small_ref[...]
    gc = _sel_dot(tri_ref[...], small)
    gc_x = _dot_sel(gc, hexp_g)
    beta_x = _dot_sel(small, hexp_b, terms=2)
    egc_x = jnp.exp(gc_x)
    glast_x = gc_x[chunk - 1:chunk, :]
    eglast_x = jnp.exp(glast_x)
    kb = k * beta_x
    vb = v * beta_x
    kbe = kb * egc_x
    qe = q * egc_x
    kdec = k * jnp.exp(glast_x - gc_x)

    pad_rows = LANES - chunk
    gc_sq = jnp.concatenate([gc, jnp.zeros((pad_rows, LANES), F32)], axis=0) if pad_rows else gc
    gc_t = gc_sq.T

    ri = lax.broadcasted_iota(jnp.int32, (chunk, chunk), 0)
    ci = lax.broadcasted_iota(jnp.int32, (chunk, chunk), 1)
    eye = jnp.where(ri == ci, 1.0, 0.0).astype(F32)
    n_double = int(np.log2(chunk)) - 1
    assert 2 ** (n_double + 1) == chunk

    outs = []
    for hd in range(HEADS):
        sl = slice(hd * HEAD_DIM, (hd + 1) * HEAD_DIM)
        gcol = gc[:, L_G + hd:L_G + hd + 1]
        grow = gc_t[L_G + hd:L_G + hd + 1, 0:chunk]
        dec = jnp.where(ri >= ci, jnp.exp(gcol - grow), 0.0)
        k_h = k[:, sl].astype(BF16)
        a = _dot_nt(kb[:, sl].astype(BF16), k_h) * jnp.where(ri > ci, dec, 0.0)
        pw = -a
        tinv = eye + pw
        for _ in range(n_double):
            pw_b = pw.astype(BF16)
            pw = _dot(pw_b, pw_b)
            tinv = tinv + _dot(tinv.astype(BF16), pw.astype(BF16))
        rhs = jnp.concatenate([vb[:, sl], kbe[:, sl]], axis=1).astype(BF16)
        uw = _dot(tinv.astype(BF16), rhs)
        u, wm = uw[:, 0:HEAD_DIM], uw[:, HEAD_DIM:2 * HEAD_DIM]
        s_h = state[hd]
        s_b = s_h.astype(BF16)
        v_new = u - _dot(wm.astype(BF16), s_b)
        v_new_b = v_new.astype(BF16)
        qk = _dot_nt(q[:, sl].astype(BF16), k_h) * dec
        o_h = _dot(qe[:, sl].astype(BF16), s_b) + _dot(qk.astype(BF16), v_new_b)
        state[hd] = s_h * eglast_x[:, sl] + _dot_tn(kdec[:, sl].astype(BF16), v_new_b)
        outs.append(o_h)
    o_ref[...] = jnp.concatenate(outs, axis=1)

    @pl.when(c_idx == n_chunks - 1)
    def _():
        sfin_ref[0] = state[...]


def _gdn(gqkv, small, hist, s0, conv_w, batch, seq, chunk):
    assert seq % chunk == 0 and chunk % SUBLANES == 0 and chunk <= LANES
    nc = seq // chunk
    r = np.arange(chunk)
    tri = jnp.asarray(r[None, :] <= r[:, None], BF16)
    head_of = np.arange(WIDTH) // HEAD_DIM
    hsum = jnp.asarray(head_of[:, None] == (np.arange(LANES)[None, :] - L_G), BF16)
    hexp_g = jnp.asarray((np.arange(LANES)[:, None] - L_G) == head_of[None, :], BF16)
    hexp_b = jnp.asarray((np.arange(LANES)[:, None] - L_BETA) == head_of[None, :], BF16)
    conv_w8 = jnp.concatenate([conv_w, jnp.zeros((SUBLANES - CONV_WIDTH, CONV_DIM), F32)], axis=0)
    return pl.pallas_call(
        functools.partial(_gdn_kernel, chunk=chunk),
        grid=(batch, nc),
        in_specs=[pl.BlockSpec((chunk, CONV_DIM), lambda b, c: (b * nc + c, 0)),
                  pl.BlockSpec((chunk, LANES), lambda b, c: (b * nc + c, 0)),
                  pl.BlockSpec((1, SUBLANES, CONV_DIM), lambda b, c: (b, 0, 0)),
                  pl.BlockSpec((1, HEADS, HEAD_DIM, HEAD_DIM), lambda b, c: (b, 0, 0, 0)),
                  _const_spec((SUBLANES, CONV_DIM)), _const_spec((chunk, chunk)),
                  _const_spec((WIDTH, LANES)), _const_spec((LANES, WIDTH)), _const_spec((LANES, WIDTH))],
        out_specs=(pl.BlockSpec((chunk, WIDTH), lambda b, c: (b * nc + c, 0)),
                   pl.BlockSpec((1, HEADS, HEAD_DIM, HEAD_DIM), lambda b, c: (b, 0, 0, 0))),
        out_shape=(jax.ShapeDtypeStruct((batch * seq, WIDTH), F32),
                   jax.ShapeDtypeStruct((batch, HEADS, HEAD_DIM, HEAD_DIM), F32)),
        scratch_shapes=[pltpu.VMEM((chunk + SUBLANES, CONV_DIM), F32),
                        pltpu.VMEM((HEADS, HEAD_DIM, HEAD_DIM), F32)],
        compiler_params=pltpu.CompilerParams(dimension_semantics=("arbitrary", "arbitrary"),
                                             vmem_limit_bytes=VMEM_LIMIT),
        name="gdn",
    )(gqkv, small, hist, s0, conv_w8, tri, hsum, hexp_g, hexp_b)


GROUP = 4
GROUP_W = GROUP * HEAD_DIM


def _stack_heads(x, lane):
    return jnp.concatenate([jnp.where(lane == hh, x, jnp.zeros_like(x)) for hh in range(GROUP)], axis=0)


def _fold_heads(x, rows):
    out = x[0:rows]
    for hh in range(1, GROUP):
        out = out + x[hh * rows:(hh + 1) * rows]
    return out


def _gdn_local_kernel(gq_ref, prev_ref, small_ref, convw_ref, tri_ref, hsum_ref, hexp_g_ref, hexp_b_ref, dexp_ref,
                      u_ref, w_ref, qe_ref, kdec_ref, qkw_ref, egl_ref, xbuf, *, chunk):
    i = pl.program_id(1)
    tm = gq_ref.shape[0]
    hist_rows = SUBLANES
    prev = prev_ref[...]
    xbuf[0:hist_rows, :] = jnp.where(i == 0, jnp.zeros_like(prev), prev)
    xbuf[hist_rows:hist_rows + tm, :] = gq_ref[...]
    w = convw_ref[...]
    base = hist_rows - (CONV_WIDTH - 1)
    y = xbuf[base:base + tm, :] * w[0:1, :]
    for t in range(1, CONV_WIDTH):
        y = y + xbuf[base + t:base + t + tm, :] * w[t:t + 1, :]
    y = y * _sigmoid(y)

    hsum = hsum_ref[...]
    hexp_g = hexp_g_ref[...]
    hexp_b = hexp_b_ref[...]

    def l2n(x):
        inv = lax.rsqrt(_dot_sel(x * x, hsum, terms=2) + EPS)
        return x * _dot_sel(inv, hexp_g, terms=2)

    q = l2n(y[:, 0:WIDTH]) * (HEAD_DIM ** -0.5)
    k = l2n(y[:, WIDTH:2 * WIDTH])
    v = y[:, 2 * WIDTH:3 * WIDTH]

    small = small_ref[...]
    gc = _sel_dot(tri_ref[...], small)
    gc_x = _dot_sel(gc, hexp_g)
    beta_x = _dot_sel(small, hexp_b, terms=2)
    n_chunks = tm // chunk
    lasts = [gc_x[(c + 1) * chunk - 1:(c + 1) * chunk, :] for c in range(n_chunks)]
    glast_x = jnp.concatenate([jnp.broadcast_to(r, (chunk, WIDTH)) for r in lasts], axis=0)
    egc_x = jnp.exp(gc_x)
    kb = k * beta_x
    vb = (v * beta_x).astype(BF16)
    kbe = (kb * egc_x).astype(BF16)
    kb = kb.astype(BF16)
    qe_ref[...] = (q * egc_x).astype(BF16)
    kdec_ref[...] = (k * jnp.exp(glast_x - gc_x)).astype(BF16)
    for c in range(n_chunks):
        egl_ref[c] = jnp.broadcast_to(jnp.exp(lasts[c]), (SUBLANES, WIDTH))
    q = q.astype(BF16)
    k = k.astype(BF16)

    rows = GROUP * chunk
    lane = lax.broadcasted_iota(jnp.int32, (chunk, GROUP_W), 1)
    head_of_lane = lane // HEAD_DIM
    ri = lax.broadcasted_iota(jnp.int32, (rows, rows), 0)
    ci = lax.broadcasted_iota(jnp.int32, (rows, rows), 1)
    same = (ri // chunk) == (ci // chunk)
    incl = same & (ri >= ci)
    strict = same & (ri > ci)
    eye = jnp.where(ri == ci, 1.0, 0.0).astype(F32)
    n_double = int(np.log2(chunk)) - 1

    placed = None
    for t, part in enumerate(_split3(gc)):
        term = _dot(part, dexp_ref[t])
        placed = term if placed is None else placed + term
    seg = lax.broadcasted_iota(jnp.int32, (1, WIDTH), 1) % HEAD_DIM
    diff_l = (placed[:, 0:WIDTH] + jnp.where((seg >= 3) & (seg < 6), 1.0, 0.0)).astype(BF16)
    diff_r = (jnp.where(seg < 3, 1.0, 0.0) - placed[:, WIDTH:2 * WIDTH]).astype(BF16)

    probs = [(slice(c * chunk, (c + 1) * chunk), slice(g * GROUP_W, (g + 1) * GROUP_W))
             for c in range(n_chunks) for g in range(HEADS // GROUP)]
    stack = lambda x, pr: _stack_heads(x[pr[0], pr[1]], head_of_lane)
    diffs = [_dot_nt(stack(diff_l, pr), stack(diff_r, pr)) for pr in probs]
    aqs = [_dot_nt(jnp.concatenate([stack(kb, pr), stack(q, pr)], axis=0), stack(k, pr)) for pr in probs]
    decays = [jnp.exp(jnp.where(incl, d, -1e30)) for d in diffs]
    for pr, aq, decay in zip(probs, aqs, decays):
        qkw_ref[pr[0], pr[1]] = _fold_heads(aq[rows:2 * rows] * decay, chunk).astype(BF16)
    pws = [-(aq[0:rows] * jnp.where(strict, decay, 0.0)) for aq, decay in zip(aqs, decays)]
    tinvs = [eye + pw for pw in pws]
    pws = [pw.astype(BF16) for pw in pws]
    pws = [_dot(pw, pw).astype(BF16) for pw in pws]
    for it in range(n_double):
        if it < n_double - 1:
            both = [_dot(jnp.concatenate([pw, tinv.astype(BF16)], axis=0), pw) for pw, tinv in zip(pws, tinvs)]
            pws = [b[0:rows].astype(BF16) for b in both]
            tinvs = [tinv + b[rows:2 * rows] for tinv, b in zip(tinvs, both)]
        else:
            tinvs = [tinv + _dot(tinv.astype(BF16), pw) for pw, tinv in zip(pws, tinvs)]
    uws = [_dot(tinv.astype(BF16), jnp.concatenate([stack(vb, pr), stack(kbe, pr)], axis=1))
           for pr, tinv in zip(probs, tinvs)]
    for pr, uw in zip(probs, uws):
        u_ref[pr[0], pr[1]] = _fold_heads(uw[:, 0:GROUP_W], chunk).astype(BF16)
        w_ref[pr[0], pr[1]] = _fold_heads(uw[:, GROUP_W:2 * GROUP_W], chunk).astype(BF16)


def _gdn_scan_kernel(u_ref, w_ref, qe_ref, kdec_ref, qkw_ref, egl_ref, o_ref, sfin_ref, state, *, chunk):
    c_idx = pl.program_id(0)
    batch = u_ref.shape[0]

    @pl.when(c_idx == 0)
    def _():
        state[...] = jnp.zeros_like(state)

    rows = GROUP * chunk
    lane = lax.broadcasted_iota(jnp.int32, (chunk, GROUP_W), 1) // HEAD_DIM
    ri = lax.broadcasted_iota(jnp.int32, (GROUP_W, GROUP_W), 0)
    ci = lax.broadcasted_iota(jnp.int32, (GROUP_W, GROUP_W), 1)
    same = (ri // HEAD_DIM) == (ci // HEAD_DIM)
    probs = [(b, g, slice(g * GROUP_W, (g + 1) * GROUP_W)) for b in range(batch) for g in range(HEADS // GROUP)]
    s_fs = [state[b, g] for b, g, _ in probs]
    s_bs = [s_f.astype(BF16) for s_f in s_fs]
    v_news = [(u_ref[b, :, ls].astype(F32) - _dot(w_ref[b, :, ls], s_b)).astype(BF16)
              for (b, _, ls), s_b in zip(probs, s_bs)]
    for (b, _, ls), s_b, v_new in zip(probs, s_bs, v_news):
        o_ref[b, :, ls] = _dot(jnp.concatenate([qe_ref[b, :, ls], qkw_ref[b, :, ls]], axis=1),
                               jnp.concatenate([s_b, _stack_heads(v_new, lane)], axis=0))
    kvs = [_dot_tn(kdec_ref[b, :, ls], v_new) for (b, _, ls), v_new in zip(probs, v_news)]
    for (b, g, ls), s_f, kv in zip(probs, s_fs, kvs):
        state[b, g] = s_f * egl_ref[b, 0, 0:1, ls] + jnp.where(same, kv, 0.0)

    @pl.when(c_idx == pl.num_programs(0) - 1)
    def _():
        for b in range(batch):
            for hd in range(HEADS):
                g, hh = divmod(hd, GROUP)
                sl = slice(hh * HEAD_DIM, (hh + 1) * HEAD_DIM)
                sfin_ref[b, hd] = state[b, g, sl, sl]


def _gdn_long(gqkv, small, conv_w, batch, seq, chunk):
    tm = TOKEN_TILE
    assert seq % tm == 0 and tm % chunk == 0 and GROUP * chunk == GROUP_W
    nt = seq // tm
    nc = seq // chunk
    r = np.arange(tm)
    tri = jnp.asarray((r[None, :] <= r[:, None]) & ((r[None, :] // chunk) == (r[:, None] // chunk)), BF16)
    head_of = np.arange(WIDTH) // HEAD_DIM
    hsum = jnp.asarray(head_of[:, None] == (np.arange(LANES)[None, :] - L_G), BF16)
    hexp_g = jnp.asarray((np.arange(LANES)[:, None] - L_G) == head_of[None, :], BF16)
    hexp_b = jnp.asarray((np.arange(LANES)[:, None] - L_BETA) == head_of[None, :], BF16)
    conv_w8 = jnp.concatenate([conv_w, jnp.zeros((SUBLANES - CONV_WIDTH, CONV_DIM), F32)], axis=0)
    dexp = np.zeros((3, LANES, 2 * WIDTH), np.float32)
    for t in range(3):
        for hd in range(HEADS):
            dexp[t, L_G + hd, hd * HEAD_DIM + t] = 1.0
            dexp[t, L_G + hd, WIDTH + hd * HEAD_DIM + 3 + t] = 1.0
    dexp = jnp.asarray(dexp, BF16)
    row = lambda w: pl.BlockSpec((tm, w), lambda b, i: (b * nt + i, 0))
    per_tile = tm // chunk
    tok = jax.ShapeDtypeStruct((batch * seq, WIDTH), BF16)
    u, w, qe, kdec, qkw, egl = pl.pallas_call(
        functools.partial(_gdn_local_kernel, chunk=chunk),
        grid=(batch, nt),
        in_specs=[row(CONV_DIM),
                  pl.BlockSpec((SUBLANES, CONV_DIM),
                               lambda b, i: (jnp.maximum((b * nt + i) * (tm // SUBLANES) - 1, 0), 0)),
                  row(LANES), _const_spec((SUBLANES, CONV_DIM)), _const_spec((tm, tm)),
                  _const_spec((WIDTH, LANES)), _const_spec((LANES, WIDTH)), _const_spec((LANES, WIDTH)),
                  _const_spec((3, LANES, 2 * WIDTH))],
        out_specs=(row(WIDTH), row(WIDTH), row(WIDTH), row(WIDTH), row(WIDTH),
                   pl.BlockSpec((per_tile, SUBLANES, WIDTH), lambda b, i: (b * nt + i, 0, 0))),
        out_shape=(tok, tok, tok, tok, tok, jax.ShapeDtypeStruct((batch * nc, SUBLANES, WIDTH), F32)),
        scratch_shapes=[pltpu.VMEM((tm + SUBLANES, CONV_DIM), F32)],
        compiler_params=pltpu.CompilerParams(dimension_semantics=("arbitrary", "arbitrary"),
                                             vmem_limit_bytes=VMEM_LIMIT),
        name="gdn_local",
    )(gqkv, gqkv, small, conv_w8, tri, hsum, hexp_g, hexp_b, dexp)

    blk = pl.BlockSpec((batch, chunk, WIDTH), lambda c: (0, c, 0))
    as3 = lambda a: a.reshape(batch, seq, WIDTH)
    o, s_fin = pl.pallas_call(
        functools.partial(_gdn_scan_kernel, chunk=chunk),
        grid=(nc,),
        in_specs=[blk, blk, blk, blk, blk,
                  pl.BlockSpec((batch, 1, SUBLANES, WIDTH), lambda c: (0, c, 0, 0))],
        out_specs=(blk, pl.BlockSpec((batch, HEADS, HEAD_DIM, HEAD_DIM), lambda c: (0, 0, 0, 0))),
        out_shape=(jax.ShapeDtypeStruct((batch, seq, WIDTH), F32),
                   jax.ShapeDtypeStruct((batch, HEADS, HEAD_DIM, HEAD_DIM), F32)),
        scratch_shapes=[pltpu.VMEM((batch, HEADS // GROUP, GROUP_W, GROUP_W), F32)],
        compiler_params=pltpu.CompilerParams(dimension_semantics=("arbitrary",), vmem_limit_bytes=VMEM_LIMIT),
        name="gdn_scan",
    )(as3(u), as3(w), as3(qe), as3(kdec), as3(qkw), egl.reshape(batch, nc, SUBLANES, WIDTH))
    return o.reshape(batch * seq, WIDTH), s_fin


def _post_kernel(x_ref, fox_ref, gdn_ref, sgg_ref, sga_ref, sgb_ref,
                 wpa_ref, wpb_ref, wout_ref, wup_ref, wdown_ref,
                 ng_ref, gpost_ref, gpre2_ref, gpost2_ref, hsum_ref, hexp_ref, y_ref):
    ya = _dot(fox_ref[...], wpa_ref[...])
    o = gdn_ref[...]
    ms = _dot_sel(o * o, hsum_ref[...], terms=2) * (1.0 / HEAD_DIM)
    o = o * _dot_sel(lax.rsqrt(ms + EPS), hexp_ref[...], terms=2) * ng_ref[...] * sgg_ref[...].astype(F32)
    yb = _dot(o.astype(BF16), wpb_ref[...])
    m = sga_ref[...].astype(F32) * ya + sgb_ref[...].astype(F32) * yb
    mix = _dot(m.astype(BF16), wout_ref[...])
    y1 = x_ref[...] + _rms(mix, gpost_ref[...])
    h2 = _rms(y1, gpre2_ref[...]).astype(BF16)
    acc = jnp.zeros(y1.shape, F32)
    for c in range(D_FF // FF_TILE):
        u = jnp.maximum(_dot(h2, wup_ref[:, c * FF_TILE:(c + 1) * FF_TILE]), 0.0)
        acc = acc + _dot((u * u).astype(BF16), wdown_ref[c * FF_TILE:(c + 1) * FF_TILE, :])
    y_ref[...] = y1 + _rms(acc, gpost2_ref[...])


def _post(x, fox, gdn_o, sgg, sga, sgb, w_pa, w_pb, w_out, w_up, w_down, ng, gpost, gpre2, gpost2):
    t_total = x.shape[0]
    tm = TOKEN_TILE
    row = lambda w: pl.BlockSpec((tm, w), lambda i: (i, 0))
    head_of = np.arange(WIDTH) // HEAD_DIM
    hsum = jnp.asarray(head_of[:, None] == (np.arange(LANES)[None, :] - L_G), BF16)
    hexp = jnp.asarray((np.arange(LANES)[:, None] - L_G) == head_of[None, :], BF16)
    return pl.pallas_call(
        _post_kernel,
        grid=(t_total // tm,),
        in_specs=[row(D_MODEL), row(WIDTH), row(WIDTH), row(WIDTH), row(D_MODEL), row(D_MODEL),
                  _const_spec((WIDTH, D_MODEL)), _const_spec((WIDTH, D_MODEL)), _const_spec((D_MODEL, D_MODEL)),
                  _const_spec((D_MODEL, D_FF)), _const_spec((D_FF, D_MODEL)),
                  _const_spec((1, WIDTH)), _const_spec((1, D_MODEL)), _const_spec((1, D_MODEL)),
                  _const_spec((1, D_MODEL)), _const_spec((WIDTH, LANES)), _const_spec((LANES, WIDTH))],
        out_specs=row(D_MODEL),
        out_shape=jax.ShapeDtypeStruct((t_total, D_MODEL), F32),
        compiler_params=pltpu.CompilerParams(dimension_semantics=("arbitrary",), vmem_limit_bytes=VMEM_LIMIT),
        name="post",
    )(x, fox, gdn_o, sgg, sga, sgb, w_pa, w_pb, w_out, w_up, w_down, ng, gpost, gpre2, gpost2, hsum, hexp)


def _rearrange_w_in(w_in):
    o_ff = 3 * WIDTH
    o_gqkv = o_ff + HEADS
    o_ga = o_gqkv + CONV_DIM
    o_gb = o_ga + HEADS
    o_gg = o_gb + HEADS
    o_gate_a = o_gg + WIDTH
    o_gate_b = o_gate_a + D_MODEL
    cols = [w_in[:, 0:o_ff], w_in[:, o_gqkv:o_ga], w_in[:, o_gg:o_gate_a], w_in[:, o_gate_a:o_gate_b],
            w_in[:, o_gate_b:o_gate_b + D_MODEL], w_in[:, o_ff:o_gqkv], w_in[:, o_ga:o_gb], w_in[:, o_gb:o_gg],
            jnp.zeros((D_MODEL, LANES - 3 * HEADS), w_in.dtype)]
    return jnp.concatenate(cols, axis=1).astype(BF16)


def _lane_row(*pieces):
    v = jnp.concatenate([p.astype(F32) for p in pieces])
    return jnp.concatenate([v, jnp.zeros((LANES - v.shape[0],), F32)])[None, :]


def _pad_hist(conv_cache):
    b = conv_cache.shape[0]
    return jnp.concatenate([jnp.zeros((b, SUBLANES - (CONV_WIDTH - 1), CONV_DIM), F32), conv_cache.astype(F32)], axis=1)


def kernel(x_prompt, x_sample, cache_fox_k, cache_fox_v, cache_fox_logf, state_gdn, state_gdn_conv,
           w_in, fox_forget_bias, gdn_conv_w, gdn_a_log, gdn_dt_bias, gdn_norm_g,
           w_proj_fox, w_proj_gdn, w_out, norm_mix_pre, norm_mix_post, norm_mlp_pre, norm_mlp_post,
           w_up, w_down):
    depth = w_in.shape[0]
    bp, sp, _ = x_prompt.shape
    bs, ss, _ = x_sample.shape
    past = cache_fox_k.shape[2]
    y_p = x_prompt.reshape(bp * sp, D_MODEL)
    y_s = x_sample.reshape(bs * ss, D_MODEL)
    st_p, st_s = [], []
    for l in range(depth):
        w_r = _rearrange_w_in(w_in[l])
        padd = _lane_row(fox_forget_bias[l], gdn_dt_bias[l])
        palog = _lane_row(jnp.zeros((HEADS,), F32), gdn_a_log[l])
        gain = norm_mix_pre[l][None, :]
        post_w = (w_proj_fox[l].astype(BF16), w_proj_gdn[l].astype(BF16), w_out[l].astype(BF16),
                  w_up[l].astype(BF16), w_down[l].astype(BF16),
                  jnp.tile(gdn_norm_g[l], HEADS)[None, :], norm_mix_post[l][None, :],
                  norm_mlp_pre[l][None, :], norm_mlp_post[l][None, :])

        q, k, vt, fk, fv, gqkv, sgg, sga, sgb, small, cum = _in_proj(y_p, gain, w_r, padd, palog, sp)
        fox = _fox_prompt(q, k, vt, cum, bp, sp)
        gdn_o, s_fin = _gdn_long(gqkv, small, gdn_conv_w[l], bp, sp, GDN_CHUNK)
        y_p_new = _post(y_p, fox, gdn_o, sgg, sga, sgb, *post_w)
        st_p.append((fk.reshape(bp, sp, HEADS, HEAD_DIM), fv.reshape(bp, sp, HEADS, HEAD_DIM),
                     small[:, L_LOGF:L_LOGF + HEADS].reshape(bp, sp, HEADS), s_fin,
                     gqkv.reshape(bp, sp, CONV_DIM)[:, sp - (CONV_WIDTH - 1):, :]))
        y_p = y_p_new

        q, k, vt, fk, fv, gqkv, sgg, sga, sgb, small, cum = _in_proj(y_s, gain, w_r, padd, palog, ss)
        fox = _fox_sample(q, fk, fv, small,
                          cache_fox_k[l].reshape(bs, past, WIDTH), cache_fox_v[l].reshape(bs, past, WIDTH),
                          cache_fox_logf[l], bs, ss, past)
        gdn_o, s_new = _gdn(gqkv, small, _pad_hist(state_gdn_conv[l]), state_gdn[l].astype(F32),
                            gdn_conv_w[l], bs, ss, ss)
        y_s_new = _post(y_s, fox, gdn_o, sgg, sga, sgb, *post_w)
        conv_ext = jnp.concatenate([state_gdn_conv[l].astype(F32), gqkv.reshape(bs, ss, CONV_DIM)], axis=1)
        st_s.append((fk.reshape(bs, ss, HEADS, HEAD_DIM), fv.reshape(bs, ss, HEADS, HEAD_DIM),
                     small[:, L_LOGF:L_LOGF + HEADS].reshape(bs, ss, HEADS), s_new,
                     conv_ext[:, conv_ext.shape[1] - (CONV_WIDTH - 1):, :]))
        y_s = y_s_new

    fk_p, fv_p, lf_p, sg_p, cv_p = [jnp.stack(a) for a in zip(*st_p)]
    fk_s, fv_s, lf_s, sg_s, cv_s = [jnp.stack(a) for a in zip(*st_s)]
    return (y_p.reshape(bp, sp, D_MODEL), y_s.reshape(bs, ss, D_MODEL),
            fk_p, fv_p, lf_p, sg_p, cv_p, fk_s, fv_s, lf_s, sg_s, cv_s)
```

```python
import functools

import jax
import jax.numpy as jnp
import numpy as np
from jax import lax
from jax.experimental import pallas as pl
from jax.experimental.pallas import tpu as pltpu

F32 = jnp.float32
BF16 = jnp.bfloat16

D_MODEL = 1024
HEADS = 8
HEAD_DIM = 64
WIDTH = HEADS * HEAD_DIM
CONV_DIM = 3 * WIDTH
CONV_WIDTH = 4
D_FF = 4 * D_MODEL
EPS = 1e-6
LOG2E = 1.4426950408889634

LANES = 128
SUBLANES = 8
TOKEN_TILE = 256
ATTN_TILE = 256
ATTN_HEADS = 4
KEY_TILE = 512
FF_TILE = 1024
GDN_CHUNK = 64
VMEM_LIMIT = 56 * 1024 * 1024

C_Q, C_K, C_V, C_GQKV, C_GG, C_GA, C_GB, C_SMALL = 0, 512, 1024, 1536, 3072, 3584, 4608, 5632
IN_COLS = C_SMALL + LANES
L_LOGF, L_G, L_BETA = 0, 8, 16


def _split3(x):
    hi = x.astype(BF16)
    r = x - hi.astype(F32)
    mid = r.astype(BF16)
    lo = (r - mid.astype(F32)).astype(BF16)
    return hi, mid, lo


def _dot(a, b):
    return jnp.dot(a, b, preferred_element_type=F32)


def _dot_nt(a, b):
    return lax.dot_general(a, b, (((1,), (1,)), ((), ())), preferred_element_type=F32)


def _dot_tn(a, b):
    return lax.dot_general(a, b, (((0,), (0,)), ((), ())), preferred_element_type=F32)


def _sel_dot(sel, x, terms=3):
    parts = _split3(x)[:terms]
    n = x.shape[1]
    if n % LANES:
        return sum(_dot(sel, p) for p in parts)
    wide = _dot(sel, jnp.concatenate(parts, axis=1))
    return sum(wide[:, t * n:(t + 1) * n] for t in range(terms))


def _dot_sel(x, sel, terms=3):
    parts = _split3(x)[:terms]
    if terms == 1:
        return _dot(parts[0], sel)
    out = _dot(jnp.concatenate(parts[0:2], axis=1), jnp.concatenate([sel, sel], axis=0))
    for p in parts[2:]:
        out = out + _dot(p, sel)
    return out


def _sigmoid(x):
    return 1.0 / (1.0 + jnp.exp(-x))


def _rms(x, gain):
    ms = jnp.mean(x * x, axis=-1, keepdims=True)
    return x * lax.rsqrt(ms + EPS) * gain


def _const_spec(shape):
    nd = len(shape)
    return pl.BlockSpec(shape, lambda *_: (0,) * nd)


def _inproj_kernel(x_ref, gain_ref, w_ref, padd_ref, palog_ref, tri_ref,
                   q_ref, k_ref, vt_ref, fk_ref, fv_ref, gqkv_ref, sgg_ref, sga_ref, sgb_ref,
                   small_ref, cum_ref, carry_ref, *, tiles_per_seg):
    i = pl.program_id(0)
    h = _rms(x_ref[...], gain_ref[...]).astype(BF16)

    def mm(c0, width):
        return _dot(h, w_ref[:, c0:c0 + width])

    zq = mm(C_Q, WIDTH)
    q_ref[...] = (zq * (LOG2E * HEAD_DIM ** -0.5)).astype(BF16)
    zk = mm(C_K, WIDTH)
    fk_ref[...] = zk
    k_ref[...] = zk.astype(BF16)
    zv = mm(C_V, WIDTH)
    fv_ref[...] = zv
    vt_ref[0] = zv.T.astype(BF16)
    for c in range(3):
        gqkv_ref[:, c * WIDTH:(c + 1) * WIDTH] = mm(C_GQKV + c * WIDTH, WIDTH)
    zg = mm(C_GG, WIDTH)
    sgg_ref[...] = (zg * _sigmoid(zg)).astype(BF16)
    for c in range(2):
        sga_ref[:, c * WIDTH:(c + 1) * WIDTH] = _sigmoid(mm(C_GA + c * WIDTH, WIDTH)).astype(BF16)
        sgb_ref[:, c * WIDTH:(c + 1) * WIDTH] = _sigmoid(mm(C_GB + c * WIDTH, WIDTH)).astype(BF16)

    zs = mm(C_SMALL, LANES)
    t = zs + padd_ref[...]
    l1p = jnp.log1p(jnp.exp(-jnp.abs(t)))
    logf = jnp.minimum(t, 0.0) - l1p
    g = -jnp.exp(palog_ref[...]) * (jnp.maximum(t, 0.0) + l1p)
    beta = _sigmoid(zs)
    lane = lax.broadcasted_iota(jnp.int32, zs.shape, 1)
    small = jnp.where(lane < L_G, logf, jnp.where(lane < L_BETA, g, jnp.where(lane < L_BETA + HEADS, beta, 0.0)))
    small_ref[...] = small

    cum = _sel_dot(tri_ref[...], small)
    if tiles_per_seg > 1:
        @pl.when(i % tiles_per_seg == 0)
        def _():
            carry_ref[...] = jnp.zeros_like(carry_ref)
        cum = cum + carry_ref[0:1, :]
        carry_ref[...] = jnp.broadcast_to(cum[-1:, :], carry_ref.shape)
    cum_ref[...] = cum


def _in_proj(x, gain, w_r, padd, palog, seg_len):
    t_total = x.shape[0]
    tm = TOKEN_TILE
    assert t_total % tm == 0
    nt = t_total // tm
    if seg_len >= tm:
        assert seg_len % tm == 0
        tiles_per_seg = seg_len // tm
        r = np.arange(tm)
        tri = (r[None, :] <= r[:, None])
    else:
        assert tm % seg_len == 0
        tiles_per_seg = 1
        r = np.arange(tm)
        tri = (r[None, :] <= r[:, None]) & ((r[None, :] // seg_len) == (r[:, None] // seg_len))
    tri = jnp.asarray(tri, BF16)

    row = lambda w: pl.BlockSpec((tm, w), lambda i: (i, 0))
    out_shape = (
        jax.ShapeDtypeStruct((t_total, WIDTH), BF16),
        jax.ShapeDtypeStruct((t_total, WIDTH), BF16),
        jax.ShapeDtypeStruct((nt, WIDTH, tm), BF16),
        jax.ShapeDtypeStruct((t_total, WIDTH), F32),
        jax.ShapeDtypeStruct((t_total, WIDTH), F32),
        jax.ShapeDtypeStruct((t_total, CONV_DIM), F32),
        jax.ShapeDtypeStruct((t_total, WIDTH), BF16),
        jax.ShapeDtypeStruct((t_total, D_MODEL), BF16),
        jax.ShapeDtypeStruct((t_total, D_MODEL), BF16),
        jax.ShapeDtypeStruct((t_total, LANES), F32),
        jax.ShapeDtypeStruct((t_total, LANES), F32),
    )
    out_specs = (row(WIDTH), row(WIDTH), pl.BlockSpec((1, WIDTH, tm), lambda i: (i, 0, 0)),
                 row(WIDTH), row(WIDTH), row(CONV_DIM), row(WIDTH), row(D_MODEL), row(D_MODEL),
                 row(LANES), row(LANES))
    return pl.pallas_call(
        functools.partial(_inproj_kernel, tiles_per_seg=tiles_per_seg),
        grid=(nt,),
        in_specs=[row(D_MODEL), _const_spec((1, D_MODEL)), _const_spec((D_MODEL, IN_COLS)),
                  _const_spec((1, LANES)), _const_spec((1, LANES)), _const_spec((tm, tm))],
        out_specs=out_specs,
        out_shape=out_shape,
        scratch_shapes=[pltpu.VMEM((SUBLANES, LANES), F32)],
        compiler_params=pltpu.CompilerParams(dimension_semantics=("arbitrary",), vmem_limit_bytes=VMEM_LIMIT),
        name="in_proj",
    )(x, gain, w_r, padd, palog, tri)


def _fox_prompt_kernel(q_ref, k_ref, vt_ref, cum_ref, o_ref, bias_ref, s0_ref, s1_ref, p0_ref, p1_ref,
                       *, seq, tq, tk, nh):
    grp = pl.program_id(1)
    qi = pl.program_id(2)
    rows = 512
    vt_tile = vt_ref.shape[2]
    sub = tk // vt_tile
    heads = range(nh)
    s_slots = (s0_ref, s1_ref)
    p_slots = (p0_ref, p1_ref)

    @pl.when(qi == 0)
    def _prep():
        def fill(r, carry):
            c = cum_ref[pl.ds(r * rows, rows), :]
            lane = lax.broadcasted_iota(jnp.int32, c.shape, 1)
            slab = jnp.zeros(c.shape, F32)
            for hh in heads:
                col = jnp.sum(jnp.where(lane == nh * grp + hh, c, 0.0), axis=-1, keepdims=True) * (-LOG2E)
                for t, part in enumerate(_split3(col)):
                    slab = jnp.where(lane == 3 * hh + t, part.astype(F32), slab)
            bias_ref[pl.ds(r * rows, rows), :] = slab.astype(BF16)
            return carry
        lax.fori_loop(0, seq // rows, fill, 0)

    lane_q = lax.broadcasted_iota(jnp.int32, (tq, LANES), 1)
    qh = []
    for hh in heads:
        q2 = q_ref[:, (hh // 2) * LANES:(hh // 2 + 1) * LANES]
        half = hh % 2
        q_m = jnp.where((lane_q >= HEAD_DIM * half) & (lane_q < HEAD_DIM * (half + 1)), q2, jnp.zeros_like(q2))
        ones = jnp.where((lane_q >= 3 * hh) & (lane_q < 3 * hh + 3), 1.0, 0.0).astype(BF16)
        qh.append(jnp.concatenate([q_m, ones], axis=1))

    def scores_to(slot, j):
        bias = bias_ref[pl.ds(j * tk, tk), :]
        kjs = [jnp.concatenate([k_ref[pl.ds(j * tk, tk), sl * LANES:(sl + 1) * LANES], bias], axis=1)
               for sl in range(nh // 2)]
        s_ts = [_dot_nt(kjs[hh // 2], qh[hh]) for hh in heads]
        for hh in heads:
            s_slots[slot][hh] = s_ts[hh]
        return tuple(jnp.max(s_t, axis=0, keepdims=True) for s_t in s_ts)

    ones_rows = jnp.ones((2 * SUBLANES, vt_tile), BF16)

    def add_values(slot, j, accs, alphas):
        pvs = [sum(_dot(jnp.concatenate([vt_ref[j * sub + t, HEAD_DIM * hh:HEAD_DIM * (hh + 1), :], ones_rows], axis=0),
                        p_slots[slot][hh, t * vt_tile:(t + 1) * vt_tile, :]) for t in range(sub)) for hh in heads]
        return tuple(alphas[hh] * accs[hh] + pvs[hh] for hh in heads)

    def rescale(ms, bms):
        m_new = tuple(jnp.maximum(ms[hh], bms[hh]) for hh in heads)
        return m_new, tuple(jnp.exp2(ms[hh] - m_new[hh]) for hh in heads)

    def probabilities(slot, hh, m_new, visible_from):
        s_t = s_slots[slot][hh]
        if visible_from is not None:
            s_t = jnp.where(visible_from, s_t, -1e30)
        p_slots[slot][hh] = jnp.exp2(s_t - m_new).astype(BF16)

    def trip(j, rd, carry):
        ms, accs, a_prev, bms = carry
        wr = 1 - rd
        accs = add_values(wr, jnp.maximum(j - 1, 0), accs, a_prev)
        m_new, alphas = rescale(ms, bms)
        for hh in heads:
            probabilities(rd, hh, m_new[hh], None)
        bm_next = scores_to(wr, j + 1)
        return m_new, accs, alphas, bm_next

    n_full = (qi * tq) // tk

    def last(rd, carry):
        ms, accs, a_prev, _ = carry
        accs = add_values(1 - rd, jnp.maximum(n_full - 1, 0), accs, a_prev)
        kr = lax.broadcasted_iota(jnp.int32, (tk, tq), 0)
        qc = lax.broadcasted_iota(jnp.int32, (tk, tq), 1)
        visible = kr <= qc + (qi * tq - n_full * tk)
        bms = tuple(jnp.max(jnp.where(visible, s_slots[rd][hh], -1e30), axis=0, keepdims=True) for hh in heads)
        m_new, alphas = rescale(ms, bms)
        for hh in heads:
            probabilities(rd, hh, m_new[hh], visible)
        accs = add_values(rd, n_full, accs, alphas)
        return jnp.concatenate([accs[hh][0:HEAD_DIM] / accs[hh][HEAD_DIM:HEAD_DIM + 1] for hh in heads], axis=0)

    each = lambda f: tuple(f() for _ in heads)
    p1_ref[...] = jnp.zeros(p1_ref.shape, BF16)
    init = (each(lambda: jnp.full((1, tq), -1e30, F32)), each(lambda: jnp.zeros((HEAD_DIM + 2 * SUBLANES, tq), F32)),
            each(lambda: jnp.ones((1, tq), F32)), scores_to(0, 0))
    carry = lax.fori_loop(
        0, n_full, lambda j, c: lax.cond(j % 2 == 0, lambda c: trip(j, 0, c), lambda c: trip(j, 1, c), c), init)
    o_t = lax.cond(n_full % 2 == 0, lambda c: last(0, c), lambda c: last(1, c), carry)
    o_ref[...] = o_t.T.astype(BF16)


def _fox_prompt(q, k, vt, cum, batch, seq):
    tq, tk, nh = ATTN_TILE, KEY_TILE, ATTN_HEADS
    assert seq % tk == 0 and tk % tq == 0 and tk % TOKEN_TILE == 0 and HEADS % nh == 0 and nh % 2 == 0
    nq = seq // tq
    width = nh * HEAD_DIM
    return pl.pallas_call(
        functools.partial(_fox_prompt_kernel, seq=seq, tq=tq, tk=tk, nh=nh),
        grid=(batch, HEADS // nh, nq),
        in_specs=[pl.BlockSpec((tq, width), lambda b, g, i: (b * nq + i, g)),
                  pl.BlockSpec((seq, width), lambda b, g, i: (b, g)),
                  pl.BlockSpec((seq // TOKEN_TILE, width, TOKEN_TILE), lambda b, g, i: (b, g, 0)),
                  pl.BlockSpec((seq, LANES), lambda b, g, i: (b, 0))],
        out_specs=pl.BlockSpec((tq, width), lambda b, g, i: (b * nq + i, g)),
        out_shape=jax.ShapeDtypeStruct((batch * seq, WIDTH), BF16),
        scratch_shapes=[pltpu.VMEM((seq, LANES), BF16),
                        pltpu.VMEM((nh, tk, tq), F32), pltpu.VMEM((nh, tk, tq), F32),
                        pltpu.VMEM((nh, tk, tq), BF16), pltpu.VMEM((nh, tk, tq), BF16)],
        compiler_params=pltpu.CompilerParams(dimension_semantics=("arbitrary",) * 3, vmem_limit_bytes=VMEM_LIMIT),
        name="fox_prompt",
    )(q, k, vt, cum)


def _fox_sample_kernel(q_ref, kn_ref, vn_ref, small_ref, kc_ref, vc_ref, lfc_ref, tri_ref, trin_ref, o_ref,
                       *, past, steps):
    blk = tri_ref.shape[0]
    carry = jnp.zeros((1, HEADS), F32)
    cums = []
    for r in range(past // blk):
        c = _sel_dot(tri_ref[...], lfc_ref[0, r * blk:(r + 1) * blk, :]) + carry
        cums.append(c)
        carry = c[-1:, :]
    cum_c = jnp.concatenate(cums, axis=0)
    cum_n = _sel_dot(trin_ref[...], small_ref[:, L_LOGF:L_LOGF + HEADS]) + carry

    q = q_ref[...]
    kc = kc_ref[0].astype(BF16)
    vc = vc_ref[0].astype(BF16)
    kn = kn_ref[...].astype(BF16)
    vn = vn_ref[...].astype(BF16)
    lane_c = lax.broadcasted_iota(jnp.int32, (past, LANES), 1)
    lane_n = lax.broadcasted_iota(jnp.int32, (steps, LANES), 1)
    kr = lax.broadcasted_iota(jnp.int32, (steps, steps), 0)
    qc = lax.broadcasted_iota(jnp.int32, (steps, steps), 1)
    ones_c = jnp.ones((past, LANES), BF16)
    ones_n = jnp.ones((steps, LANES), BF16)
    slabs = []
    for pr in range(HEADS // 2):
        ls = slice(pr * LANES, (pr + 1) * LANES)
        q_s, kc_s, kn_s, vc_s, vn_s = q[:, ls], kc[:, ls], kn[:, ls], vc[:, ls], vn[:, ls]
        o_s = jnp.zeros((steps, LANES), F32)
        for hh in range(2):
            hd = 2 * pr + hh
            in_c = (lane_c >= HEAD_DIM * hh) & (lane_c < HEAD_DIM * (hh + 1))
            in_n = (lane_n >= HEAD_DIM * hh) & (lane_n < HEAD_DIM * (hh + 1))
            qm = jnp.where(in_n, q_s, jnp.zeros_like(q_s))
            s1 = _dot_nt(kc_s, qm) - LOG2E * cum_c[:, hd:hd + 1]
            s2 = _dot_nt(kn_s, qm) - LOG2E * cum_n[:, hd:hd + 1]
            s2 = jnp.where(kr <= qc, s2, -1e30)
            m = jnp.maximum(jnp.max(s1, axis=0, keepdims=True), jnp.max(s2, axis=0, keepdims=True))
            p1 = jnp.exp2(s1 - m).astype(BF16)
            p2 = jnp.exp2(s2 - m).astype(BF16)
            num = _dot_tn(p1, jnp.where(in_c, vc_s, jnp.zeros_like(vc_s))) + \
                _dot_tn(p2, jnp.where(in_n, vn_s, jnp.zeros_like(vn_s)))
            den = _dot_tn(p1, ones_c) + _dot_tn(p2, ones_n)
            o_s = o_s + num / den
        slabs.append(o_s)
    o_ref[...] = jnp.concatenate(slabs, axis=1).astype(BF16)


def _fox_sample(q, kn, vn, small, kc, vc, lfc, batch, steps, past):
    blk = 256
    assert past % blk == 0
    r = np.arange(blk)
    tri = jnp.asarray(r[None, :] <= r[:, None], BF16)
    rn = np.arange(steps)
    trin = jnp.asarray(rn[None, :] <= rn[:, None], BF16)
    row = lambda w: pl.BlockSpec((steps, w), lambda b: (b, 0))
    return pl.pallas_call(
        functools.partial(_fox_sample_kernel, past=past, steps=steps),
        grid=(batch,),
        in_specs=[row(WIDTH), row(WIDTH), row(WIDTH), row(LANES),
                  pl.BlockSpec((1, past, WIDTH), lambda b: (b, 0, 0)),
                  pl.BlockSpec((1, past, WIDTH), lambda b: (b, 0, 0)),
                  pl.BlockSpec((1, past, HEADS), lambda b: (b, 0, 0)),
                  _const_spec((blk, blk)), _const_spec((steps, steps))],
        out_specs=row(WIDTH),
        out_shape=jax.ShapeDtypeStruct((batch * steps, WIDTH), BF16),
        compiler_params=pltpu.CompilerParams(dimension_semantics=("arbitrary",), vmem_limit_bytes=VMEM_LIMIT),
        name="fox_sample",
    )(q, kn, vn, small, kc, vc, lfc, tri, trin)


def _gdn_kernel(gq_ref, small_ref, hist_ref, s0_ref, convw_ref, tri_ref, hsum_ref, hexp_g_ref, hexp_b_ref,
                o_ref, sfin_ref, xbuf, state, *, chunk):
    c_idx = pl.program_id(1)
    n_chunks = pl.num_programs(1)
    hist_rows = SUBLANES

    @pl.when(c_idx == 0)
    def _():
        xbuf[0:hist_rows, :] = hist_ref[0]
        state[...] = s0_ref[0]

    xbuf[hist_rows:hist_rows + chunk, :] = gq_ref[...]
    w = convw_ref[...]
    base = hist_rows - (CONV_WIDTH - 1)
    y = xbuf[base:base + chunk, :] * w[0:1, :]
    for i in range(1, CONV_WIDTH):
        y = y + xbuf[base + i:base + i + chunk, :] * w[i:i + 1, :]
    xbuf[0:hist_rows, :] = xbuf[chunk:chunk + hist_rows, :]
    y = y * _sigmoid(y)

    hsum = hsum_ref[...]
    hexp_g = hexp_g_ref[...]
    hexp_b = hexp_b_ref[...]

    def l2n(x):
        inv = lax.rsqrt(_dot_sel(x * x, hsum, terms=2) + EPS)
        return x * _dot_sel(inv, hexp_g, terms=2)

    q = l2n(y[:, 0:WIDTH]) * (HEAD_DIM ** -0.5)
    k = l2n(y[:, WIDTH:2 * WIDTH])
    v = y[:, 2 * WIDTH:3 * WIDTH]

    small = small_ref[...]
    gc = _sel_dot(tri_ref[...], small)
    gc_x = _dot_sel(gc, hexp_g)
    beta_x = _dot_sel(small, hexp_b, terms=2)
    egc_x = jnp.exp(gc_x)
    glast_x = gc_x[chunk - 1:chunk, :]
    eglast_x = jnp.exp(glast_x)
    kb = k * beta_x
    vb = v * beta_x
    kbe = kb * egc_x
    qe = q * egc_x
    kdec = k * jnp.exp(glast_x - gc_x)

    pad_rows = LANES - chunk
    gc_sq = jnp.concatenate([gc, jnp.zeros((pad_rows, LANES), F32)], axis=0) if pad_rows else gc
    gc_t = gc_sq.T

    ri = lax.broadcasted_iota(jnp.int32, (chunk, chunk), 0)
    ci = lax.broadcasted_iota(jnp.int32, (chunk, chunk), 1)
    eye = jnp.where(ri == ci, 1.0, 0.0).astype(F32)
    n_double = int(np.log2(chunk)) - 1
    assert 2 ** (n_double + 1) == chunk

    heads = range(HEADS)
    sls = [slice(hd * HEAD_DIM, (hd + 1) * HEAD_DIM) for hd in heads]
    decs = [jnp.where(ri >= ci, jnp.exp(gc[:, L_G + hd:L_G + hd + 1] - gc_t[L_G + hd:L_G + hd + 1, 0:chunk]), 0.0)
            for hd in heads]
    k_hs = [k[:, sl].astype(BF16) for sl in sls]
    pws = [-(_dot_nt(kb[:, sl].astype(BF16), k_h) * jnp.where(ri > ci, dec, 0.0))
           for sl, k_h, dec in zip(sls, k_hs, decs)]
    qks = [(_dot_nt(q[:, sl].astype(BF16), k_h) * dec).astype(BF16) for sl, k_h, dec in zip(sls, k_hs, decs)]
    tinvs = [eye + pw for pw in pws]
    for _ in range(n_double):
        pws = [pw.astype(BF16) for pw in pws]
        pws = [_dot(pw, pw) for pw in pws]
        tinvs = [tinv + _dot(tinv.astype(BF16), pw.astype(BF16)) for tinv, pw in zip(tinvs, pws)]
    uws = [_dot(tinv.astype(BF16), jnp.concatenate([vb[:, sl], kbe[:, sl]], axis=1).astype(BF16))
           for tinv, sl in zip(tinvs, sls)]
    s_fs = [state[hd] for hd in heads]
    s_bs = [s_f.astype(BF16) for s_f in s_fs]
    v_news = [(uw[:, 0:HEAD_DIM] - _dot(uw[:, HEAD_DIM:2 * HEAD_DIM].astype(BF16), s_b)).astype(BF16)
              for uw, s_b in zip(uws, s_bs)]
    outs = [_dot(qe[:, sl].astype(BF16), s_b) + _dot(qk, v_new)
            for sl, s_b, qk, v_new in zip(sls, s_bs, qks, v_news)]
    for hd in heads:
        state[hd] = s_fs[hd] * eglast_x[:, sls[hd]] + _dot_tn(kdec[:, sls[hd]].astype(BF16), v_news[hd])
    o_ref[...] = jnp.concatenate(outs, axis=1)

    @pl.when(c_idx == n_chunks - 1)
    def _():
        sfin_ref[0] = state[...]


def _gdn(gqkv, small, hist, s0, conv_w, batch, seq, chunk):
    assert seq % chunk == 0 and chunk % SUBLANES == 0 and chunk <= LANES
    nc = seq // chunk
    r = np.arange(chunk)
    tri = jnp.asarray(r[None, :] <= r[:, None], BF16)
    head_of = np.arange(WIDTH) // HEAD_DIM
    hsum = jnp.asarray(head_of[:, None] == (np.arange(LANES)[None, :] - L_G), BF16)
    hexp_g = jnp.asarray((np.arange(LANES)[:, None] - L_G) == head_of[None, :], BF16)
    hexp_b = jnp.asarray((np.arange(LANES)[:, None] - L_BETA) == head_of[None, :], BF16)
    conv_w8 = jnp.concatenate([conv_w, jnp.zeros((SUBLANES - CONV_WIDTH, CONV_DIM), F32)], axis=0)
    return pl.pallas_call(
        functools.partial(_gdn_kernel, chunk=chunk),
        grid=(batch, nc),
        in_specs=[pl.BlockSpec((chunk, CONV_DIM), lambda b, c: (b * nc + c, 0)),
                  pl.BlockSpec((chunk, LANES), lambda b, c: (b * nc + c, 0)),
                  pl.BlockSpec((1, SUBLANES, CONV_DIM), lambda b, c: (b, 0, 0)),
                  pl.BlockSpec((1, HEADS, HEAD_DIM, HEAD_DIM), lambda b, c: (b, 0, 0, 0)),
                  _const_spec((SUBLANES, CONV_DIM)), _const_spec((chunk, chunk)),
                  _const_spec((WIDTH, LANES)), _const_spec((LANES, WIDTH)), _const_spec((LANES, WIDTH))],
        out_specs=(pl.BlockSpec((chunk, WIDTH), lambda b, c: (b * nc + c, 0)),
                   pl.BlockSpec((1, HEADS, HEAD_DIM, HEAD_DIM), lambda b, c: (b, 0, 0, 0))),
        out_shape=(jax.ShapeDtypeStruct((batch * seq, WIDTH), F32),
                   jax.ShapeDtypeStruct((batch, HEADS, HEAD_DIM, HEAD_DIM), F32)),
        scratch_shapes=[pltpu.VMEM((chunk + SUBLANES, CONV_DIM), F32),
                        pltpu.VMEM((HEADS, HEAD_DIM, HEAD_DIM), F32)],
        compiler_params=pltpu.CompilerParams(dimension_semantics=("arbitrary", "arbitrary"),
                                             vmem_limit_bytes=VMEM_LIMIT),
        name="gdn",
    )(gqkv, small, hist, s0, conv_w8, tri, hsum, hexp_g, hexp_b)


GROUP = 4
GROUP_W = GROUP * HEAD_DIM


def _stack_heads(x, lane):
    return jnp.concatenate([jnp.where(lane == hh, x, jnp.zeros_like(x)) for hh in range(GROUP)], axis=0)


def _fold_heads(x, rows):
    out = x[0:rows]
    for hh in range(1, GROUP):
        out = out + x[hh * rows:(hh + 1) * rows]
    return out


def _gdn_local_kernel(gq_ref, prev_ref, small_ref, convw_ref, tri_ref, hsum_ref, hexp_g_ref, hexp_b_ref, dexp_ref,
                      u_ref, w_ref, qe_ref, kdec_ref, qkw_ref, egl_ref, xbuf, *, chunk):
    i = pl.program_id(1)
    tm = gq_ref.shape[0]
    hist_rows = SUBLANES
    prev = prev_ref[...]
    xbuf[0:hist_rows, :] = jnp.where(i == 0, jnp.zeros_like(prev), prev)
    xbuf[hist_rows:hist_rows + tm, :] = gq_ref[...]
    w = convw_ref[...]
    base = hist_rows - (CONV_WIDTH - 1)
    y = xbuf[base:base + tm, :] * w[0:1, :]
    for t in range(1, CONV_WIDTH):
        y = y + xbuf[base + t:base + t + tm, :] * w[t:t + 1, :]
    y = y * _sigmoid(y)

    hsum = hsum_ref[...]
    hexp_g = hexp_g_ref[...]
    hexp_b = hexp_b_ref[...]

    def l2n(x):
        inv = lax.rsqrt(_dot_sel(x * x, hsum, terms=2) + EPS)
        return x * _dot_sel(inv, hexp_g, terms=2)

    q = l2n(y[:, 0:WIDTH]) * (HEAD_DIM ** -0.5)
    k = l2n(y[:, WIDTH:2 * WIDTH])
    v = y[:, 2 * WIDTH:3 * WIDTH]

    small = small_ref[...]
    gc = _sel_dot(tri_ref[...], small)
    gc_x = _dot_sel(gc, hexp_g)
    beta_x = _dot_sel(small, hexp_b, terms=2)
    n_chunks = tm // chunk
    lasts = [gc_x[(c + 1) * chunk - 1:(c + 1) * chunk, :] for c in range(n_chunks)]
    glast_x = jnp.concatenate([jnp.broadcast_to(r, (chunk, WIDTH)) for r in lasts], axis=0)
    egc_x = jnp.exp(gc_x)
    kb = k * beta_x
    vb = (v * beta_x).astype(BF16)
    kbe = (kb * egc_x).astype(BF16)
    kb = kb.astype(BF16)
    qe_ref[...] = (q * egc_x).astype(BF16)
    kdec_ref[...] = (k * jnp.exp(glast_x - gc_x)).astype(BF16)
    for c in range(n_chunks):
        egl_ref[c] = jnp.broadcast_to(jnp.exp(lasts[c]), (SUBLANES, WIDTH))
    q = q.astype(BF16)
    k = k.astype(BF16)

    rows = GROUP * chunk
    lane = lax.broadcasted_iota(jnp.int32, (chunk, GROUP_W), 1)
    head_of_lane = lane // HEAD_DIM
    ri = lax.broadcasted_iota(jnp.int32, (rows, rows), 0)
    ci = lax.broadcasted_iota(jnp.int32, (rows, rows), 1)
    same = (ri // chunk) == (ci // chunk)
    incl = same & (ri >= ci)
    strict = same & (ri > ci)
    eye = jnp.where(ri == ci, 1.0, 0.0).astype(F32)
    n_double = int(np.log2(chunk)) - 1

    placed = None
    for t, part in enumerate(_split3(gc)):
        term = _dot(part, dexp_ref[t])
        placed = term if placed is None else placed + term
    lane_d = lax.broadcasted_iota(jnp.int32, (chunk, LANES), 1)
    head_d = jnp.where(lane_d < 6 * HEADS, lane_d // 6, -1)
    used = lane_d[0:1, :] < 6 * HEADS
    hi_half = lane_d[0:1, :] % 6 >= 3
    diff_l = (placed[:, 0:LANES] + jnp.where(used & hi_half, 1.0, 0.0)).astype(BF16)
    diff_r = (jnp.where(used & ~hi_half, 1.0, 0.0) - placed[:, LANES:2 * LANES]).astype(BF16)

    probs = [(slice(c * chunk, (c + 1) * chunk), slice(g * GROUP_W, (g + 1) * GROUP_W))
             for c in range(n_chunks) for g in range(HEADS // GROUP)]
    stack = lambda x, pr: _stack_heads(x[pr[0], pr[1]], head_of_lane)
    stack_d = lambda x, c, g: jnp.concatenate(
        [jnp.where(head_d == GROUP * g + hh, x[c * chunk:(c + 1) * chunk], jnp.zeros((chunk, LANES), BF16))
         for hh in range(GROUP)], axis=0)
    diffs = [_dot_nt(stack_d(diff_l, c, g), stack_d(diff_r, c, g))
             for c in range(n_chunks) for g in range(HEADS // GROUP)]
    aqs = [_dot_nt(jnp.concatenate([stack(kb, pr), stack(q, pr)], axis=0), stack(k, pr)) for pr in probs]
    decays = [jnp.exp(jnp.where(incl, d, -1e30)) for d in diffs]
    for pr, aq, decay in zip(probs, aqs, decays):
        qkw_ref[pr[0], pr[1]] = _fold_heads(aq[rows:2 * rows] * decay, chunk).astype(BF16)
    pws = [-(aq[0:rows] * jnp.where(strict, decay, 0.0)) for aq, decay in zip(aqs, decays)]
    tinvs = [eye + pw for pw in pws]
    pws = [pw.astype(BF16) for pw in pws]
    pws = [_dot(pw, pw).astype(BF16) for pw in pws]
    for it in range(n_double):
        if it < n_double - 1:
            both = [_dot(jnp.concatenate([pw, tinv.astype(BF16)], axis=0), pw) for pw, tinv in zip(pws, tinvs)]
            pws = [b[0:rows].astype(BF16) for b in both]
            tinvs = [tinv + b[rows:2 * rows] for tinv, b in zip(tinvs, both)]
        else:
            tinvs = [tinv + _dot(tinv.astype(BF16), pw) for pw, tinv in zip(pws, tinvs)]
    uws = [_dot(tinv.astype(BF16), jnp.concatenate([stack(vb, pr), stack(kbe, pr)], axis=1))
           for pr, tinv in zip(probs, tinvs)]
    for pr, uw in zip(probs, uws):
        u_ref[pr[0], pr[1]] = _fold_heads(uw[:, 0:GROUP_W], chunk).astype(BF16)
        w_ref[pr[0], pr[1]] = _fold_heads(uw[:, GROUP_W:2 * GROUP_W], chunk).astype(BF16)


def _gdn_scan_kernel(u_ref, w_ref, qe_ref, kdec_ref, qkw_ref, egl_ref, o_ref, sfin_ref, state, *, chunk):
    c_idx = pl.program_id(0)
    batch = u_ref.shape[0]

    @pl.when(c_idx == 0)
    def _():
        state[...] = jnp.zeros_like(state)

    rows = GROUP * chunk
    lane = lax.broadcasted_iota(jnp.int32, (chunk, GROUP_W), 1) // HEAD_DIM
    ri = lax.broadcasted_iota(jnp.int32, (GROUP_W, GROUP_W), 0)
    ci = lax.broadcasted_iota(jnp.int32, (GROUP_W, GROUP_W), 1)
    same = (ri // HEAD_DIM) == (ci // HEAD_DIM)
    probs = [(b, g, slice(g * GROUP_W, (g + 1) * GROUP_W)) for b in range(batch) for g in range(HEADS // GROUP)]
    s_fs = [state[b, g] for b, g, _ in probs]
    s_bs = [s_f.astype(BF16) for s_f in s_fs]
    v_news = [(u_ref[b, :, ls].astype(F32) - _dot(w_ref[b, :, ls], s_b)).astype(BF16)
              for (b, _, ls), s_b in zip(probs, s_bs)]
    for (b, _, ls), s_b, v_new in zip(probs, s_bs, v_news):
        o_ref[b, :, ls] = _dot(jnp.concatenate([qe_ref[b, :, ls], qkw_ref[b, :, ls]], axis=1),
                               jnp.concatenate([s_b, _stack_heads(v_new, lane)], axis=0))
    kvs = [_dot_tn(kdec_ref[b, :, ls], v_new) for (b, _, ls), v_new in zip(probs, v_news)]
    for (b, g, ls), s_f, kv in zip(probs, s_fs, kvs):
        state[b, g] = s_f * egl_ref[b, 0, 0:1, ls] + jnp.where(same, kv, 0.0)

    @pl.when(c_idx == pl.num_programs(0) - 1)
    def _():
        for b in range(batch):
            for hd in range(HEADS):
                g, hh = divmod(hd, GROUP)
                sl = slice(hh * HEAD_DIM, (hh + 1) * HEAD_DIM)
                sfin_ref[b, hd] = state[b, g, sl, sl]


def _gdn_long(gqkv, small, conv_w, batch, seq, chunk):
    tm = TOKEN_TILE
    assert seq % tm == 0 and tm % chunk == 0 and GROUP * chunk == GROUP_W
    nt = seq // tm
    nc = seq // chunk
    r = np.arange(tm)
    tri = jnp.asarray((r[None, :] <= r[:, None]) & ((r[None, :] // chunk) == (r[:, None] // chunk)), BF16)
    head_of = np.arange(WIDTH) // HEAD_DIM
    hsum = jnp.asarray(head_of[:, None] == (np.arange(LANES)[None, :] - L_G), BF16)
    hexp_g = jnp.asarray((np.arange(LANES)[:, None] - L_G) == head_of[None, :], BF16)
    hexp_b = jnp.asarray((np.arange(LANES)[:, None] - L_BETA) == head_of[None, :], BF16)
    conv_w8 = jnp.concatenate([conv_w, jnp.zeros((SUBLANES - CONV_WIDTH, CONV_DIM), F32)], axis=0)
    dexp = np.zeros((3, LANES, 2 * LANES), np.float32)
    for t in range(3):
        for hd in range(HEADS):
            dexp[t, L_G + hd, 6 * hd + t] = 1.0
            dexp[t, L_G + hd, LANES + 6 * hd + 3 + t] = 1.0
    dexp = jnp.asarray(dexp, BF16)
    row = lambda w: pl.BlockSpec((tm, w), lambda b, i: (b * nt + i, 0))
    per_tile = tm // chunk
    tok = jax.ShapeDtypeStruct((batch * seq, WIDTH), BF16)
    u, w, qe, kdec, qkw, egl = pl.pallas_call(
        functools.partial(_gdn_local_kernel, chunk=chunk),
        grid=(batch, nt),
        in_specs=[row(CONV_DIM),
                  pl.BlockSpec((SUBLANES, CONV_DIM),
                               lambda b, i: (jnp.maximum((b * nt + i) * (tm // SUBLANES) - 1, 0), 0)),
                  row(LANES), _const_spec((SUBLANES, CONV_DIM)), _const_spec((tm, tm)),
                  _const_spec((WIDTH, LANES)), _const_spec((LANES, WIDTH)), _const_spec((LANES, WIDTH)),
                  _const_spec((3, LANES, 2 * LANES))],
        out_specs=(row(WIDTH), row(WIDTH), row(WIDTH), row(WIDTH), row(WIDTH),
                   pl.BlockSpec((per_tile, SUBLANES, WIDTH), lambda b, i: (b * nt + i, 0, 0))),
        out_shape=(tok, tok, tok, tok, tok, jax.ShapeDtypeStruct((batch * nc, SUBLANES, WIDTH), F32)),
        scratch_shapes=[pltpu.VMEM((tm + SUBLANES, CONV_DIM), F32)],
        compiler_params=pltpu.CompilerParams(dimension_semantics=("arbitrary", "arbitrary"),
                                             vmem_limit_bytes=VMEM_LIMIT),
        name="gdn_local",
    )(gqkv, gqkv, small, conv_w8, tri, hsum, hexp_g, hexp_b, dexp)

    blk = pl.BlockSpec((batch, chunk, WIDTH), lambda c: (0, c, 0))
    as3 = lambda a: a.reshape(batch, seq, WIDTH)
    o, s_fin = pl.pallas_call(
        functools.partial(_gdn_scan_kernel, chunk=chunk),
        grid=(nc,),
        in_specs=[blk, blk, blk, blk, blk,
                  pl.BlockSpec((batch, 1, SUBLANES, WIDTH), lambda c: (0, c, 0, 0))],
        out_specs=(blk, pl.BlockSpec((batch, HEADS, HEAD_DIM, HEAD_DIM), lambda c: (0, 0, 0, 0))),
        out_shape=(jax.ShapeDtypeStruct((batch, seq, WIDTH), F32),
                   jax.ShapeDtypeStruct((batch, HEADS, HEAD_DIM, HEAD_DIM), F32)),
        scratch_shapes=[pltpu.VMEM((batch, HEADS // GROUP, GROUP_W, GROUP_W), F32)],
        compiler_params=pltpu.CompilerParams(dimension_semantics=("arbitrary",), vmem_limit_bytes=VMEM_LIMIT),
        name="gdn_scan",
    )(as3(u), as3(w), as3(qe), as3(kdec), as3(qkw), egl.reshape(batch, nc, SUBLANES, WIDTH))
    return o.reshape(batch * seq, WIDTH), s_fin


def _post_kernel(x_ref, fox_ref, gdn_ref, sgg_ref, sga_ref, sgb_ref,
                 wpa_ref, wpb_ref, wout_ref, wup_ref, wdown_ref,
                 ng_ref, gpost_ref, gpre2_ref, gpost2_ref, hsum_ref, hexp_ref, y_ref):
    ya = _dot(fox_ref[...], wpa_ref[...])
    o = gdn_ref[...]
    ms = _dot_sel(o * o, hsum_ref[...], terms=2) * (1.0 / HEAD_DIM)
    o = o * _dot_sel(lax.rsqrt(ms + EPS), hexp_ref[...], terms=2) * ng_ref[...] * sgg_ref[...].astype(F32)
    yb = _dot(o.astype(BF16), wpb_ref[...])
    m = sga_ref[...].astype(F32) * ya + sgb_ref[...].astype(F32) * yb
    mix = _dot(m.astype(BF16), wout_ref[...])
    y1 = x_ref[...] + _rms(mix, gpost_ref[...])
    h2 = _rms(y1, gpre2_ref[...]).astype(BF16)
    acc = jnp.zeros(y1.shape, F32)
    for c in range(D_FF // FF_TILE):
        u = jnp.maximum(_dot(h2, wup_ref[:, c * FF_TILE:(c + 1) * FF_TILE]), 0.0)
        acc = acc + _dot((u * u).astype(BF16), wdown_ref[c * FF_TILE:(c + 1) * FF_TILE, :])
    y_ref[...] = y1 + _rms(acc, gpost2_ref[...])


def _post(x, fox, gdn_o, sgg, sga, sgb, w_pa, w_pb, w_out, w_up, w_down, ng, gpost, gpre2, gpost2):
    t_total = x.shape[0]
    tm = TOKEN_TILE
    row = lambda w: pl.BlockSpec((tm, w), lambda i: (i, 0))
    head_of = np.arange(WIDTH) // HEAD_DIM
    hsum = jnp.asarray(head_of[:, None] == (np.arange(LANES)[None, :] - L_G), BF16)
    hexp = jnp.asarray((np.arange(LANES)[:, None] - L_G) == head_of[None, :], BF16)
    return pl.pallas_call(
        _post_kernel,
        grid=(t_total // tm,),
        in_specs=[row(D_MODEL), row(WIDTH), row(WIDTH), row(WIDTH), row(D_MODEL), row(D_MODEL),
                  _const_spec((WIDTH, D_MODEL)), _const_spec((WIDTH, D_MODEL)), _const_spec((D_MODEL, D_MODEL)),
                  _const_spec((D_MODEL, D_FF)), _const_spec((D_FF, D_MODEL)),
                  _const_spec((1, WIDTH)), _const_spec((1, D_MODEL)), _const_spec((1, D_MODEL)),
                  _const_spec((1, D_MODEL)), _const_spec((WIDTH, LANES)), _const_spec((LANES, WIDTH))],
        out_specs=row(D_MODEL),
        out_shape=jax.ShapeDtypeStruct((t_total, D_MODEL), F32),
        compiler_params=pltpu.CompilerParams(dimension_semantics=("arbitrary",), vmem_limit_bytes=VMEM_LIMIT),
        name="post",
    )(x, fox, gdn_o, sgg, sga, sgb, w_pa, w_pb, w_out, w_up, w_down, ng, gpost, gpre2, gpost2, hsum, hexp)


def _rearrange_w_in(w_in):
    o_ff = 3 * WIDTH
    o_gqkv = o_ff + HEADS
    o_ga = o_gqkv + CONV_DIM
    o_gb = o_ga + HEADS
    o_gg = o_gb + HEADS
    o_gate_a = o_gg + WIDTH
    o_gate_b = o_gate_a + D_MODEL
    cols = [w_in[:, 0:o_ff], w_in[:, o_gqkv:o_ga], w_in[:, o_gg:o_gate_a], w_in[:, o_gate_a:o_gate_b],
            w_in[:, o_gate_b:o_gate_b + D_MODEL], w_in[:, o_ff:o_gqkv], w_in[:, o_ga:o_gb], w_in[:, o_gb:o_gg],
            jnp.zeros((D_MODEL, LANES - 3 * HEADS), w_in.dtype)]
    return jnp.concatenate(cols, axis=1).astype(BF16)


def _lane_row(*pieces):
    v = jnp.concatenate([p.astype(F32) for p in pieces])
    return jnp.concatenate([v, jnp.zeros((LANES - v.shape[0],), F32)])[None, :]


def _pad_hist(conv_cache):
    b = conv_cache.shape[0]
    return jnp.concatenate([jnp.zeros((b, SUBLANES - (CONV_WIDTH - 1), CONV_DIM), F32), conv_cache.astype(F32)], axis=1)


def kernel(x_prompt, x_sample, cache_fox_k, cache_fox_v, cache_fox_logf, state_gdn, state_gdn_conv,
           w_in, fox_forget_bias, gdn_conv_w, gdn_a_log, gdn_dt_bias, gdn_norm_g,
           w_proj_fox, w_proj_gdn, w_out, norm_mix_pre, norm_mix_post, norm_mlp_pre, norm_mlp_post,
           w_up, w_down):
    depth = w_in.shape[0]
    bp, sp, _ = x_prompt.shape
    bs, ss, _ = x_sample.shape
    past = cache_fox_k.shape[2]
    y_p = x_prompt.reshape(bp * sp, D_MODEL)
    y_s = x_sample.reshape(bs * ss, D_MODEL)
    st_p, st_s = [], []
    for l in range(depth):
        w_r = _rearrange_w_in(w_in[l])
        padd = _lane_row(fox_forget_bias[l], gdn_dt_bias[l])
        palog = _lane_row(jnp.zeros((HEADS,), F32), gdn_a_log[l])
        gain = norm_mix_pre[l][None, :]
        post_w = (w_proj_fox[l].astype(BF16), w_proj_gdn[l].astype(BF16), w_out[l].astype(BF16),
                  w_up[l].astype(BF16), w_down[l].astype(BF16),
                  jnp.tile(gdn_norm_g[l], HEADS)[None, :], norm_mix_post[l][None, :],
                  norm_mlp_pre[l][None, :], norm_mlp_post[l][None, :])

        q, k, vt, fk, fv, gqkv, sgg, sga, sgb, small, cum = _in_proj(y_p, gain, w_r, padd, palog, sp)
        fox = _fox_prompt(q, k, vt, cum, bp, sp)
        gdn_o, s_fin = _gdn_long(gqkv, small, gdn_conv_w[l], bp, sp, GDN_CHUNK)
        y_p_new = _post(y_p, fox, gdn_o, sgg, sga, sgb, *post_w)
        st_p.append((fk.reshape(bp, sp, HEADS, HEAD_DIM), fv.reshape(bp, sp, HEADS, HEAD_DIM),
                     small[:, L_LOGF:L_LOGF + HEADS].reshape(bp, sp, HEADS), s_fin,
                     gqkv.reshape(bp, sp, CONV_DIM)[:, sp - (CONV_WIDTH - 1):, :]))
        y_p = y_p_new

        q, k, vt, fk, fv, gqkv, sgg, sga, sgb, small, cum = _in_proj(y_s, gain, w_r, padd, palog, ss)
        fox = _fox_sample(q, fk, fv, small,
                          cache_fox_k[l].reshape(bs, past, WIDTH), cache_fox_v[l].reshape(bs, past, WIDTH),
                          cache_fox_logf[l], bs, ss, past)
        gdn_o, s_new = _gdn(gqkv, small, _pad_hist(state_gdn_conv[l]), state_gdn[l].astype(F32),
                            gdn_conv_w[l], bs, ss, ss)
        y_s_new = _post(y_s, fox, gdn_o, sgg, sga, sgb, *post_w)
        conv_ext = jnp.concatenate([state_gdn_conv[l].astype(F32), gqkv.reshape(bs, ss, CONV_DIM)], axis=1)
        st_s.append((fk.reshape(bs, ss, HEADS, HEAD_DIM), fv.reshape(bs, ss, HEADS, HEAD_DIM),
                     small[:, L_LOGF:L_LOGF + HEADS].reshape(bs, ss, HEADS), s_new,
                     conv_ext[:, conv_ext.shape[1] - (CONV_WIDTH - 1):, :]))
        y_s = y_s_new

    fk_p, fv_p, lf_p, sg_p, cv_p = [jnp.stack(a) for a in zip(*st_p)]
    fk_s, fv_s, lf_s, sg_s, cv_s = [jnp.stack(a) for a in zip(*st_s)]
    return (y_p.reshape(bp, sp, D_MODEL), y_s.reshape(bs, ss, D_MODEL),
            fk_p, fv_p, lf_p, sg_p, cv_p, fk_s, fv_s, lf_s, sg_s, cv_s)
```

```python
import functools

import jax
import jax.numpy as jnp
import numpy as np
from jax import lax
from jax.experimental import pallas as pl
from jax.experimental.pallas import tpu as pltpu

F32 = jnp.float32
BF16 = jnp.bfloat16

D_MODEL = 1024
HEADS = 8
HEAD_DIM = 64
WIDTH = HEADS * HEAD_DIM
CONV_DIM = 3 * WIDTH
CONV_WIDTH = 4
D_FF = 4 * D_MODEL
EPS = 1e-6
LOG2E = 1.4426950408889634

LANES = 128
SUBLANES = 8
TOKEN_TILE = 256
POST_TILE = 512
ATTN_TILE = 256
ATTN_HEADS = 4
KEY_TILE = 512
FF_TILE = 1024
GDN_CHUNK = 64
VMEM_LIMIT = 56 * 1024 * 1024

C_Q, C_K, C_V, C_GQKV, C_GG, C_GA, C_GB, C_SMALL = 0, 512, 1024, 1536, 3072, 3584, 4608, 5632
IN_COLS = C_SMALL + LANES
L_LOGF, L_G, L_BETA = 0, 8, 16


def _split3(x):
    hi = x.astype(BF16)
    r = x - hi.astype(F32)
    mid = r.astype(BF16)
    lo = (r - mid.astype(F32)).astype(BF16)
    return hi, mid, lo


def _dot(a, b):
    return jnp.dot(a, b, preferred_element_type=F32)


def _dot_nt(a, b):
    return lax.dot_general(a, b, (((1,), (1,)), ((), ())), preferred_element_type=F32)


def _dot_tn(a, b):
    return lax.dot_general(a, b, (((0,), (0,)), ((), ())), preferred_element_type=F32)


def _sel_dot(sel, x, terms=3):
    parts = _split3(x)[:terms]
    n = x.shape[1]
    if n % LANES:
        return sum(_dot(sel, p) for p in parts)
    wide = _dot(sel, jnp.concatenate(parts, axis=1))
    return sum(wide[:, t * n:(t + 1) * n] for t in range(terms))


def _dot_sel(x, sel, terms=3):
    parts = _split3(x)[:terms]
    if terms == 1:
        return _dot(parts[0], sel)
    out = _dot(jnp.concatenate(parts[0:2], axis=1), jnp.concatenate([sel, sel], axis=0))
    for p in parts[2:]:
        out = out + _dot(p, sel)
    return out


def _sigmoid(x):
    return 1.0 / (1.0 + jnp.exp(-x))


def _rms(x, gain):
    ms = jnp.mean(x * x, axis=-1, keepdims=True)
    return x * lax.rsqrt(ms + EPS) * gain


def _const_spec(shape):
    nd = len(shape)
    return pl.BlockSpec(shape, lambda *_: (0,) * nd, pipeline_mode=pl.Buffered(1))


def _inproj_kernel(x_ref, gain_ref, w_ref, padd_ref, palog_ref, tri_ref,
                   q_ref, k_ref, vt_ref, fk_ref, fv_ref, gqkv_ref, sgg_ref, sga_ref, sgb_ref,
                   small_ref, cum_ref, carry_ref, *, tiles_per_seg):
    i = pl.program_id(0)
    h = _rms(x_ref[...], gain_ref[...]).astype(BF16)

    def mm(c0, width):
        return _dot(h, w_ref[:, c0:c0 + width])

    zq = mm(C_Q, WIDTH)
    q_ref[...] = (zq * (LOG2E * HEAD_DIM ** -0.5)).astype(BF16)
    zk = mm(C_K, WIDTH)
    fk_ref[...] = zk
    k_ref[...] = zk.astype(BF16)
    zv = mm(C_V, WIDTH)
    fv_ref[...] = zv
    vt_ref[0] = zv.T.astype(BF16)
    for c in range(3):
        gqkv_ref[:, c * WIDTH:(c + 1) * WIDTH] = mm(C_GQKV + c * WIDTH, WIDTH)
    zg = mm(C_GG, WIDTH)
    sgg_ref[...] = (zg * _sigmoid(zg)).astype(BF16)
    for c in range(2):
        sga_ref[:, c * WIDTH:(c + 1) * WIDTH] = _sigmoid(mm(C_GA + c * WIDTH, WIDTH)).astype(BF16)
        sgb_ref[:, c * WIDTH:(c + 1) * WIDTH] = _sigmoid(mm(C_GB + c * WIDTH, WIDTH)).astype(BF16)

    zs = mm(C_SMALL, LANES)
    t = zs + padd_ref[...]
    l1p = jnp.log1p(jnp.exp(-jnp.abs(t)))
    logf = jnp.minimum(t, 0.0) - l1p
    g = -jnp.exp(palog_ref[...]) * (jnp.maximum(t, 0.0) + l1p)
    beta = _sigmoid(zs)
    lane = lax.broadcasted_iota(jnp.int32, zs.shape, 1)
    small = jnp.where(lane < L_G, logf, jnp.where(lane < L_BETA, g, jnp.where(lane < L_BETA + HEADS, beta, 0.0)))
    small_ref[...] = small

    cum = _sel_dot(tri_ref[...], small)
    if tiles_per_seg > 1:
        @pl.when(i % tiles_per_seg == 0)
        def _():
            carry_ref[...] = jnp.zeros_like(carry_ref)
        cum = cum + carry_ref[0:1, :]
        carry_ref[...] = jnp.broadcast_to(cum[-1:, :], carry_ref.shape)
    cum_ref[...] = cum


def _in_proj(x, gain, w_r, padd, palog, seg_len):
    t_total = x.shape[0]
    tm = TOKEN_TILE
    assert t_total % tm == 0
    nt = t_total // tm
    if seg_len >= tm:
        assert seg_len % tm == 0
        tiles_per_seg = seg_len // tm
        r = np.arange(tm)
        tri = (r[None, :] <= r[:, None])
    else:
        assert tm % seg_len == 0
        tiles_per_seg = 1
        r = np.arange(tm)
        tri = (r[None, :] <= r[:, None]) & ((r[None, :] // seg_len) == (r[:, None] // seg_len))
    tri = jnp.asarray(tri, BF16)

    row = lambda w: pl.BlockSpec((tm, w), lambda i: (i, 0))
    out_shape = (
        jax.ShapeDtypeStruct((t_total, WIDTH), BF16),
        jax.ShapeDtypeStruct((t_total, WIDTH), BF16),
        jax.ShapeDtypeStruct((nt, WIDTH, tm), BF16),
        jax.ShapeDtypeStruct((t_total, WIDTH), F32),
        jax.ShapeDtypeStruct((t_total, WIDTH), F32),
        jax.ShapeDtypeStruct((t_total, CONV_DIM), F32),
        jax.ShapeDtypeStruct((t_total, WIDTH), BF16),
        jax.ShapeDtypeStruct((t_total, D_MODEL), BF16),
        jax.ShapeDtypeStruct((t_total, D_MODEL), BF16),
        jax.ShapeDtypeStruct((t_total, LANES), F32),
        jax.ShapeDtypeStruct((t_total, LANES), F32),
    )
    out_specs = (row(WIDTH), row(WIDTH), pl.BlockSpec((1, WIDTH, tm), lambda i: (i, 0, 0)),
                 row(WIDTH), row(WIDTH), row(CONV_DIM), row(WIDTH), row(D_MODEL), row(D_MODEL),
                 row(LANES), row(LANES))
    return pl.pallas_call(
        functools.partial(_inproj_kernel, tiles_per_seg=tiles_per_seg),
        grid=(nt,),
        in_specs=[row(D_MODEL), _const_spec((1, D_MODEL)), _const_spec((D_MODEL, IN_COLS)),
                  _const_spec((1, LANES)), _const_spec((1, LANES)), _const_spec((tm, tm))],
        out_specs=out_specs,
        out_shape=out_shape,
        scratch_shapes=[pltpu.VMEM((SUBLANES, LANES), F32)],
        compiler_params=pltpu.CompilerParams(dimension_semantics=("arbitrary",), vmem_limit_bytes=VMEM_LIMIT),
        name="in_proj",
    )(x, gain, w_r, padd, palog, tri)


def _fox_prompt_kernel(q_ref, k_ref, vt_ref, cum_ref, o_ref, bias_ref, s0_ref, s1_ref, p0_ref, p1_ref,
                       *, seq, tq, tk, nh):
    grp = pl.program_id(1)
    qi = pl.program_id(2)
    rows = 512
    vt_tile = vt_ref.shape[2]
    sub = tk // vt_tile
    heads = range(nh)
    s_slots = (s0_ref, s1_ref)
    p_slots = (p0_ref, p1_ref)

    @pl.when(qi == 0)
    def _prep():
        def fill(r, carry):
            c = cum_ref[pl.ds(r * rows, rows), :]
            lane = lax.broadcasted_iota(jnp.int32, c.shape, 1)
            slab = jnp.zeros(c.shape, F32)
            for hh in heads:
                col = jnp.sum(jnp.where(lane == nh * grp + hh, c, 0.0), axis=-1, keepdims=True) * (-LOG2E)
                for t, part in enumerate(_split3(col)):
                    slab = jnp.where(lane == 3 * hh + t, part.astype(F32), slab)
            bias_ref[pl.ds(r * rows, rows), :] = slab.astype(BF16)
            return carry
        lax.fori_loop(0, seq // rows, fill, 0)

    lane_q = lax.broadcasted_iota(jnp.int32, (tq, LANES), 1)
    qh = []
    for hh in heads:
        q2 = q_ref[:, (hh // 2) * LANES:(hh // 2 + 1) * LANES]
        half = hh % 2
        q_m = jnp.where((lane_q >= HEAD_DIM * half) & (lane_q < HEAD_DIM * (half + 1)), q2, jnp.zeros_like(q2))
        ones = jnp.where((lane_q >= 3 * hh) & (lane_q < 3 * hh + 3), 1.0, 0.0)
        qh.append(jnp.concatenate([q_m.astype(F32), ones], axis=1).T.astype(BF16))

    def scores_to(slot, j):
        bias = bias_ref[pl.ds(j * tk, tk), :]
        kjs = [jnp.concatenate([k_ref[pl.ds(j * tk, tk), sl * LANES:(sl + 1) * LANES], bias], axis=1)
               for sl in range(nh // 2)]
        s_ts = [_dot(kjs[hh // 2], qh[hh]) for hh in heads]
        for hh in heads:
            s_slots[slot][hh] = s_ts[hh]
        return tuple(jnp.max(s_t, axis=0, keepdims=True) for s_t in s_ts)

    ones_rows = jnp.ones((2 * SUBLANES, vt_tile), BF16)

    def add_values(slot, j, accs, alphas):
        pvs = [sum(_dot(jnp.concatenate([vt_ref[j * sub + t, HEAD_DIM * hh:HEAD_DIM * (hh + 1), :], ones_rows], axis=0),
                        p_slots[slot][hh, t * vt_tile:(t + 1) * vt_tile, :]) for t in range(sub)) for hh in heads]
        return tuple(alphas[hh] * accs[hh] + pvs[hh] for hh in heads)

    def rescale(ms, bms):
        m_new = tuple(jnp.maximum(ms[hh], bms[hh]) for hh in heads)
        return m_new, tuple(jnp.exp2(ms[hh] - m_new[hh]) for hh in heads)

    def probabilities(slot, hh, m_new, visible_from):
        s_t = s_slots[slot][hh]
        if visible_from is not None:
            s_t = jnp.where(visible_from, s_t, -1e30)
        p_slots[slot][hh] = jnp.exp2(s_t - m_new).astype(BF16)

    def trip(j, rd, carry):
        ms, accs, a_prev, bms = carry
        wr = 1 - rd
        accs = add_values(wr, jnp.maximum(j - 1, 0), accs, a_prev)
        m_new, alphas = rescale(ms, bms)
        for hh in heads:
            probabilities(rd, hh, m_new[hh], None)
        bm_next = scores_to(wr, j + 1)
        return m_new, accs, alphas, bm_next

    n_full = (qi * tq) // tk

    def last(rd, carry):
        ms, accs, a_prev, _ = carry
        accs = add_values(1 - rd, jnp.maximum(n_full - 1, 0), accs, a_prev)
        kr = lax.broadcasted_iota(jnp.int32, (tk, tq), 0)
        qc = lax.broadcasted_iota(jnp.int32, (tk, tq), 1)
        visible = kr <= qc + (qi * tq - n_full * tk)
        bms = tuple(jnp.max(jnp.where(visible, s_slots[rd][hh], -1e30), axis=0, keepdims=True) for hh in heads)
        m_new, alphas = rescale(ms, bms)
        for hh in heads:
            probabilities(rd, hh, m_new[hh], visible)
        accs = add_values(rd, n_full, accs, alphas)
        return jnp.concatenate([accs[hh][0:HEAD_DIM] / accs[hh][HEAD_DIM:HEAD_DIM + 1] for hh in heads], axis=0)

    each = lambda f: tuple(f() for _ in heads)
    p1_ref[...] = jnp.zeros(p1_ref.shape, BF16)
    init = (each(lambda: jnp.full((1, tq), -1e30, F32)), each(lambda: jnp.zeros((HEAD_DIM + 2 * SUBLANES, tq), F32)),
            each(lambda: jnp.ones((1, tq), F32)), scores_to(0, 0))
    carry = lax.fori_loop(
        0, n_full, lambda j, c: lax.cond(j % 2 == 0, lambda c: trip(j, 0, c), lambda c: trip(j, 1, c), c), init)
    o_t = lax.cond(n_full % 2 == 0, lambda c: last(0, c), lambda c: last(1, c), carry)
    o_ref[...] = o_t.T.astype(BF16)


def _fox_prompt(q, k, vt, cum, batch, seq):
    tq, tk, nh = ATTN_TILE, KEY_TILE, ATTN_HEADS
    assert seq % tk == 0 and tk % tq == 0 and tk % TOKEN_TILE == 0 and HEADS % nh == 0 and nh % 2 == 0
    nq = seq // tq
    width = nh * HEAD_DIM
    return pl.pallas_call(
        functools.partial(_fox_prompt_kernel, seq=seq, tq=tq, tk=tk, nh=nh),
        grid=(batch, HEADS // nh, nq),
        in_specs=[pl.BlockSpec((tq, width), lambda b, g, i: (b * nq + i, g)),
                  pl.BlockSpec((seq, width), lambda b, g, i: (b, g)),
                  pl.BlockSpec((seq // TOKEN_TILE, width, TOKEN_TILE), lambda b, g, i: (b, g, 0)),
                  pl.BlockSpec((seq, LANES), lambda b, g, i: (b, 0))],
        out_specs=pl.BlockSpec((tq, width), lambda b, g, i: (b * nq + i, g)),
        out_shape=jax.ShapeDtypeStruct((batch * seq, WIDTH), BF16),
        scratch_shapes=[pltpu.VMEM((seq, LANES), BF16),
                        pltpu.VMEM((nh, tk, tq), F32), pltpu.VMEM((nh, tk, tq), F32),
                        pltpu.VMEM((nh, tk, tq), BF16), pltpu.VMEM((nh, tk, tq), BF16)],
        compiler_params=pltpu.CompilerParams(dimension_semantics=("arbitrary",) * 3, vmem_limit_bytes=VMEM_LIMIT),
        name="fox_prompt",
    )(q, k, vt, cum)


def _fox_sample_kernel(q_ref, kn_ref, vn_ref, small_ref, kc_ref, vc_ref, lfc_ref, tri_ref, trin_ref, o_ref,
                       *, past, steps):
    blk = tri_ref.shape[0]
    carry = jnp.zeros((1, HEADS), F32)
    cums = []
    for r in range(past // blk):
        c = _sel_dot(tri_ref[...], lfc_ref[0, r * blk:(r + 1) * blk, :]) + carry
        cums.append(c)
        carry = c[-1:, :]
    cum_c = jnp.concatenate(cums, axis=0)
    cum_n = _sel_dot(trin_ref[...], small_ref[:, L_LOGF:L_LOGF + HEADS]) + carry

    q = q_ref[...]
    kc = kc_ref[0].astype(BF16)
    vc = vc_ref[0].astype(BF16)
    kn = kn_ref[...].astype(BF16)
    vn = vn_ref[...].astype(BF16)
    lane_c = lax.broadcasted_iota(jnp.int32, (past, LANES), 1)
    lane_n = lax.broadcasted_iota(jnp.int32, (steps, LANES), 1)
    kr = lax.broadcasted_iota(jnp.int32, (steps, steps), 0)
    qc = lax.broadcasted_iota(jnp.int32, (steps, steps), 1)
    ones_c = jnp.ones((past, LANES), BF16)
    ones_n = jnp.ones((steps, LANES), BF16)
    slabs = []
    for pr in range(HEADS // 2):
        ls = slice(pr * LANES, (pr + 1) * LANES)
        q_s, kc_s, kn_s, vc_s, vn_s = q[:, ls], kc[:, ls], kn[:, ls], vc[:, ls], vn[:, ls]
        o_s = jnp.zeros((steps, LANES), F32)
        for hh in range(2):
            hd = 2 * pr + hh
            in_c = (lane_c >= HEAD_DIM * hh) & (lane_c < HEAD_DIM * (hh + 1))
            in_n = (lane_n >= HEAD_DIM * hh) & (lane_n < HEAD_DIM * (hh + 1))
            qm = jnp.where(in_n, q_s, jnp.zeros_like(q_s))
            s1 = _dot_nt(kc_s, qm) - LOG2E * cum_c[:, hd:hd + 1]
            s2 = _dot_nt(kn_s, qm) - LOG2E * cum_n[:, hd:hd + 1]
            s2 = jnp.where(kr <= qc, s2, -1e30)
            m = jnp.maximum(jnp.max(s1, axis=0, keepdims=True), jnp.max(s2, axis=0, keepdims=True))
            p1 = jnp.exp2(s1 - m).astype(BF16)
            p2 = jnp.exp2(s2 - m).astype(BF16)
            num = _dot_tn(p1, jnp.where(in_c, vc_s, jnp.zeros_like(vc_s))) + \
                _dot_tn(p2, jnp.where(in_n, vn_s, jnp.zeros_like(vn_s)))
            den = _dot_tn(p1, ones_c) + _dot_tn(p2, ones_n)
            o_s = o_s + num / den
        slabs.append(o_s)
    o_ref[...] = jnp.concatenate(slabs, axis=1).astype(BF16)


def _fox_sample(q, kn, vn, small, kc, vc, lfc, batch, steps, past):
    blk = 256
    assert past % blk == 0
    r = np.arange(blk)
    tri = jnp.asarray(r[None, :] <= r[:, None], BF16)
    rn = np.arange(steps)
    trin = jnp.asarray(rn[None, :] <= rn[:, None], BF16)
    row = lambda w: pl.BlockSpec((steps, w), lambda b: (b, 0))
    return pl.pallas_call(
        functools.partial(_fox_sample_kernel, past=past, steps=steps),
        grid=(batch,),
        in_specs=[row(WIDTH), row(WIDTH), row(WIDTH), row(LANES),
                  pl.BlockSpec((1, past, WIDTH), lambda b: (b, 0, 0)),
                  pl.BlockSpec((1, past, WIDTH), lambda b: (b, 0, 0)),
                  pl.BlockSpec((1, past, HEADS), lambda b: (b, 0, 0)),
                  _const_spec((blk, blk)), _const_spec((steps, steps))],
        out_specs=row(WIDTH),
        out_shape=jax.ShapeDtypeStruct((batch * steps, WIDTH), BF16),
        compiler_params=pltpu.CompilerParams(dimension_semantics=("arbitrary",), vmem_limit_bytes=VMEM_LIMIT),
        name="fox_sample",
    )(q, kn, vn, small, kc, vc, lfc, tri, trin)


def _gdn_kernel(gq_ref, small_ref, hist_ref, s0_ref, convw_ref, tri_ref, hsum_ref, hexp_g_ref, hexp_b_ref,
                o_ref, sfin_ref, xbuf, state, *, chunk):
    c_idx = pl.program_id(1)
    n_chunks = pl.num_programs(1)
    hist_rows = SUBLANES

    @pl.when(c_idx == 0)
    def _():
        xbuf[0:hist_rows, :] = hist_ref[0]
        state[...] = s0_ref[0]

    xbuf[hist_rows:hist_rows + chunk, :] = gq_ref[...]
    w = convw_ref[...]
    base = hist_rows - (CONV_WIDTH - 1)
    y = xbuf[base:base + chunk, :] * w[0:1, :]
    for i in range(1, CONV_WIDTH):
        y = y + xbuf[base + i:base + i + chunk, :] * w[i:i + 1, :]
    xbuf[0:hist_rows, :] = xbuf[chunk:chunk + hist_rows, :]
    y = y * _sigmoid(y)

    hsum = hsum_ref[...]
    hexp_g = hexp_g_ref[...]
    hexp_b = hexp_b_ref[...]

    def l2n(x):
        inv = lax.rsqrt(_dot_sel(x * x, hsum, terms=2) + EPS)
        return x * _dot_sel(inv, hexp_g, terms=2)

    q = l2n(y[:, 0:WIDTH]) * (HEAD_DIM ** -0.5)
    k = l2n(y[:, WIDTH:2 * WIDTH])
    v = y[:, 2 * WIDTH:3 * WIDTH]

    small = small_ref[...]
    gc = _sel_dot(tri_ref[...], small)
    gc_x = _dot_sel(gc, hexp_g)
    beta_x = _dot_sel(small, hexp_b, terms=2)
    egc_x = jnp.exp(gc_x)
    glast_x = gc_x[chunk - 1:chunk, :]
    eglast_x = jnp.exp(glast_x)
    kb = k * beta_x
    vb = v * beta_x
    kbe = kb * egc_x
    qe = q * egc_x
    kdec = k * jnp.exp(glast_x - gc_x)

    pad_rows = LANES - chunk
    gc_sq = jnp.concatenate([gc, jnp.zeros((pad_rows, LANES), F32)], axis=0) if pad_rows else gc
    gc_t = gc_sq.T

    ri = lax.broadcasted_iota(jnp.int32, (chunk, chunk), 0)
    ci = lax.broadcasted_iota(jnp.int32, (chunk, chunk), 1)
    eye = jnp.where(ri == ci, 1.0, 0.0).astype(F32)
    n_double = int(np.log2(chunk)) - 1
    assert 2 ** (n_double + 1) == chunk

    heads = range(HEADS)
    sls = [slice(hd * HEAD_DIM, (hd + 1) * HEAD_DIM) for hd in heads]
    decs = [jnp.where(ri >= ci, jnp.exp(gc[:, L_G + hd:L_G + hd + 1] - gc_t[L_G + hd:L_G + hd + 1, 0:chunk]), 0.0)
            for hd in heads]
    k_hs = [k[:, sl].astype(BF16) for sl in sls]
    pws = [-(_dot_nt(kb[:, sl].astype(BF16), k_h) * jnp.where(ri > ci, dec, 0.0))
           for sl, k_h, dec in zip(sls, k_hs, decs)]
    qks = [(_dot_nt(q[:, sl].astype(BF16), k_h) * dec).astype(BF16) for sl, k_h, dec in zip(sls, k_hs, decs)]
    tinvs = [eye + pw for pw in pws]
    for _ in range(n_double):
        pws = [pw.astype(BF16) for pw in pws]
        pws = [_dot(pw, pw) for pw in pws]
        tinvs = [tinv + _dot(tinv.astype(BF16), pw.astype(BF16)) for tinv, pw in zip(tinvs, pws)]
    uws = [_dot(tinv.astype(BF16), jnp.concatenate([vb[:, sl], kbe[:, sl]], axis=1).astype(BF16))
           for tinv, sl in zip(tinvs, sls)]
    s_fs = [state[hd] for hd in heads]
    s_bs = [s_f.astype(BF16) for s_f in s_fs]
    v_news = [(uw[:, 0:HEAD_DIM] - _dot(uw[:, HEAD_DIM:2 * HEAD_DIM].astype(BF16), s_b)).astype(BF16)
              for uw, s_b in zip(uws, s_bs)]
    outs = [_dot(qe[:, sl].astype(BF16), s_b) + _dot(qk, v_new)
            for sl, s_b, qk, v_new in zip(sls, s_bs, qks, v_news)]
    for hd in heads:
        state[hd] = s_fs[hd] * eglast_x[:, sls[hd]] + _dot_tn(kdec[:, sls[hd]].astype(BF16), v_news[hd])
    o_ref[...] = jnp.concatenate(outs, axis=1)

    @pl.when(c_idx == n_chunks - 1)
    def _():
        sfin_ref[0] = state[...]


def _gdn(gqkv, small, hist, s0, conv_w, batch, seq, chunk):
    assert seq % chunk == 0 and chunk % SUBLANES == 0 and chunk <= LANES
    nc = seq // chunk
    r = np.arange(chunk)
    tri = jnp.asarray(r[None, :] <= r[:, None], BF16)
    head_of = np.arange(WIDTH) // HEAD_DIM
    hsum = jnp.asarray(head_of[:, None] == (np.arange(LANES)[None, :] - L_G), BF16)
    hexp_g = jnp.asarray((np.arange(LANES)[:, None] - L_G) == head_of[None, :], BF16)
    hexp_b = jnp.asarray((np.arange(LANES)[:, None] - L_BETA) == head_of[None, :], BF16)
    conv_w8 = jnp.concatenate([conv_w, jnp.zeros((SUBLANES - CONV_WIDTH, CONV_DIM), F32)], axis=0)
    return pl.pallas_call(
        functools.partial(_gdn_kernel, chunk=chunk),
        grid=(batch, nc),
        in_specs=[pl.BlockSpec((chunk, CONV_DIM), lambda b, c: (b * nc + c, 0)),
                  pl.BlockSpec((chunk, LANES), lambda b, c: (b * nc + c, 0)),
                  pl.BlockSpec((1, SUBLANES, CONV_DIM), lambda b, c: (b, 0, 0)),
                  pl.BlockSpec((1, HEADS, HEAD_DIM, HEAD_DIM), lambda b, c: (b, 0, 0, 0)),
                  _const_spec((SUBLANES, CONV_DIM)), _const_spec((chunk, chunk)),
                  _const_spec((WIDTH, LANES)), _const_spec((LANES, WIDTH)), _const_spec((LANES, WIDTH))],
        out_specs=(pl.BlockSpec((chunk, WIDTH), lambda b, c: (b * nc + c, 0)),
                   pl.BlockSpec((1, HEADS, HEAD_DIM, HEAD_DIM), lambda b, c: (b, 0, 0, 0))),
        out_shape=(jax.ShapeDtypeStruct((batch * seq, WIDTH), F32),
                   jax.ShapeDtypeStruct((batch, HEADS, HEAD_DIM, HEAD_DIM), F32)),
        scratch_shapes=[pltpu.VMEM((chunk + SUBLANES, CONV_DIM), F32),
                        pltpu.VMEM((HEADS, HEAD_DIM, HEAD_DIM), F32)],
        compiler_params=pltpu.CompilerParams(dimension_semantics=("arbitrary", "arbitrary"),
                                             vmem_limit_bytes=VMEM_LIMIT),
        name="gdn",
    )(gqkv, small, hist, s0, conv_w8, tri, hsum, hexp_g, hexp_b)


GROUP = 4
GROUP_W = GROUP * HEAD_DIM


def _stack_heads(x, lane):
    return jnp.concatenate([jnp.where(lane == hh, x, jnp.zeros_like(x)) for hh in range(GROUP)], axis=0)


def _fold_heads(x, rows):
    out = x[0:rows]
    for hh in range(1, GROUP):
        out = out + x[hh * rows:(hh + 1) * rows]
    return out


def _gdn_local_kernel(gq_ref, prev_ref, small_ref, convw_ref, tri_ref, hsum_ref, hexp_g_ref, hexp_b_ref, dexp_ref,
                      u_ref, w_ref, qe_ref, kdec_ref, qkw_ref, egl_ref, xbuf, *, chunk):
    i = pl.program_id(1)
    tm = gq_ref.shape[0]
    hist_rows = SUBLANES
    prev = prev_ref[...]
    xbuf[0:hist_rows, :] = jnp.where(i == 0, jnp.zeros_like(prev), prev)
    xbuf[hist_rows:hist_rows + tm, :] = gq_ref[...]
    w = convw_ref[...]
    base = hist_rows - (CONV_WIDTH - 1)
    y = xbuf[base:base + tm, :] * w[0:1, :]
    for t in range(1, CONV_WIDTH):
        y = y + xbuf[base + t:base + t + tm, :] * w[t:t + 1, :]
    y = y * _sigmoid(y)

    hsum = hsum_ref[...]
    hexp_g = hexp_g_ref[...]
    hexp_b = hexp_b_ref[...]

    def l2n(x):
        inv = lax.rsqrt(_dot_sel(x * x, hsum, terms=2) + EPS)
        return x * _dot_sel(inv, hexp_g, terms=2)

    q = l2n(y[:, 0:WIDTH]) * (HEAD_DIM ** -0.5)
    k = l2n(y[:, WIDTH:2 * WIDTH])
    v = y[:, 2 * WIDTH:3 * WIDTH]

    small = small_ref[...]
    gc = _sel_dot(tri_ref[...], small)
    gc_x = _dot_sel(gc, hexp_g)
    beta_x = _dot_sel(small, hexp_b, terms=2)
    n_chunks = tm // chunk
    lasts = [gc_x[(c + 1) * chunk - 1:(c + 1) * chunk, :] for c in range(n_chunks)]
    glast_x = jnp.concatenate([jnp.broadcast_to(r, (chunk, WIDTH)) for r in lasts], axis=0)
    egc_x = jnp.exp(gc_x)
    kb = k * beta_x
    vb = (v * beta_x).astype(BF16)
    kbe = (kb * egc_x).astype(BF16)
    kb = kb.astype(BF16)
    qe_ref[...] = (q * egc_x).astype(BF16)
    kdec_ref[...] = (k * jnp.exp(glast_x - gc_x)).astype(BF16)
    for c in range(n_chunks):
        egl_ref[c] = jnp.broadcast_to(jnp.exp(lasts[c]), (SUBLANES, WIDTH))
    q = q.astype(BF16)
    k = k.astype(BF16)

    rows = GROUP * chunk
    lane = lax.broadcasted_iota(jnp.int32, (chunk, GROUP_W), 1)
    head_of_lane = lane // HEAD_DIM
    ri = lax.broadcasted_iota(jnp.int32, (rows, rows), 0)
    ci = lax.broadcasted_iota(jnp.int32, (rows, rows), 1)
    same = (ri // chunk) == (ci // chunk)
    incl = same & (ri >= ci)
    strict = same & (ri > ci)
    eye = jnp.where(ri == ci, 1.0, 0.0).astype(F32)
    n_double = int(np.log2(chunk)) - 1

    placed = None
    for t, part in enumerate(_split3(gc)):
        term = _dot(part, dexp_ref[t])
        placed = term if placed is None else placed + term
    lane_d = lax.broadcasted_iota(jnp.int32, (chunk, LANES), 1)
    head_d = jnp.where(lane_d < 6 * HEADS, lane_d // 6, -1)
    used = lane_d[0:1, :] < 6 * HEADS
    hi_half = lane_d[0:1, :] % 6 >= 3
    diff_l = (placed[:, 0:LANES] + jnp.where(used & hi_half, 1.0, 0.0)).astype(BF16)
    diff_r = (jnp.where(used & ~hi_half, 1.0, 0.0) - placed[:, LANES:2 * LANES]).astype(BF16)

    probs = [(slice(c * chunk, (c + 1) * chunk), slice(g * GROUP_W, (g + 1) * GROUP_W))
             for c in range(n_chunks) for g in range(HEADS // GROUP)]
    stack = lambda x, pr: _stack_heads(x[pr[0], pr[1]], head_of_lane)
    stack_d = lambda x, c, g: jnp.concatenate(
        [jnp.where(head_d == GROUP * g + hh, x[c * chunk:(c + 1) * chunk], jnp.zeros((chunk, LANES), BF16))
         for hh in range(GROUP)], axis=0)
    diffs = [_dot_nt(stack_d(diff_l, c, g), stack_d(diff_r, c, g))
             for c in range(n_chunks) for g in range(HEADS // GROUP)]
    aqs = [_dot_nt(jnp.concatenate([stack(kb, pr), stack(q, pr)], axis=0), stack(k, pr)) for pr in probs]
    decays = [jnp.exp(jnp.where(incl, d, -1e30)) for d in diffs]
    for pr, aq, decay in zip(probs, aqs, decays):
        qkw_ref[pr[0], pr[1]] = _fold_heads(aq[rows:2 * rows] * decay, chunk).astype(BF16)
    pws = [-(aq[0:rows] * jnp.where(strict, decay, 0.0)) for aq, decay in zip(aqs, decays)]
    tinvs = [eye + pw for pw in pws]
    pws = [pw.astype(BF16) for pw in pws]
    pws = [_dot(pw, pw).astype(BF16) for pw in pws]
    for it in range(n_double):
        if it < n_double - 1:
            both = [_dot(jnp.concatenate([pw, tinv.astype(BF16)], axis=0), pw) for pw, tinv in zip(pws, tinvs)]
            pws = [b[0:rows].astype(BF16) for b in both]
            tinvs = [tinv + b[rows:2 * rows] for tinv, b in zip(tinvs, both)]
        else:
            tinvs = [tinv + _dot(tinv.astype(BF16), pw) for pw, tinv in zip(pws, tinvs)]
    uws = [_dot(tinv.astype(BF16), jnp.concatenate([stack(vb, pr), stack(kbe, pr)], axis=1))
           for pr, tinv in zip(probs, tinvs)]
    for pr, uw in zip(probs, uws):
        u_ref[pr[0], pr[1]] = _fold_heads(uw[:, 0:GROUP_W], chunk).astype(BF16)
        w_ref[pr[0], pr[1]] = _fold_heads(uw[:, GROUP_W:2 * GROUP_W], chunk).astype(BF16)


def _gdn_scan_kernel(u_ref, w_ref, qe_ref, kdec_ref, qkw_ref, egl_ref, o_ref, sfin_ref, state, *, chunk):
    c_idx = pl.program_id(0)
    batch = u_ref.shape[0]

    @pl.when(c_idx == 0)
    def _():
        state[...] = jnp.zeros_like(state)

    rows = GROUP * chunk
    lane = lax.broadcasted_iota(jnp.int32, (chunk, GROUP_W), 1) // HEAD_DIM
    ri = lax.broadcasted_iota(jnp.int32, (GROUP_W, GROUP_W), 0)
    ci = lax.broadcasted_iota(jnp.int32, (GROUP_W, GROUP_W), 1)
    same = (ri // HEAD_DIM) == (ci // HEAD_DIM)
    probs = [(b, g, slice(g * GROUP_W, (g + 1) * GROUP_W)) for b in range(batch) for g in range(HEADS // GROUP)]
    s_fs = [state[b, g] for b, g, _ in probs]
    s_bs = [s_f.astype(BF16) for s_f in s_fs]
    v_news = [(u_ref[b, :, ls].astype(F32) - _dot(w_ref[b, :, ls], s_b)).astype(BF16)
              for (b, _, ls), s_b in zip(probs, s_bs)]
    for (b, _, ls), s_b, v_new in zip(probs, s_bs, v_news):
        o_ref[b, :, ls] = _dot(jnp.concatenate([qe_ref[b, :, ls], qkw_ref[b, :, ls]], axis=1),
                               jnp.concatenate([s_b, _stack_heads(v_new, lane)], axis=0))
    kvs = [_dot_tn(kdec_ref[b, :, ls], v_new) for (b, _, ls), v_new in zip(probs, v_news)]
    for (b, g, ls), s_f, kv in zip(probs, s_fs, kvs):
        state[b, g] = s_f * egl_ref[b, 0, 0:1, ls] + jnp.where(same, kv, 0.0)

    @pl.when(c_idx == pl.num_programs(0) - 1)
    def _():
        for b in range(batch):
            for hd in range(HEADS):
                g, hh = divmod(hd, GROUP)
                sl = slice(hh * HEAD_DIM, (hh + 1) * HEAD_DIM)
                sfin_ref[b, hd] = state[b, g, sl, sl]


def _gdn_long(gqkv, small, conv_w, batch, seq, chunk):
    tm = TOKEN_TILE
    assert seq % tm == 0 and tm % chunk == 0 and GROUP * chunk == GROUP_W
    nt = seq // tm
    nc = seq // chunk
    r = np.arange(tm)
    tri = jnp.asarray((r[None, :] <= r[:, None]) & ((r[None, :] // chunk) == (r[:, None] // chunk)), BF16)
    head_of = np.arange(WIDTH) // HEAD_DIM
    hsum = jnp.asarray(head_of[:, None] == (np.arange(LANES)[None, :] - L_G), BF16)
    hexp_g = jnp.asarray((np.arange(LANES)[:, None] - L_G) == head_of[None, :], BF16)
    hexp_b = jnp.asarray((np.arange(LANES)[:, None] - L_BETA) == head_of[None, :], BF16)
    conv_w8 = jnp.concatenate([conv_w, jnp.zeros((SUBLANES - CONV_WIDTH, CONV_DIM), F32)], axis=0)
    dexp = np.zeros((3, LANES, 2 * LANES), np.float32)
    for t in range(3):
        for hd in range(HEADS):
            dexp[t, L_G + hd, 6 * hd + t] = 1.0
            dexp[t, L_G + hd, LANES + 6 * hd + 3 + t] = 1.0
    dexp = jnp.asarray(dexp, BF16)
    row = lambda w: pl.BlockSpec((tm, w), lambda b, i: (b * nt + i, 0))
    per_tile = tm // chunk
    tok = jax.ShapeDtypeStruct((batch * seq, WIDTH), BF16)
    u, w, qe, kdec, qkw, egl = pl.pallas_call(
        functools.partial(_gdn_local_kernel, chunk=chunk),
        grid=(batch, nt),
        in_specs=[row(CONV_DIM),
                  pl.BlockSpec((SUBLANES, CONV_DIM),
                               lambda b, i: (jnp.maximum((b * nt + i) * (tm // SUBLANES) - 1, 0), 0)),
                  row(LANES), _const_spec((SUBLANES, CONV_DIM)), _const_spec((tm, tm)),
                  _const_spec((WIDTH, LANES)), _const_spec((LANES, WIDTH)), _const_spec((LANES, WIDTH)),
                  _const_spec((3, LANES, 2 * LANES))],
        out_specs=(row(WIDTH), row(WIDTH), row(WIDTH), row(WIDTH), row(WIDTH),
                   pl.BlockSpec((per_tile, SUBLANES, WIDTH), lambda b, i: (b * nt + i, 0, 0))),
        out_shape=(tok, tok, tok, tok, tok, jax.ShapeDtypeStruct((batch * nc, SUBLANES, WIDTH), F32)),
        scratch_shapes=[pltpu.VMEM((tm + SUBLANES, CONV_DIM), F32)],
        compiler_params=pltpu.CompilerParams(dimension_semantics=("arbitrary", "arbitrary"),
                                             vmem_limit_bytes=VMEM_LIMIT),
        name="gdn_local",
    )(gqkv, gqkv, small, conv_w8, tri, hsum, hexp_g, hexp_b, dexp)

    blk = pl.BlockSpec((batch, chunk, WIDTH), lambda c: (0, c, 0))
    as3 = lambda a: a.reshape(batch, seq, WIDTH)
    o, s_fin = pl.pallas_call(
        functools.partial(_gdn_scan_kernel, chunk=chunk),
        grid=(nc,),
        in_specs=[blk, blk, blk, blk, blk,
                  pl.BlockSpec((batch, 1, SUBLANES, WIDTH), lambda c: (0, c, 0, 0))],
        out_specs=(blk, pl.BlockSpec((batch, HEADS, HEAD_DIM, HEAD_DIM), lambda c: (0, 0, 0, 0))),
        out_shape=(jax.ShapeDtypeStruct((batch, seq, WIDTH), F32),
                   jax.ShapeDtypeStruct((batch, HEADS, HEAD_DIM, HEAD_DIM), F32)),
        scratch_shapes=[pltpu.VMEM((batch, HEADS // GROUP, GROUP_W, GROUP_W), F32)],
        compiler_params=pltpu.CompilerParams(dimension_semantics=("arbitrary",), vmem_limit_bytes=VMEM_LIMIT),
        name="gdn_scan",
    )(as3(u), as3(w), as3(qe), as3(kdec), as3(qkw), egl.reshape(batch, nc, SUBLANES, WIDTH))
    return o.reshape(batch * seq, WIDTH), s_fin


def _post_kernel(x_ref, fox_ref, gdn_ref, sgg_ref, sga_ref, sgb_ref,
                 wpa_ref, wpb_ref, wout_ref, wup_ref, wdown_ref,
                 ng_ref, gpost_ref, gpre2_ref, gpost2_ref, hsum_ref, hexp_ref, y_ref):
    ya = _dot(fox_ref[...], wpa_ref[...])
    o = gdn_ref[...]
    ms = _dot_sel(o * o, hsum_ref[...], terms=2) * (1.0 / HEAD_DIM)
    o = o * _dot_sel(lax.rsqrt(ms + EPS), hexp_ref[...], terms=2) * ng_ref[...] * sgg_ref[...].astype(F32)
    yb = _dot(o.astype(BF16), wpb_ref[...])
    m = sga_ref[...].astype(F32) * ya + sgb_ref[...].astype(F32) * yb
    mix = _dot(m.astype(BF16), wout_ref[...])
    y1 = x_ref[...] + _rms(mix, gpost_ref[...])
    h2 = _rms(y1, gpre2_ref[...]).astype(BF16)
    acc = jnp.zeros(y1.shape, F32)
    for c in range(D_FF // FF_TILE):
        u = jnp.maximum(_dot(h2, wup_ref[:, c * FF_TILE:(c + 1) * FF_TILE]), 0.0)
        acc = acc + _dot((u * u).astype(BF16), wdown_ref[c * FF_TILE:(c + 1) * FF_TILE, :])
    y_ref[...] = y1 + _rms(acc, gpost2_ref[...])


def _post(x, fox, gdn_o, sgg, sga, sgb, w_pa, w_pb, w_out, w_up, w_down, ng, gpost, gpre2, gpost2):
    t_total = x.shape[0]
    tm = min(POST_TILE, t_total)
    assert t_total % tm == 0
    row = lambda w: pl.BlockSpec((tm, w), lambda i: (i, 0))
    head_of = np.arange(WIDTH) // HEAD_DIM
    hsum = jnp.asarray(head_of[:, None] == (np.arange(LANES)[None, :] - L_G), BF16)
    hexp = jnp.asarray((np.arange(LANES)[:, None] - L_G) == head_of[None, :], BF16)
    return pl.pallas_call(
        _post_kernel,
        grid=(t_total // tm,),
        in_specs=[row(D_MODEL), row(WIDTH), row(WIDTH), row(WIDTH), row(D_MODEL), row(D_MODEL),
                  _const_spec((WIDTH, D_MODEL)), _const_spec((WIDTH, D_MODEL)), _const_spec((D_MODEL, D_MODEL)),
                  _const_spec((D_MODEL, D_FF)), _const_spec((D_FF, D_MODEL)),
                  _const_spec((1, WIDTH)), _const_spec((1, D_MODEL)), _const_spec((1, D_MODEL)),
                  _const_spec((1, D_MODEL)), _const_spec((WIDTH, LANES)), _const_spec((LANES, WIDTH))],
        out_specs=row(D_MODEL),
        out_shape=jax.ShapeDtypeStruct((t_total, D_MODEL), F32),
        compiler_params=pltpu.CompilerParams(dimension_semantics=("arbitrary",), vmem_limit_bytes=VMEM_LIMIT),
        name="post",
    )(x, fox, gdn_o, sgg, sga, sgb, w_pa, w_pb, w_out, w_up, w_down, ng, gpost, gpre2, gpost2, hsum, hexp)


def _rearrange_w_in(w_in):
    o_ff = 3 * WIDTH
    o_gqkv = o_ff + HEADS
    o_ga = o_gqkv + CONV_DIM
    o_gb = o_ga + HEADS
    o_gg = o_gb + HEADS
    o_gate_a = o_gg + WIDTH
    o_gate_b = o_gate_a + D_MODEL
    cols = [w_in[:, 0:o_ff], w_in[:, o_gqkv:o_ga], w_in[:, o_gg:o_gate_a], w_in[:, o_gate_a:o_gate_b],
            w_in[:, o_gate_b:o_gate_b + D_MODEL], w_in[:, o_ff:o_gqkv], w_in[:, o_ga:o_gb], w_in[:, o_gb:o_gg],
            jnp.zeros((D_MODEL, LANES - 3 * HEADS), w_in.dtype)]
    return jnp.concatenate(cols, axis=1).astype(BF16)


def _lane_row(*pieces):
    v = jnp.concatenate([p.astype(F32) for p in pieces])
    return jnp.concatenate([v, jnp.zeros((LANES - v.shape[0],), F32)])[None, :]


def _pad_hist(conv_cache):
    b = conv_cache.shape[0]
    return jnp.concatenate([jnp.zeros((b, SUBLANES - (CONV_WIDTH - 1), CONV_DIM), F32), conv_cache.astype(F32)], axis=1)


def kernel(x_prompt, x_sample, cache_fox_k, cache_fox_v, cache_fox_logf, state_gdn, state_gdn_conv,
           w_in, fox_forget_bias, gdn_conv_w, gdn_a_log, gdn_dt_bias, gdn_norm_g,
           w_proj_fox, w_proj_gdn, w_out, norm_mix_pre, norm_mix_post, norm_mlp_pre, norm_mlp_post,
           w_up, w_down):
    depth = w_in.shape[0]
    bp, sp, _ = x_prompt.shape
    bs, ss, _ = x_sample.shape
    past = cache_fox_k.shape[2]
    y_p = x_prompt.reshape(bp * sp, D_MODEL)
    y_s = x_sample.reshape(bs * ss, D_MODEL)
    st_p, st_s = [], []
    for l in range(depth):
        w_r = _rearrange_w_in(w_in[l])
        padd = _lane_row(fox_forget_bias[l], gdn_dt_bias[l])
        palog = _lane_row(jnp.zeros((HEADS,), F32), gdn_a_log[l])
        gain = norm_mix_pre[l][None, :]
        post_w = (w_proj_fox[l].astype(BF16), w_proj_gdn[l].astype(BF16), w_out[l].astype(BF16),
                  w_up[l].astype(BF16), w_down[l].astype(BF16),
                  jnp.tile(gdn_norm_g[l], HEADS)[None, :], norm_mix_post[l][None, :],
                  norm_mlp_pre[l][None, :], norm_mlp_post[l][None, :])

        q, k, vt, fk, fv, gqkv, sgg, sga, sgb, small, cum = _in_proj(y_p, gain, w_r, padd, palog, sp)
        fox = _fox_prompt(q, k, vt, cum, bp, sp)
        gdn_o, s_fin = _gdn_long(gqkv, small, gdn_conv_w[l], bp, sp, GDN_CHUNK)
        y_p_new = _post(y_p, fox, gdn_o, sgg, sga, sgb, *post_w)
        st_p.append((fk.reshape(bp, sp, HEADS, HEAD_DIM), fv.reshape(bp, sp, HEADS, HEAD_DIM),
                     small[:, L_LOGF:L_LOGF + HEADS].reshape(bp, sp, HEADS), s_fin,
                     gqkv.reshape(bp, sp, CONV_DIM)[:, sp - (CONV_WIDTH - 1):, :]))
        y_p = y_p_new

        q, k, vt, fk, fv, gqkv, sgg, sga, sgb, small, cum = _in_proj(y_s, gain, w_r, padd, palog, ss)
        fox = _fox_sample(q, fk, fv, small,
                          cache_fox_k[l].reshape(bs, past, WIDTH), cache_fox_v[l].reshape(bs, past, WIDTH),
                          cache_fox_logf[l], bs, ss, past)
        gdn_o, s_new = _gdn(gqkv, small, _pad_hist(state_gdn_conv[l]), state_gdn[l].astype(F32),
                            gdn_conv_w[l], bs, ss, ss)
        y_s_new = _post(y_s, fox, gdn_o, sgg, sga, sgb, *post_w)
        conv_ext = jnp.concatenate([state_gdn_conv[l].astype(F32), gqkv.reshape(bs, ss, CONV_DIM)], axis=1)
        st_s.append((fk.reshape(bs, ss, HEADS, HEAD_DIM), fv.reshape(bs, ss, HEADS, HEAD_DIM),
                     small[:, L_LOGF:L_LOGF + HEADS].reshape(bs, ss, HEADS), s_new,
                     conv_ext[:, conv_ext.shape[1] - (CONV_WIDTH - 1):, :]))
        y_s = y_s_new

    fk_p, fv_p, lf_p, sg_p, cv_p = [jnp.stack(a) for a in zip(*st_p)]
    fk_s, fv_s, lf_s, sg_s, cv_s = [jnp.stack(a) for a in zip(*st_s)]
    return (y_p.reshape(bp, sp, D_MODEL), y_s.reshape(bs, ss, D_MODEL),
            fk_p, fv_p, lf_p, sg_p, cv_p, fk_s, fv_s, lf_s, sg_s, cv_s)
```

```python
import functools

import jax
import jax.numpy as jnp
import numpy as np
from jax import lax
from jax.experimental import pallas as pl
from jax.experimental.pallas import tpu as pltpu

F32 = jnp.float32
BF16 = jnp.bfloat16

D_MODEL = 1024
HEADS = 8
HEAD_DIM = 64
WIDTH = HEADS * HEAD_DIM
CONV_DIM = 3 * WIDTH
CONV_WIDTH = 4
D_FF = 4 * D_MODEL
EPS = 1e-6
LOG2E = 1.4426950408889634

LANES = 128
SUBLANES = 8
TOKEN_TILE = 256
POST_TILE = 512
ATTN_TILE = 256
ATTN_HEADS = 4
KEY_TILE = 512
FF_TILE = 1024
GDN_CHUNK = 64
VMEM_LIMIT = 56 * 1024 * 1024

C_Q, C_K, C_V, C_GQKV, C_GG, C_GA, C_GB, C_SMALL = 0, 512, 1024, 1536, 3072, 3584, 4608, 5632
IN_COLS = C_SMALL + LANES
L_LOGF, L_G, L_BETA = 0, 8, 16


def _split3(x):
    hi = x.astype(BF16)
    r = x - hi.astype(F32)
    mid = r.astype(BF16)
    lo = (r - mid.astype(F32)).astype(BF16)
    return hi, mid, lo


def _dot(a, b):
    return jnp.dot(a, b, preferred_element_type=F32)


def _dot_nt(a, b):
    return lax.dot_general(a, b, (((1,), (1,)), ((), ())), preferred_element_type=F32)


def _dot_tn(a, b):
    return lax.dot_general(a, b, (((0,), (0,)), ((), ())), preferred_element_type=F32)


def _sel_dot(sel, x, terms=3):
    parts = _split3(x)[:terms]
    n = x.shape[1]
    if n % LANES:
        return sum(_dot(sel, p) for p in parts)
    wide = _dot(sel, jnp.concatenate(parts, axis=1))
    return sum(wide[:, t * n:(t + 1) * n] for t in range(terms))


def _dot_sel(x, sel, terms=3):
    parts = _split3(x)[:terms]
    if terms == 1:
        return _dot(parts[0], sel)
    out = _dot(jnp.concatenate(parts[0:2], axis=1), jnp.concatenate([sel, sel], axis=0))
    for p in parts[2:]:
        out = out + _dot(p, sel)
    return out


def _sigmoid(x):
    return 1.0 / (1.0 + jnp.exp(-x))


def _rms(x, gain):
    ms = jnp.mean(x * x, axis=-1, keepdims=True)
    return x * lax.rsqrt(ms + EPS) * gain


def _const_spec(shape):
    nd = len(shape)
    return pl.BlockSpec(shape, lambda *_: (0,) * nd, pipeline_mode=pl.Buffered(1))


def _inproj_kernel(x_ref, gain_ref, w_ref, padd_ref, palog_ref, tri_ref,
                   q_ref, k_ref, vt_ref, fk_ref, fv_ref, gqkv_ref, sgg_ref, sga_ref, sgb_ref,
                   small_ref, cum_ref, carry_ref, *, tiles_per_seg):
    i = pl.program_id(0)
    h = _rms(x_ref[...], gain_ref[...]).astype(BF16)

    def mm(c0, width):
        return _dot(h, w_ref[:, c0:c0 + width])

    zq = mm(C_Q, WIDTH)
    q_ref[...] = (zq * (LOG2E * HEAD_DIM ** -0.5)).astype(BF16)
    zk = mm(C_K, WIDTH)
    k_ref[...] = zk.astype(BF16)
    zv = mm(C_V, WIDTH)
    for hd in range(HEADS):
        fk_ref[:, hd, :] = zk[:, hd * HEAD_DIM:(hd + 1) * HEAD_DIM]
        fv_ref[:, hd, :] = zv[:, hd * HEAD_DIM:(hd + 1) * HEAD_DIM]
    vt_ref[0] = zv.T.astype(BF16)
    for c in range(3):
        gqkv_ref[:, c * WIDTH:(c + 1) * WIDTH] = mm(C_GQKV + c * WIDTH, WIDTH)
    zg = mm(C_GG, WIDTH)
    sgg_ref[...] = (zg * _sigmoid(zg)).astype(BF16)
    for c in range(2):
        sga_ref[:, c * WIDTH:(c + 1) * WIDTH] = _sigmoid(mm(C_GA + c * WIDTH, WIDTH)).astype(BF16)
        sgb_ref[:, c * WIDTH:(c + 1) * WIDTH] = _sigmoid(mm(C_GB + c * WIDTH, WIDTH)).astype(BF16)

    zs = mm(C_SMALL, LANES)
    t = zs + padd_ref[...]
    l1p = jnp.log1p(jnp.exp(-jnp.abs(t)))
    logf = jnp.minimum(t, 0.0) - l1p
    g = -jnp.exp(palog_ref[...]) * (jnp.maximum(t, 0.0) + l1p)
    beta = _sigmoid(zs)
    lane = lax.broadcasted_iota(jnp.int32, zs.shape, 1)
    small = jnp.where(lane < L_G, logf, jnp.where(lane < L_BETA, g, jnp.where(lane < L_BETA + HEADS, beta, 0.0)))
    small_ref[...] = small

    cum = _sel_dot(tri_ref[...], small)
    if tiles_per_seg > 1:
        @pl.when(i % tiles_per_seg == 0)
        def _():
            carry_ref[...] = jnp.zeros_like(carry_ref)
        cum = cum + carry_ref[0:1, :]
        carry_ref[...] = jnp.broadcast_to(cum[-1:, :], carry_ref.shape)
    cum_ref[...] = cum


def _in_proj(x, gain, w_r, padd, palog, seg_len):
    t_total = x.shape[0]
    tm = TOKEN_TILE
    assert t_total % tm == 0
    nt = t_total // tm
    if seg_len >= tm:
        assert seg_len % tm == 0
        tiles_per_seg = seg_len // tm
        r = np.arange(tm)
        tri = (r[None, :] <= r[:, None])
    else:
        assert tm % seg_len == 0
        tiles_per_seg = 1
        r = np.arange(tm)
        tri = (r[None, :] <= r[:, None]) & ((r[None, :] // seg_len) == (r[:, None] // seg_len))
    tri = jnp.asarray(tri, BF16)

    row = lambda w: pl.BlockSpec((tm, w), lambda i: (i, 0))
    out_shape = (
        jax.ShapeDtypeStruct((t_total, WIDTH), BF16),
        jax.ShapeDtypeStruct((t_total, WIDTH), BF16),
        jax.ShapeDtypeStruct((nt, WIDTH, tm), BF16),
        jax.ShapeDtypeStruct((t_total, HEADS, HEAD_DIM), F32),
        jax.ShapeDtypeStruct((t_total, HEADS, HEAD_DIM), F32),
        jax.ShapeDtypeStruct((t_total, CONV_DIM), F32),
        jax.ShapeDtypeStruct((t_total, WIDTH), BF16),
        jax.ShapeDtypeStruct((t_total, D_MODEL), BF16),
        jax.ShapeDtypeStruct((t_total, D_MODEL), BF16),
        jax.ShapeDtypeStruct((t_total, LANES), F32),
        jax.ShapeDtypeStruct((t_total, LANES), F32),
    )
    per_head = pl.BlockSpec((tm, HEADS, HEAD_DIM), lambda i: (i, 0, 0))
    out_specs = (row(WIDTH), row(WIDTH), pl.BlockSpec((1, WIDTH, tm), lambda i: (i, 0, 0)),
                 per_head, per_head, row(CONV_DIM), row(WIDTH), row(D_MODEL), row(D_MODEL),
                 row(LANES), row(LANES))
    return pl.pallas_call(
        functools.partial(_inproj_kernel, tiles_per_seg=tiles_per_seg),
        grid=(nt,),
        in_specs=[row(D_MODEL), _const_spec((1, D_MODEL)), _const_spec((D_MODEL, IN_COLS)),
                  _const_spec((1, LANES)), _const_spec((1, LANES)), _const_spec((tm, tm))],
        out_specs=out_specs,
        out_shape=out_shape,
        scratch_shapes=[pltpu.VMEM((SUBLANES, LANES), F32)],
        compiler_params=pltpu.CompilerParams(dimension_semantics=("arbitrary",), vmem_limit_bytes=VMEM_LIMIT),
        name="in_proj",
    )(x, gain, w_r, padd, palog, tri)


def _fox_prompt_kernel(q_ref, k_ref, vt_ref, cum_ref, o_ref, bias_ref, s0_ref, s1_ref, p0_ref, p1_ref,
                       *, seq, tq, tk, nh):
    grp = pl.program_id(1)
    qi = pl.program_id(2)
    rows = 512
    vt_tile = vt_ref.shape[2]
    sub = tk // vt_tile
    heads = range(nh)
    s_slots = (s0_ref, s1_ref)
    p_slots = (p0_ref, p1_ref)

    @pl.when(qi == 0)
    def _prep():
        def fill(r, carry):
            c = cum_ref[pl.ds(r * rows, rows), :]
            lane = lax.broadcasted_iota(jnp.int32, c.shape, 1)
            slab = jnp.zeros(c.shape, F32)
            for hh in heads:
                col = jnp.sum(jnp.where(lane == nh * grp + hh, c, 0.0), axis=-1, keepdims=True) * (-LOG2E)
                for t, part in enumerate(_split3(col)):
                    slab = jnp.where(lane == 3 * hh + t, part.astype(F32), slab)
            bias_ref[pl.ds(r * rows, rows), :] = slab.astype(BF16)
            return carry
        lax.fori_loop(0, seq // rows, fill, 0)

    lane_q = lax.broadcasted_iota(jnp.int32, (tq, LANES), 1)
    qh = []
    for hh in heads:
        q2 = q_ref[:, (hh // 2) * LANES:(hh // 2 + 1) * LANES]
        half = hh % 2
        q_m = jnp.where((lane_q >= HEAD_DIM * half) & (lane_q < HEAD_DIM * (half + 1)), q2, jnp.zeros_like(q2))
        ones = jnp.where((lane_q >= 3 * hh) & (lane_q < 3 * hh + 3), 1.0, 0.0)
        qh.append(jnp.concatenate([q_m.astype(F32), ones], axis=1).T.astype(BF16))

    def scores_to(slot, j):
        bias = bias_ref[pl.ds(j * tk, tk), :]
        kjs = [jnp.concatenate([k_ref[pl.ds(j * tk, tk), sl * LANES:(sl + 1) * LANES], bias], axis=1)
               for sl in range(nh // 2)]
        s_ts = [_dot(kjs[hh // 2], qh[hh]) for hh in heads]
        for hh in heads:
            s_slots[slot][hh] = s_ts[hh]
        return tuple(jnp.max(s_t, axis=0, keepdims=True) for s_t in s_ts)

    ones_rows = jnp.ones((2 * SUBLANES, vt_tile), BF16)

    def add_values(slot, j, accs, alphas):
        pvs = [sum(_dot(jnp.concatenate([vt_ref[j * sub + t, HEAD_DIM * hh:HEAD_DIM * (hh + 1), :], ones_rows], axis=0),
                        p_slots[slot][hh, t * vt_tile:(t + 1) * vt_tile, :]) for t in range(sub)) for hh in heads]
        return tuple(alphas[hh] * accs[hh] + pvs[hh] for hh in heads)

    def rescale(ms, bms):
        m_new = tuple(jnp.maximum(ms[hh], bms[hh]) for hh in heads)
        return m_new, tuple(jnp.exp2(ms[hh] - m_new[hh]) for hh in heads)

    def probabilities(slot, hh, m_new, visible_from):
        s_t = s_slots[slot][hh]
        if visible_from is not None:
            s_t = jnp.where(visible_from, s_t, -1e30)
        p_slots[slot][hh] = jnp.exp2(s_t - m_new).astype(BF16)

    def trip(j, rd, carry):
        ms, accs, a_prev, bms = carry
        wr = 1 - rd
        accs = add_values(wr, jnp.maximum(j - 1, 0), accs, a_prev)
        m_new, alphas = rescale(ms, bms)
        for hh in heads:
            probabilities(rd, hh, m_new[hh], None)
        bm_next = scores_to(wr, j + 1)
        return m_new, accs, alphas, bm_next

    n_full = (qi * tq) // tk

    def last(rd, carry):
        ms, accs, a_prev, _ = carry
        accs = add_values(1 - rd, jnp.maximum(n_full - 1, 0), accs, a_prev)
        kr = lax.broadcasted_iota(jnp.int32, (tk, tq), 0)
        qc = lax.broadcasted_iota(jnp.int32, (tk, tq), 1)
        visible = kr <= qc + (qi * tq - n_full * tk)
        bms = tuple(jnp.max(jnp.where(visible, s_slots[rd][hh], -1e30), axis=0, keepdims=True) for hh in heads)
        m_new, alphas = rescale(ms, bms)
        for hh in heads:
            probabilities(rd, hh, m_new[hh], visible)
        accs = add_values(rd, n_full, accs, alphas)
        return jnp.concatenate([accs[hh][0:HEAD_DIM] / accs[hh][HEAD_DIM:HEAD_DIM + 1] for hh in heads], axis=0)

    each = lambda f: tuple(f() for _ in heads)
    p1_ref[...] = jnp.zeros(p1_ref.shape, BF16)
    init = (each(lambda: jnp.full((1, tq), -1e30, F32)), each(lambda: jnp.zeros((HEAD_DIM + 2 * SUBLANES, tq), F32)),
            each(lambda: jnp.ones((1, tq), F32)), scores_to(0, 0))
    carry = lax.fori_loop(
        0, n_full, lambda j, c: lax.cond(j % 2 == 0, lambda c: trip(j, 0, c), lambda c: trip(j, 1, c), c), init)
    o_t = lax.cond(n_full % 2 == 0, lambda c: last(0, c), lambda c: last(1, c), carry)
    o_ref[...] = o_t.T.astype(BF16)


def _fox_prompt(q, k, vt, cum, batch, seq):
    tq, tk, nh = ATTN_TILE, KEY_TILE, ATTN_HEADS
    assert seq % tk == 0 and tk % tq == 0 and tk % TOKEN_TILE == 0 and HEADS % nh == 0 and nh % 2 == 0
    nq = seq // tq
    width = nh * HEAD_DIM
    return pl.pallas_call(
        functools.partial(_fox_prompt_kernel, seq=seq, tq=tq, tk=tk, nh=nh),
        grid=(batch, HEADS // nh, nq),
        in_specs=[pl.BlockSpec((tq, width), lambda b, g, i: (b * nq + i, g)),
                  pl.BlockSpec((seq, width), lambda b, g, i: (b, g)),
                  pl.BlockSpec((seq // TOKEN_TILE, width, TOKEN_TILE), lambda b, g, i: (b, g, 0)),
                  pl.BlockSpec((seq, LANES), lambda b, g, i: (b, 0))],
        out_specs=pl.BlockSpec((tq, width), lambda b, g, i: (b * nq + i, g)),
        out_shape=jax.ShapeDtypeStruct((batch * seq, WIDTH), BF16),
        scratch_shapes=[pltpu.VMEM((seq, LANES), BF16),
                        pltpu.VMEM((nh, tk, tq), F32), pltpu.VMEM((nh, tk, tq), F32),
                        pltpu.VMEM((nh, tk, tq), BF16), pltpu.VMEM((nh, tk, tq), BF16)],
        compiler_params=pltpu.CompilerParams(dimension_semantics=("arbitrary",) * 3, vmem_limit_bytes=VMEM_LIMIT),
        name="fox_prompt",
    )(q, k, vt, cum)


def _fox_sample_kernel(q_ref, kn_ref, vn_ref, small_ref, kc_ref, vc_ref, lfc_ref, tri_ref, trin_ref, o_ref,
                       *, past, steps):
    blk = tri_ref.shape[0]
    carry = jnp.zeros((1, HEADS), F32)
    cums = []
    for r in range(past // blk):
        c = _sel_dot(tri_ref[...], lfc_ref[0, r * blk:(r + 1) * blk, :]) + carry
        cums.append(c)
        carry = c[-1:, :]
    cum_c = jnp.concatenate(cums, axis=0)
    cum_n = _sel_dot(trin_ref[...], small_ref[:, L_LOGF:L_LOGF + HEADS]) + carry

    q = q_ref[...]
    kc = kc_ref[0].astype(BF16)
    vc = vc_ref[0].astype(BF16)
    kn = kn_ref[...].astype(BF16)
    vn = vn_ref[...].astype(BF16)
    lane_c = lax.broadcasted_iota(jnp.int32, (past, LANES), 1)
    lane_n = lax.broadcasted_iota(jnp.int32, (steps, LANES), 1)
    kr = lax.broadcasted_iota(jnp.int32, (steps, steps), 0)
    qc = lax.broadcasted_iota(jnp.int32, (steps, steps), 1)
    ones_c = jnp.ones((past, LANES), BF16)
    ones_n = jnp.ones((steps, LANES), BF16)
    heads = range(HEADS)
    slab = [slice((hd // 2) * LANES, (hd // 2 + 1) * LANES) for hd in heads]
    in_c = [(lane_c >= HEAD_DIM * (hd % 2)) & (lane_c < HEAD_DIM * (hd % 2 + 1)) for hd in heads]
    in_n = [(lane_n >= HEAD_DIM * (hd % 2)) & (lane_n < HEAD_DIM * (hd % 2 + 1)) for hd in heads]
    qms = [jnp.where(in_n[hd], q[:, slab[hd]], jnp.zeros((steps, LANES), BF16)) for hd in heads]
    s1s = [_dot_nt(kc[:, slab[hd]], qms[hd]) - LOG2E * cum_c[:, hd:hd + 1] for hd in heads]
    s2s = [jnp.where(kr <= qc, _dot_nt(kn[:, slab[hd]], qms[hd]) - LOG2E * cum_n[:, hd:hd + 1], -1e30)
           for hd in heads]
    ms = [jnp.maximum(jnp.max(s1, axis=0, keepdims=True), jnp.max(s2, axis=0, keepdims=True))
          for s1, s2 in zip(s1s, s2s)]
    p1s = [jnp.exp2(s1 - m).astype(BF16) for s1, m in zip(s1s, ms)]
    p2s = [jnp.exp2(s2 - m).astype(BF16) for s2, m in zip(s2s, ms)]
    nums = [_dot_tn(p1s[hd], jnp.where(in_c[hd], vc[:, slab[hd]], jnp.zeros((past, LANES), BF16)))
            + _dot_tn(p2s[hd], jnp.where(in_n[hd], vn[:, slab[hd]], jnp.zeros((steps, LANES), BF16))) for hd in heads]
    dens = [_dot_tn(p1s[hd], ones_c) + _dot_tn(p2s[hd], ones_n) for hd in heads]
    outs = [n / d for n, d in zip(nums, dens)]
    o_ref[...] = jnp.concatenate([outs[2 * pr] + outs[2 * pr + 1] for pr in range(HEADS // 2)], axis=1).astype(BF16)


def _fox_sample(q, kn, vn, small, kc, vc, lfc, batch, steps, past):
    blk = 256
    assert past % blk == 0
    r = np.arange(blk)
    tri = jnp.asarray(r[None, :] <= r[:, None], BF16)
    rn = np.arange(steps)
    trin = jnp.asarray(rn[None, :] <= rn[:, None], BF16)
    row = lambda w: pl.BlockSpec((steps, w), lambda b: (b, 0))
    return pl.pallas_call(
        functools.partial(_fox_sample_kernel, past=past, steps=steps),
        grid=(batch,),
        in_specs=[row(WIDTH), row(WIDTH), row(WIDTH), row(LANES),
                  pl.BlockSpec((1, past, WIDTH), lambda b: (b, 0, 0)),
                  pl.BlockSpec((1, past, WIDTH), lambda b: (b, 0, 0)),
                  pl.BlockSpec((1, past, HEADS), lambda b: (b, 0, 0)),
                  _const_spec((blk, blk)), _const_spec((steps, steps))],
        out_specs=row(WIDTH),
        out_shape=jax.ShapeDtypeStruct((batch * steps, WIDTH), BF16),
        compiler_params=pltpu.CompilerParams(dimension_semantics=("arbitrary",), vmem_limit_bytes=VMEM_LIMIT),
        name="fox_sample",
    )(q, kn, vn, small, kc, vc, lfc, tri, trin)


def _gdn_kernel(gq_ref, small_ref, hist_ref, s0_ref, convw_ref, tri_ref, hsum_ref, hexp_g_ref, hexp_b_ref,
                o_ref, sfin_ref, xbuf, state, *, chunk):
    c_idx = pl.program_id(1)
    n_chunks = pl.num_programs(1)
    hist_rows = SUBLANES

    @pl.when(c_idx == 0)
    def _():
        xbuf[0:hist_rows, :] = hist_ref[0]
        state[...] = s0_ref[0]

    xbuf[hist_rows:hist_rows + chunk, :] = gq_ref[...]
    w = convw_ref[...]
    base = hist_rows - (CONV_WIDTH - 1)
    y = xbuf[base:base + chunk, :] * w[0:1, :]
    for i in range(1, CONV_WIDTH):
        y = y + xbuf[base + i:base + i + chunk, :] * w[i:i + 1, :]
    xbuf[0:hist_rows, :] = xbuf[chunk:chunk + hist_rows, :]
    y = y * _sigmoid(y)

    hsum = hsum_ref[...]
    hexp_g = hexp_g_ref[...]
    hexp_b = hexp_b_ref[...]

    def l2n(x):
        inv = lax.rsqrt(_dot_sel(x * x, hsum, terms=2) + EPS)
        return x * _dot_sel(inv, hexp_g, terms=2)

    q = l2n(y[:, 0:WIDTH]) * (HEAD_DIM ** -0.5)
    k = l2n(y[:, WIDTH:2 * WIDTH])
    v = y[:, 2 * WIDTH:3 * WIDTH]

    small = small_ref[...]
    gc = _sel_dot(tri_ref[...], small)
    gc_x = _dot_sel(gc, hexp_g)
    beta_x = _dot_sel(small, hexp_b, terms=2)
    egc_x = jnp.exp(gc_x)
    glast_x = gc_x[chunk - 1:chunk, :]
    eglast_x = jnp.exp(glast_x)
    kb = k * beta_x
    vb = v * beta_x
    kbe = kb * egc_x
    qe = q * egc_x
    kdec = k * jnp.exp(glast_x - gc_x)

    pad_rows = LANES - chunk
    gc_sq = jnp.concatenate([gc, jnp.zeros((pad_rows, LANES), F32)], axis=0) if pad_rows else gc
    gc_t = gc_sq.T

    ri = lax.broadcasted_iota(jnp.int32, (chunk, chunk), 0)
    ci = lax.broadcasted_iota(jnp.int32, (chunk, chunk), 1)
    eye = jnp.where(ri == ci, 1.0, 0.0).astype(F32)
    n_double = int(np.log2(chunk)) - 1
    assert 2 ** (n_double + 1) == chunk

    heads = range(HEADS)
    sls = [slice(hd * HEAD_DIM, (hd + 1) * HEAD_DIM) for hd in heads]
    decs = [jnp.where(ri >= ci, jnp.exp(gc[:, L_G + hd:L_G + hd + 1] - gc_t[L_G + hd:L_G + hd + 1, 0:chunk]), 0.0)
            for hd in heads]
    k_hs = [k[:, sl].astype(BF16) for sl in sls]
    pws = [-(_dot_nt(kb[:, sl].astype(BF16), k_h) * jnp.where(ri > ci, dec, 0.0))
           for sl, k_h, dec in zip(sls, k_hs, decs)]
    qks = [(_dot_nt(q[:, sl].astype(BF16), k_h) * dec).astype(BF16) for sl, k_h, dec in zip(sls, k_hs, decs)]
    tinvs = [eye + pw for pw in pws]
    for _ in range(n_double):
        pws = [pw.astype(BF16) for pw in pws]
        pws = [_dot(pw, pw) for pw in pws]
        tinvs = [tinv + _dot(tinv.astype(BF16), pw.astype(BF16)) for tinv, pw in zip(tinvs, pws)]
    uws = [_dot(tinv.astype(BF16), jnp.concatenate([vb[:, sl], kbe[:, sl]], axis=1).astype(BF16))
           for tinv, sl in zip(tinvs, sls)]
    s_fs = [state[hd] for hd in heads]
    s_bs = [s_f.astype(BF16) for s_f in s_fs]
    v_news = [(uw[:, 0:HEAD_DIM] - _dot(uw[:, HEAD_DIM:2 * HEAD_DIM].astype(BF16), s_b)).astype(BF16)
              for uw, s_b in zip(uws, s_bs)]
    outs = [_dot(qe[:, sl].astype(BF16), s_b) + _dot(qk, v_new)
            for sl, s_b, qk, v_new in zip(sls, s_bs, qks, v_news)]
    for hd in heads:
        state[hd] = s_fs[hd] * eglast_x[:, sls[hd]] + _dot_tn(kdec[:, sls[hd]].astype(BF16), v_news[hd])
    o_ref[...] = jnp.concatenate(outs, axis=1)

    @pl.when(c_idx == n_chunks - 1)
    def _():
        sfin_ref[0] = state[...]


def _gdn(gqkv, small, hist, s0, conv_w, batch, seq, chunk):
    assert seq % chunk == 0 and chunk % SUBLANES == 0 and chunk <= LANES
    nc = seq // chunk
    r = np.arange(chunk)
    tri = jnp.asarray(r[None, :] <= r[:, None], BF16)
    head_of = np.arange(WIDTH) // HEAD_DIM
    hsum = jnp.asarray(head_of[:, None] == (np.arange(LANES)[None, :] - L_G), BF16)
    hexp_g = jnp.asarray((np.arange(LANES)[:, None] - L_G) == head_of[None, :], BF16)
    hexp_b = jnp.asarray((np.arange(LANES)[:, None] - L_BETA) == head_of[None, :], BF16)
    conv_w8 = jnp.concatenate([conv_w, jnp.zeros((SUBLANES - CONV_WIDTH, CONV_DIM), F32)], axis=0)
    return pl.pallas_call(
        functools.partial(_gdn_kernel, chunk=chunk),
        grid=(batch, nc),
        in_specs=[pl.BlockSpec((chunk, CONV_DIM), lambda b, c: (b * nc + c, 0)),
                  pl.BlockSpec((chunk, LANES), lambda b, c: (b * nc + c, 0)),
                  pl.BlockSpec((1, SUBLANES, CONV_DIM), lambda b, c: (b, 0, 0)),
                  pl.BlockSpec((1, HEADS, HEAD_DIM, HEAD_DIM), lambda b, c: (b, 0, 0, 0)),
                  _const_spec((SUBLANES, CONV_DIM)), _const_spec((chunk, chunk)),
                  _const_spec((WIDTH, LANES)), _const_spec((LANES, WIDTH)), _const_spec((LANES, WIDTH))],
        out_specs=(pl.BlockSpec((chunk, WIDTH), lambda b, c: (b * nc + c, 0)),
                   pl.BlockSpec((1, HEADS, HEAD_DIM, HEAD_DIM), lambda b, c: (b, 0, 0, 0))),
        out_shape=(jax.ShapeDtypeStruct((batch * seq, WIDTH), F32),
                   jax.ShapeDtypeStruct((batch, HEADS, HEAD_DIM, HEAD_DIM), F32)),
        scratch_shapes=[pltpu.VMEM((chunk + SUBLANES, CONV_DIM), F32),
                        pltpu.VMEM((HEADS, HEAD_DIM, HEAD_DIM), F32)],
        compiler_params=pltpu.CompilerParams(dimension_semantics=("arbitrary", "arbitrary"),
                                             vmem_limit_bytes=VMEM_LIMIT),
        name="gdn",
    )(gqkv, small, hist, s0, conv_w8, tri, hsum, hexp_g, hexp_b)


GROUP = 4
GROUP_W = GROUP * HEAD_DIM


def _stack_heads(x, lane):
    return jnp.concatenate([jnp.where(lane == hh, x, jnp.zeros_like(x)) for hh in range(GROUP)], axis=0)


def _fold_heads(x, rows):
    out = x[0:rows]
    for hh in range(1, GROUP):
        out = out + x[hh * rows:(hh + 1) * rows]
    return out


def _gdn_local_kernel(gq_ref, prev_ref, small_ref, convw_ref, tri_ref, hsum_ref, hexp_g_ref, hexp_b_ref, dexp_ref,
                      u_ref, w_ref, qe_ref, kdec_ref, qkw_ref, egl_ref, xbuf, *, chunk):
    i = pl.program_id(1)
    tm = gq_ref.shape[0]
    hist_rows = SUBLANES
    prev = prev_ref[...]
    xbuf[0:hist_rows, :] = jnp.where(i == 0, jnp.zeros_like(prev), prev)
    xbuf[hist_rows:hist_rows + tm, :] = gq_ref[...]
    w = convw_ref[...]
    base = hist_rows - (CONV_WIDTH - 1)
    y = xbuf[base:base + tm, :] * w[0:1, :]
    for t in range(1, CONV_WIDTH):
        y = y + xbuf[base + t:base + t + tm, :] * w[t:t + 1, :]
    y = y * _sigmoid(y)

    hsum = hsum_ref[...]
    hexp_g = hexp_g_ref[...]
    hexp_b = hexp_b_ref[...]

    def l2n(x):
        inv = lax.rsqrt(_dot_sel(x * x, hsum, terms=2) + EPS)
        return x * _dot_sel(inv, hexp_g, terms=2)

    q = l2n(y[:, 0:WIDTH]) * (HEAD_DIM ** -0.5)
    k = l2n(y[:, WIDTH:2 * WIDTH])
    v = y[:, 2 * WIDTH:3 * WIDTH]

    small = small_ref[...]
    gc = _sel_dot(tri_ref[...], small)
    gc_x = _dot_sel(gc, hexp_g)
    beta_x = _dot_sel(small, hexp_b, terms=2)
    n_chunks = tm // chunk
    lasts = [gc_x[(c + 1) * chunk - 1:(c + 1) * chunk, :] for c in range(n_chunks)]
    glast_x = jnp.concatenate([jnp.broadcast_to(r, (chunk, WIDTH)) for r in lasts], axis=0)
    egc_x = jnp.exp(gc_x)
    kb = k * beta_x
    vb = (v * beta_x).astype(BF16)
    kbe = (kb * egc_x).astype(BF16)
    kb = kb.astype(BF16)
    qe_ref[...] = (q * egc_x).astype(BF16)
    kdec_ref[...] = (k * jnp.exp(glast_x - gc_x)).astype(BF16)
    for c in range(n_chunks):
        egl_ref[c] = jnp.broadcast_to(jnp.exp(lasts[c]), (SUBLANES, WIDTH))
    q = q.astype(BF16)
    k = k.astype(BF16)

    rows = GROUP * chunk
    lane = lax.broadcasted_iota(jnp.int32, (chunk, GROUP_W), 1)
    head_of_lane = lane // HEAD_DIM
    ri = lax.broadcasted_iota(jnp.int32, (rows, rows), 0)
    ci = lax.broadcasted_iota(jnp.int32, (rows, rows), 1)
    same = (ri // chunk) == (ci // chunk)
    incl = same & (ri >= ci)
    strict = same & (ri > ci)
    eye = jnp.where(ri == ci, 1.0, 0.0).astype(F32)
    n_double = int(np.log2(chunk)) - 1

    placed = None
    for t, part in enumerate(_split3(gc)):
        term = _dot(part, dexp_ref[t])
        placed = term if placed is None else placed + term
    lane_d = lax.broadcasted_iota(jnp.int32, (chunk, LANES), 1)
    head_d = jnp.where(lane_d < 6 * HEADS, lane_d // 6, -1)
    used = lane_d[0:1, :] < 6 * HEADS
    hi_half = lane_d[0:1, :] % 6 >= 3
    diff_l = (placed[:, 0:LANES] + jnp.where(used & hi_half, 1.0, 0.0)).astype(BF16)
    diff_r = (jnp.where(used & ~hi_half, 1.0, 0.0) - placed[:, LANES:2 * LANES]).astype(BF16)

    probs = [(slice(c * chunk, (c + 1) * chunk), slice(g * GROUP_W, (g + 1) * GROUP_W))
             for c in range(n_chunks) for g in range(HEADS // GROUP)]
    stack = lambda x, pr: _stack_heads(x[pr[0], pr[1]], head_of_lane)
    stack_d = lambda x, c, g: jnp.concatenate(
        [jnp.where(head_d == GROUP * g + hh, x[c * chunk:(c + 1) * chunk], jnp.zeros((chunk, LANES), BF16))
         for hh in range(GROUP)], axis=0)
    diffs = [_dot_nt(stack_d(diff_l, c, g), stack_d(diff_r, c, g))
             for c in range(n_chunks) for g in range(HEADS // GROUP)]
    aqs = [_dot_nt(jnp.concatenate([stack(kb, pr), stack(q, pr)], axis=0), stack(k, pr)) for pr in probs]
    decays = [jnp.exp(jnp.where(incl, d, -1e30)) for d in diffs]
    for pr, aq, decay in zip(probs, aqs, decays):
        qkw_ref[pr[0], pr[1]] = _fold_heads(aq[rows:2 * rows] * decay, chunk).astype(BF16)
    pws = [-(aq[0:rows] * jnp.where(strict, decay, 0.0)) for aq, decay in zip(aqs, decays)]
    tinvs = [eye + pw for pw in pws]
    pws = [pw.astype(BF16) for pw in pws]
    pws = [_dot(pw, pw).astype(BF16) for pw in pws]
    for it in range(n_double):
        if it < n_double - 1:
            both = [_dot(jnp.concatenate([pw, tinv.astype(BF16)], axis=0), pw) for pw, tinv in zip(pws, tinvs)]
            pws = [b[0:rows].astype(BF16) for b in both]
            tinvs = [tinv + b[rows:2 * rows] for tinv, b in zip(tinvs, both)]
        else:
            tinvs = [tinv + _dot(tinv.astype(BF16), pw) for pw, tinv in zip(pws, tinvs)]
    uws = [_dot(tinv.astype(BF16), jnp.concatenate([stack(vb, pr), stack(kbe, pr)], axis=1))
           for pr, tinv in zip(probs, tinvs)]
    for pr, uw in zip(probs, uws):
        u_ref[pr[0], pr[1]] = _fold_heads(uw[:, 0:GROUP_W], chunk).astype(BF16)
        w_ref[pr[0], pr[1]] = _fold_heads(uw[:, GROUP_W:2 * GROUP_W], chunk).astype(BF16)


def _gdn_scan_kernel(u_ref, w_ref, qe_ref, kdec_ref, qkw_ref, egl_ref, o_ref, sfin_ref, state, *, chunk):
    c_idx = pl.program_id(0)
    batch = u_ref.shape[0]

    @pl.when(c_idx == 0)
    def _():
        state[...] = jnp.zeros_like(state)

    rows = GROUP * chunk
    lane = lax.broadcasted_iota(jnp.int32, (chunk, GROUP_W), 1) // HEAD_DIM
    ri = lax.broadcasted_iota(jnp.int32, (GROUP_W, GROUP_W), 0)
    ci = lax.broadcasted_iota(jnp.int32, (GROUP_W, GROUP_W), 1)
    same = (ri // HEAD_DIM) == (ci // HEAD_DIM)
    probs = [(b, g, slice(g * GROUP_W, (g + 1) * GROUP_W)) for b in range(batch) for g in range(HEADS // GROUP)]
    s_fs = [state[b, g] for b, g, _ in probs]
    s_bs = [s_f.astype(BF16) for s_f in s_fs]
    v_news = [(u_ref[b, :, ls].astype(F32) - _dot(w_ref[b, :, ls], s_b)).astype(BF16)
              for (b, _, ls), s_b in zip(probs, s_bs)]
    for (b, _, ls), s_b, v_new in zip(probs, s_bs, v_news):
        o_ref[b, :, ls] = _dot(jnp.concatenate([qe_ref[b, :, ls], qkw_ref[b, :, ls]], axis=1),
                               jnp.concatenate([s_b, _stack_heads(v_new, lane)], axis=0))
    kvs = [_dot_tn(kdec_ref[b, :, ls], v_new) for (b, _, ls), v_new in zip(probs, v_news)]
    for (b, g, ls), s_f, kv in zip(probs, s_fs, kvs):
        state[b, g] = s_f * egl_ref[b, 0, 0:1, ls] + jnp.where(same, kv, 0.0)

    @pl.when(c_idx == pl.num_programs(0) - 1)
    def _():
        for b in range(batch):
            for hd in range(HEADS):
                g, hh = divmod(hd, GROUP)
                sl = slice(hh * HEAD_DIM, (hh + 1) * HEAD_DIM)
                sfin_ref[b, hd] = state[b, g, sl, sl]


def _gdn_long(gqkv, small, conv_w, batch, seq, chunk):
    tm = TOKEN_TILE
    assert seq % tm == 0 and tm % chunk == 0 and GROUP * chunk == GROUP_W
    nt = seq // tm
    nc = seq // chunk
    r = np.arange(tm)
    tri = jnp.asarray((r[None, :] <= r[:, None]) & ((r[None, :] // chunk) == (r[:, None] // chunk)), BF16)
    head_of = np.arange(WIDTH) // HEAD_DIM
    hsum = jnp.asarray(head_of[:, None] == (np.arange(LANES)[None, :] - L_G), BF16)
    hexp_g = jnp.asarray((np.arange(LANES)[:, None] - L_G) == head_of[None, :], BF16)
    hexp_b = jnp.asarray((np.arange(LANES)[:, None] - L_BETA) == head_of[None, :], BF16)
    conv_w8 = jnp.concatenate([conv_w, jnp.zeros((SUBLANES - CONV_WIDTH, CONV_DIM), F32)], axis=0)
    dexp = np.zeros((3, LANES, 2 * LANES), np.float32)
    for t in range(3):
        for hd in range(HEADS):
            dexp[t, L_G + hd, 6 * hd + t] = 1.0
            dexp[t, L_G + hd, LANES + 6 * hd + 3 + t] = 1.0
    dexp = jnp.asarray(dexp, BF16)
    row = lambda w: pl.BlockSpec((tm, w), lambda b, i: (b * nt + i, 0))
    per_tile = tm // chunk
    tok = jax.ShapeDtypeStruct((batch * seq, WIDTH), BF16)
    u, w, qe, kdec, qkw, egl = pl.pallas_call(
        functools.partial(_gdn_local_kernel, chunk=chunk),
        grid=(batch, nt),
        in_specs=[row(CONV_DIM),
                  pl.BlockSpec((SUBLANES, CONV_DIM),
                               lambda b, i: (jnp.maximum((b * nt + i) * (tm // SUBLANES) - 1, 0), 0)),
                  row(LANES), _const_spec((SUBLANES, CONV_DIM)), _const_spec((tm, tm)),
                  _const_spec((WIDTH, LANES)), _const_spec((LANES, WIDTH)), _const_spec((LANES, WIDTH)),
                  _const_spec((3, LANES, 2 * LANES))],
        out_specs=(row(WIDTH), row(WIDTH), row(WIDTH), row(WIDTH), row(WIDTH),
                   pl.BlockSpec((per_tile, SUBLANES, WIDTH), lambda b, i: (b * nt + i, 0, 0))),
        out_shape=(tok, tok, tok, tok, tok, jax.ShapeDtypeStruct((batch * nc, SUBLANES, WIDTH), F32)),
        scratch_shapes=[pltpu.VMEM((tm + SUBLANES, CONV_DIM), F32)],
        compiler_params=pltpu.CompilerParams(dimension_semantics=("arbitrary", "arbitrary"),
                                             vmem_limit_bytes=VMEM_LIMIT),
        name="gdn_local",
    )(gqkv, gqkv, small, conv_w8, tri, hsum, hexp_g, hexp_b, dexp)

    blk = pl.BlockSpec((batch, chunk, WIDTH), lambda c: (0, c, 0))
    as3 = lambda a: a.reshape(batch, seq, WIDTH)
    o, s_fin = pl.pallas_call(
        functools.partial(_gdn_scan_kernel, chunk=chunk),
        grid=(nc,),
        in_specs=[blk, blk, blk, blk, blk,
                  pl.BlockSpec((batch, 1, SUBLANES, WIDTH), lambda c: (0, c, 0, 0))],
        out_specs=(blk, pl.BlockSpec((batch, HEADS, HEAD_DIM, HEAD_DIM), lambda c: (0, 0, 0, 0))),
        out_shape=(jax.ShapeDtypeStruct((batch, seq, WIDTH), F32),
                   jax.ShapeDtypeStruct((batch, HEADS, HEAD_DIM, HEAD_DIM), F32)),
        scratch_shapes=[pltpu.VMEM((batch, HEADS // GROUP, GROUP_W, GROUP_W), F32)],
        compiler_params=pltpu.CompilerParams(dimension_semantics=("arbitrary",), vmem_limit_bytes=VMEM_LIMIT),
        name="gdn_scan",
    )(as3(u), as3(w), as3(qe), as3(kdec), as3(qkw), egl.reshape(batch, nc, SUBLANES, WIDTH))
    return o.reshape(batch * seq, WIDTH), s_fin


def _post_kernel(x_ref, fox_ref, gdn_ref, sgg_ref, sga_ref, sgb_ref,
                 wpa_ref, wpb_ref, wout_ref, wup_ref, wdown_ref,
                 ng_ref, gpost_ref, gpre2_ref, gpost2_ref, hsum_ref, hexp_ref, y_ref):
    ya = _dot(fox_ref[...], wpa_ref[...])
    o = gdn_ref[...]
    ms = _dot_sel(o * o, hsum_ref[...], terms=2) * (1.0 / HEAD_DIM)
    o = o * _dot_sel(lax.rsqrt(ms + EPS), hexp_ref[...], terms=2) * ng_ref[...] * sgg_ref[...].astype(F32)
    yb = _dot(o.astype(BF16), wpb_ref[...])
    m = sga_ref[...].astype(F32) * ya + sgb_ref[...].astype(F32) * yb
    mix = _dot(m.astype(BF16), wout_ref[...])
    y1 = x_ref[...] + _rms(mix, gpost_ref[...])
    h2 = _rms(y1, gpre2_ref[...]).astype(BF16)
    acc = jnp.zeros(y1.shape, F32)
    for c in range(D_FF // FF_TILE):
        u = jnp.maximum(_dot(h2, wup_ref[:, c * FF_TILE:(c + 1) * FF_TILE]), 0.0)
        acc = acc + _dot((u * u).astype(BF16), wdown_ref[c * FF_TILE:(c + 1) * FF_TILE, :])
    y_ref[...] = y1 + _rms(acc, gpost2_ref[...])


def _post(x, fox, gdn_o, sgg, sga, sgb, w_pa, w_pb, w_out, w_up, w_down, ng, gpost, gpre2, gpost2):
    t_total = x.shape[0]
    tm = min(POST_TILE, t_total)
    assert t_total % tm == 0
    row = lambda w: pl.BlockSpec((tm, w), lambda i: (i, 0))
    head_of = np.arange(WIDTH) // HEAD_DIM
    hsum = jnp.asarray(head_of[:, None] == (np.arange(LANES)[None, :] - L_G), BF16)
    hexp = jnp.asarray((np.arange(LANES)[:, None] - L_G) == head_of[None, :], BF16)
    return pl.pallas_call(
        _post_kernel,
        grid=(t_total // tm,),
        in_specs=[row(D_MODEL), row(WIDTH), row(WIDTH), row(WIDTH), row(D_MODEL), row(D_MODEL),
                  _const_spec((WIDTH, D_MODEL)), _const_spec((WIDTH, D_MODEL)), _const_spec((D_MODEL, D_MODEL)),
                  _const_spec((D_MODEL, D_FF)), _const_spec((D_FF, D_MODEL)),
                  _const_spec((1, WIDTH)), _const_spec((1, D_MODEL)), _const_spec((1, D_MODEL)),
                  _const_spec((1, D_MODEL)), _const_spec((WIDTH, LANES)), _const_spec((LANES, WIDTH))],
        out_specs=row(D_MODEL),
        out_shape=jax.ShapeDtypeStruct((t_total, D_MODEL), F32),
        compiler_params=pltpu.CompilerParams(dimension_semantics=("arbitrary",), vmem_limit_bytes=VMEM_LIMIT),
        name="post",
    )(x, fox, gdn_o, sgg, sga, sgb, w_pa, w_pb, w_out, w_up, w_down, ng, gpost, gpre2, gpost2, hsum, hexp)


def _rearrange_w_in(w_in):
    o_ff = 3 * WIDTH
    o_gqkv = o_ff + HEADS
    o_ga = o_gqkv + CONV_DIM
    o_gb = o_ga + HEADS
    o_gg = o_gb + HEADS
    o_gate_a = o_gg + WIDTH
    o_gate_b = o_gate_a + D_MODEL
    cols = [w_in[:, 0:o_ff], w_in[:, o_gqkv:o_ga], w_in[:, o_gg:o_gate_a], w_in[:, o_gate_a:o_gate_b],
            w_in[:, o_gate_b:o_gate_b + D_MODEL], w_in[:, o_ff:o_gqkv], w_in[:, o_ga:o_gb], w_in[:, o_gb:o_gg],
            jnp.zeros((D_MODEL, LANES - 3 * HEADS), w_in.dtype)]
    return jnp.concatenate(cols, axis=1).astype(BF16)


def _lane_row(*pieces):
    v = jnp.concatenate([p.astype(F32) for p in pieces])
    return jnp.concatenate([v, jnp.zeros((LANES - v.shape[0],), F32)])[None, :]


def _pad_hist(conv_cache):
    b = conv_cache.shape[0]
    return jnp.concatenate([jnp.zeros((b, SUBLANES - (CONV_WIDTH - 1), CONV_DIM), F32), conv_cache.astype(F32)], axis=1)


def kernel(x_prompt, x_sample, cache_fox_k, cache_fox_v, cache_fox_logf, state_gdn, state_gdn_conv,
           w_in, fox_forget_bias, gdn_conv_w, gdn_a_log, gdn_dt_bias, gdn_norm_g,
           w_proj_fox, w_proj_gdn, w_out, norm_mix_pre, norm_mix_post, norm_mlp_pre, norm_mlp_post,
           w_up, w_down):
    depth = w_in.shape[0]
    bp, sp, _ = x_prompt.shape
    bs, ss, _ = x_sample.shape
    past = cache_fox_k.shape[2]
    y_p = x_prompt.reshape(bp * sp, D_MODEL)
    y_s = x_sample.reshape(bs * ss, D_MODEL)
    st_p, st_s = [], []
    for l in range(depth):
        w_r = _rearrange_w_in(w_in[l])
        padd = _lane_row(fox_forget_bias[l], gdn_dt_bias[l])
        palog = _lane_row(jnp.zeros((HEADS,), F32), gdn_a_log[l])
        gain = norm_mix_pre[l][None, :]
        post_w = (w_proj_fox[l].astype(BF16), w_proj_gdn[l].astype(BF16), w_out[l].astype(BF16),
                  w_up[l].astype(BF16), w_down[l].astype(BF16),
                  jnp.tile(gdn_norm_g[l], HEADS)[None, :], norm_mix_post[l][None, :],
                  norm_mlp_pre[l][None, :], norm_mlp_post[l][None, :])

        q, k, vt, fk, fv, gqkv, sgg, sga, sgb, small, cum = _in_proj(y_p, gain, w_r, padd, palog, sp)
        fox = _fox_prompt(q, k, vt, cum, bp, sp)
        gdn_o, s_fin = _gdn_long(gqkv, small, gdn_conv_w[l], bp, sp, GDN_CHUNK)
        y_p_new = _post(y_p, fox, gdn_o, sgg, sga, sgb, *post_w)
        st_p.append((fk.reshape(bp, sp, HEADS, HEAD_DIM), fv.reshape(bp, sp, HEADS, HEAD_DIM),
                     small[:, L_LOGF:L_LOGF + HEADS].reshape(bp, sp, HEADS), s_fin,
                     gqkv.reshape(bp, sp, CONV_DIM)[:, sp - (CONV_WIDTH - 1):, :]))
        y_p = y_p_new

        q, k, vt, fk, fv, gqkv, sgg, sga, sgb, small, cum = _in_proj(y_s, gain, w_r, padd, palog, ss)
        fox = _fox_sample(q, fk.reshape(bs * ss, WIDTH), fv.reshape(bs * ss, WIDTH), small,
                          cache_fox_k[l].reshape(bs, past, WIDTH), cache_fox_v[l].reshape(bs, past, WIDTH),
                          cache_fox_logf[l], bs, ss, past)
        gdn_o, s_new = _gdn(gqkv, small, _pad_hist(state_gdn_conv[l]), state_gdn[l].astype(F32),
                            gdn_conv_w[l], bs, ss, ss)
        y_s_new = _post(y_s, fox, gdn_o, sgg, sga, sgb, *post_w)
        conv_ext = jnp.concatenate([state_gdn_conv[l].astype(F32), gqkv.reshape(bs, ss, CONV_DIM)], axis=1)
        st_s.append((fk.reshape(bs, ss, HEADS, HEAD_DIM), fv.reshape(bs, ss, HEADS, HEAD_DIM),
                     small[:, L_LOGF:L_LOGF + HEADS].reshape(bs, ss, HEADS), s_new,
                     conv_ext[:, conv_ext.shape[1] - (CONV_WIDTH - 1):, :]))
        y_s = y_s_new

    fk_p, fv_p, lf_p, sg_p, cv_p = [jnp.stack(a) for a in zip(*st_p)]
    fk_s, fv_s, lf_s, sg_s, cv_s = [jnp.stack(a) for a in zip(*st_s)]
    return (y_p.reshape(bp, sp, D_MODEL), y_s.reshape(bs, ss, D_MODEL),
            fk_p, fv_p, lf_p, sg_p, cv_p, fk_s, fv_s, lf_s, sg_s, cv_s)
```

```python
import functools

import jax
import jax.numpy as jnp
import numpy as np
from jax import lax
from jax.experimental import pallas as pl
from jax.experimental.pallas import tpu as pltpu

F32 = jnp.float32
BF16 = jnp.bfloat16

D_MODEL = 1024
HEADS = 8
HEAD_DIM = 64
WIDTH = HEADS * HEAD_DIM
CONV_DIM = 3 * WIDTH
CONV_WIDTH = 4
D_FF = 4 * D_MODEL
EPS = 1e-6
LOG2E = 1.4426950408889634

LANES = 128
SUBLANES = 8
TOKEN_TILE = 256
POST_TILE = 512
ATTN_TILE = 256
ATTN_HEADS = 4
KEY_TILE = 512
FF_TILE = 1024
GDN_CHUNK = 64
VMEM_LIMIT = 60 * 1024 * 1024

C_Q, C_K, C_V, C_GQKV, C_GG, C_GA, C_GB, C_SMALL = 0, 512, 1024, 1536, 3072, 3584, 4608, 5632
IN_COLS = C_SMALL + LANES
L_LOGF, L_G, L_BETA = 0, 8, 16


def _split3(x):
    hi = x.astype(BF16)
    r = x - hi.astype(F32)
    mid = r.astype(BF16)
    lo = (r - mid.astype(F32)).astype(BF16)
    return hi, mid, lo


def _dot(a, b):
    return jnp.dot(a, b, preferred_element_type=F32)


def _dot_nt(a, b):
    return lax.dot_general(a, b, (((1,), (1,)), ((), ())), preferred_element_type=F32)


def _dot_tn(a, b):
    return lax.dot_general(a, b, (((0,), (0,)), ((), ())), preferred_element_type=F32)


def _sel_dot(sel, x, terms=3):
    parts = _split3(x)[:terms]
    n = x.shape[1]
    if n % LANES:
        return sum(_dot(sel, p) for p in parts)
    wide = _dot(sel, jnp.concatenate(parts, axis=1))
    return sum(wide[:, t * n:(t + 1) * n] for t in range(terms))


def _dot_sel(x, sel, terms=3):
    parts = _split3(x)[:terms]
    if terms == 1:
        return _dot(parts[0], sel)
    out = _dot(jnp.concatenate(parts[0:2], axis=1), jnp.concatenate([sel, sel], axis=0))
    for p in parts[2:]:
        out = out + _dot(p, sel)
    return out


def _sigmoid(x):
    return 1.0 / (1.0 + jnp.exp(-x))


def _rms(x, gain):
    ms = jnp.mean(x * x, axis=-1, keepdims=True)
    return x * lax.rsqrt(ms + EPS) * gain


def _const_spec(shape):
    nd = len(shape)
    return pl.BlockSpec(shape, lambda *_: (0,) * nd, pipeline_mode=pl.Buffered(1))


def _inproj_kernel(x_ref, gain_ref, w_ref, padd_ref, palog_ref, tri_ref,
                   q_ref, k_ref, vt_ref, fk_ref, fv_ref, gqkv_ref, sgg_ref, sga_ref, sgb_ref,
                   small_ref, cum_ref, carry_ref, *, tiles_per_seg):
    i = pl.program_id(0)
    h = _rms(x_ref[...], gain_ref[...]).astype(BF16)

    def mm(c0, width):
        return _dot(h, w_ref[:, c0:c0 + width])

    zq = mm(C_Q, WIDTH)
    q_ref[...] = (zq * (LOG2E * HEAD_DIM ** -0.5)).astype(BF16)
    zk = mm(C_K, WIDTH)
    k_ref[...] = zk.astype(BF16)
    zv = mm(C_V, WIDTH)
    fk_ref[...] = zk.reshape(fk_ref.shape)
    fv_ref[...] = zv.reshape(fv_ref.shape)
    vt_ref[0] = zv.T.astype(BF16)
    for c in range(3):
        gqkv_ref[:, c * WIDTH:(c + 1) * WIDTH] = mm(C_GQKV + c * WIDTH, WIDTH)
    zg = mm(C_GG, WIDTH)
    sgg_ref[...] = (zg * _sigmoid(zg)).astype(BF16)
    for c in range(2):
        sga_ref[:, c * WIDTH:(c + 1) * WIDTH] = _sigmoid(mm(C_GA + c * WIDTH, WIDTH)).astype(BF16)
        sgb_ref[:, c * WIDTH:(c + 1) * WIDTH] = _sigmoid(mm(C_GB + c * WIDTH, WIDTH)).astype(BF16)

    zs = mm(C_SMALL, LANES)
    t = zs + padd_ref[...]
    l1p = jnp.log1p(jnp.exp(-jnp.abs(t)))
    logf = jnp.minimum(t, 0.0) - l1p
    g = -jnp.exp(palog_ref[...]) * (jnp.maximum(t, 0.0) + l1p)
    beta = _sigmoid(zs)
    lane = lax.broadcasted_iota(jnp.int32, zs.shape, 1)
    small = jnp.where(lane < L_G, logf, jnp.where(lane < L_BETA, g, jnp.where(lane < L_BETA + HEADS, beta, 0.0)))
    small_ref[...] = small

    cum = _sel_dot(tri_ref[...], small)
    if tiles_per_seg > 1:
        @pl.when(i % tiles_per_seg == 0)
        def _():
            carry_ref[...] = jnp.zeros_like(carry_ref)
        cum = cum + carry_ref[0:1, :]
        carry_ref[...] = jnp.broadcast_to(cum[-1:, :], carry_ref.shape)
    cum_ref[...] = cum


def _in_proj(x, gain, w_r, padd, palog, seg_len):
    t_total = x.shape[0]
    tm = TOKEN_TILE
    assert t_total % tm == 0
    nt = t_total // tm
    if seg_len >= tm:
        assert seg_len % tm == 0
        tiles_per_seg = seg_len // tm
        r = np.arange(tm)
        tri = (r[None, :] <= r[:, None])
    else:
        assert tm % seg_len == 0
        tiles_per_seg = 1
        r = np.arange(tm)
        tri = (r[None, :] <= r[:, None]) & ((r[None, :] // seg_len) == (r[:, None] // seg_len))
    tri = jnp.asarray(tri, BF16)

    row = lambda w: pl.BlockSpec((tm, w), lambda i: (i, 0))
    out_shape = (
        jax.ShapeDtypeStruct((t_total, WIDTH), BF16),
        jax.ShapeDtypeStruct((t_total, WIDTH), BF16),
        jax.ShapeDtypeStruct((nt, WIDTH, tm), BF16),
        jax.ShapeDtypeStruct((t_total, HEADS, HEAD_DIM), F32),
        jax.ShapeDtypeStruct((t_total, HEADS, HEAD_DIM), F32),
        jax.ShapeDtypeStruct((t_total, CONV_DIM), F32),
        jax.ShapeDtypeStruct((t_total, WIDTH), BF16),
        jax.ShapeDtypeStruct((t_total, D_MODEL), BF16),
        jax.ShapeDtypeStruct((t_total, D_MODEL), BF16),
        jax.ShapeDtypeStruct((t_total, LANES), F32),
        jax.ShapeDtypeStruct((t_total, LANES), F32),
    )
    per_head = pl.BlockSpec((tm, HEADS, HEAD_DIM), lambda i: (i, 0, 0))
    out_specs = (row(WIDTH), row(WIDTH), pl.BlockSpec((1, WIDTH, tm), lambda i: (i, 0, 0)),
                 per_head, per_head, row(CONV_DIM), row(WIDTH), row(D_MODEL), row(D_MODEL),
                 row(LANES), row(LANES))
    return pl.pallas_call(
        functools.partial(_inproj_kernel, tiles_per_seg=tiles_per_seg),
        grid=(nt,),
        in_specs=[row(D_MODEL), _const_spec((1, D_MODEL)), _const_spec((D_MODEL, IN_COLS)),
                  _const_spec((1, LANES)), _const_spec((1, LANES)), _const_spec((tm, tm))],
        out_specs=out_specs,
        out_shape=out_shape,
        scratch_shapes=[pltpu.VMEM((SUBLANES, LANES), F32)],
        compiler_params=pltpu.CompilerParams(dimension_semantics=("arbitrary",), vmem_limit_bytes=VMEM_LIMIT),
        name="in_proj",
    )(x, gain, w_r, padd, palog, tri)


def _fox_prompt_kernel(q_ref, k_ref, vt_ref, cum_ref, o_ref, bias_ref, s0_ref, s1_ref, p0_ref, p1_ref,
                       *, seq, tq, tk, nh):
    grp = pl.program_id(1)
    qi = pl.program_id(2)
    rows = 512
    vt_tile = vt_ref.shape[2]
    sub = tk // vt_tile
    heads = range(nh)
    s_slots = (s0_ref, s1_ref)
    p_slots = (p0_ref, p1_ref)

    @pl.when(qi == 0)
    def _prep():
        def fill(r, carry):
            c = cum_ref[pl.ds(r * rows, rows), :]
            lane = lax.broadcasted_iota(jnp.int32, c.shape, 1)
            slab = jnp.zeros(c.shape, F32)
            for hh in heads:
                col = jnp.sum(jnp.where(lane == nh * grp + hh, c, 0.0), axis=-1, keepdims=True) * (-LOG2E)
                for t, part in enumerate(_split3(col)):
                    slab = jnp.where(lane == 3 * hh + t, part.astype(F32), slab)
            bias_ref[pl.ds(r * rows, rows), :] = slab.astype(BF16)
            return carry
        lax.fori_loop(0, seq // rows, fill, 0)

    lane_q = lax.broadcasted_iota(jnp.int32, (tq, LANES), 1)
    qh = []
    for hh in heads:
        q2 = q_ref[:, (hh // 2) * LANES:(hh // 2 + 1) * LANES]
        half = hh % 2
        q_m = jnp.where((lane_q >= HEAD_DIM * half) & (lane_q < HEAD_DIM * (half + 1)), q2, jnp.zeros_like(q2))
        ones = jnp.where((lane_q >= 3 * hh) & (lane_q < 3 * hh + 3), 1.0, 0.0)
        qh.append(jnp.concatenate([q_m.astype(F32), ones], axis=1).T.astype(BF16))

    def scores_to(slot, j):
        bias = bias_ref[pl.ds(j * tk, tk), :]
        kjs = [jnp.concatenate([k_ref[pl.ds(j * tk, tk), sl * LANES:(sl + 1) * LANES], bias], axis=1)
               for sl in range(nh // 2)]
        s_ts = [_dot(kjs[hh // 2], qh[hh]) for hh in heads]
        for hh in heads:
            s_slots[slot][hh] = s_ts[hh]
        return tuple(jnp.max(s_t, axis=0, keepdims=True) for s_t in s_ts)

    ones_rows = jnp.ones((2 * SUBLANES, vt_tile), BF16)

    def add_values(slot, j, accs, alphas):
        pvs = [sum(_dot(jnp.concatenate([vt_ref[j * sub + t, HEAD_DIM * hh:HEAD_DIM * (hh + 1), :], ones_rows], axis=0),
                        p_slots[slot][hh, t * vt_tile:(t + 1) * vt_tile, :]) for t in range(sub)) for hh in heads]
        return tuple(alphas[hh] * accs[hh] + pvs[hh] for hh in heads)

    def rescale(ms, bms):
        m_new = tuple(jnp.maximum(ms[hh], bms[hh]) for hh in heads)
        return m_new, tuple(jnp.exp2(ms[hh] - m_new[hh]) for hh in heads)

    def probabilities(slot, hh, m_new, visible_from):
        s_t = s_slots[slot][hh]
        if visible_from is not None:
            s_t = jnp.where(visible_from, s_t, -1e30)
        p_slots[slot][hh] = jnp.exp2(s_t - m_new).astype(BF16)

    def trip(j, rd, carry):
        ms, accs, a_prev, bms = carry
        wr = 1 - rd
        accs = add_values(wr, jnp.maximum(j - 1, 0), accs, a_prev)
        m_new, alphas = rescale(ms, bms)
        for hh in heads:
            probabilities(rd, hh, m_new[hh], None)
        bm_next = scores_to(wr, j + 1)
        return m_new, accs, alphas, bm_next

    n_full = (qi * tq) // tk

    def last(rd, carry):
        ms, accs, a_prev, _ = carry
        accs = add_values(1 - rd, jnp.maximum(n_full - 1, 0), accs, a_prev)
        kr = lax.broadcasted_iota(jnp.int32, (tk, tq), 0)
        qc = lax.broadcasted_iota(jnp.int32, (tk, tq), 1)
        visible = kr <= qc + (qi * tq - n_full * tk)
        bms = tuple(jnp.max(jnp.where(visible, s_slots[rd][hh], -1e30), axis=0, keepdims=True) for hh in heads)
        m_new, alphas = rescale(ms, bms)
        for hh in heads:
            probabilities(rd, hh, m_new[hh], visible)
        accs = add_values(rd, n_full, accs, alphas)
        return jnp.concatenate([accs[hh][0:HEAD_DIM] / accs[hh][HEAD_DIM:HEAD_DIM + 1] for hh in heads], axis=0)

    each = lambda f: tuple(f() for _ in heads)
    p1_ref[...] = jnp.zeros(p1_ref.shape, BF16)
    init = (each(lambda: jnp.full((1, tq), -1e30, F32)), each(lambda: jnp.zeros((HEAD_DIM + 2 * SUBLANES, tq), F32)),
            each(lambda: jnp.ones((1, tq), F32)), scores_to(0, 0))
    carry = lax.fori_loop(
        0, n_full, lambda j, c: lax.cond(j % 2 == 0, lambda c: trip(j, 0, c), lambda c: trip(j, 1, c), c), init)
    o_t = lax.cond(n_full % 2 == 0, lambda c: last(0, c), lambda c: last(1, c), carry)
    o_ref[...] = o_t.T.astype(BF16)


def _fox_prompt(q, k, vt, cum, batch, seq):
    tq, tk, nh = ATTN_TILE, KEY_TILE, ATTN_HEADS
    assert seq % tk == 0 and tk % tq == 0 and tk % TOKEN_TILE == 0 and HEADS % nh == 0 and nh % 2 == 0
    nq = seq // tq
    width = nh * HEAD_DIM
    return pl.pallas_call(
        functools.partial(_fox_prompt_kernel, seq=seq, tq=tq, tk=tk, nh=nh),
        grid=(batch, HEADS // nh, nq),
        in_specs=[pl.BlockSpec((tq, width), lambda b, g, i: (b * nq + i, g)),
                  pl.BlockSpec((seq, width), lambda b, g, i: (b, g)),
                  pl.BlockSpec((seq // TOKEN_TILE, width, TOKEN_TILE), lambda b, g, i: (b, g, 0)),
                  pl.BlockSpec((seq, LANES), lambda b, g, i: (b, 0))],
        out_specs=pl.BlockSpec((tq, width), lambda b, g, i: (b * nq + i, g)),
        out_shape=jax.ShapeDtypeStruct((batch * seq, WIDTH), BF16),
        scratch_shapes=[pltpu.VMEM((seq, LANES), BF16),
                        pltpu.VMEM((nh, tk, tq), F32), pltpu.VMEM((nh, tk, tq), F32),
                        pltpu.VMEM((nh, tk, tq), BF16), pltpu.VMEM((nh, tk, tq), BF16)],
        compiler_params=pltpu.CompilerParams(dimension_semantics=("arbitrary",) * 3, vmem_limit_bytes=VMEM_LIMIT),
        name="fox_prompt",
    )(q, k, vt, cum)


def _fox_sample_kernel(q_ref, kn_ref, vn_ref, small_ref, kc_ref, vc_ref, lfc_ref, tri_ref, trin_ref, o_ref,
                       *, past, steps):
    blk = tri_ref.shape[0]
    carry = jnp.zeros((1, HEADS), F32)
    cums = []
    for r in range(past // blk):
        c = _sel_dot(tri_ref[...], lfc_ref[0, r * blk:(r + 1) * blk, :]) + carry
        cums.append(c)
        carry = c[-1:, :]
    cum_c = jnp.concatenate(cums, axis=0)
    cum_n = _sel_dot(trin_ref[...], small_ref[:, L_LOGF:L_LOGF + HEADS]) + carry

    q = q_ref[...]
    flat = lambda ref: jnp.concatenate(
        [ref[0, r:r + blk].reshape(blk, WIDTH).astype(BF16) for r in range(0, past, blk)], axis=0)
    kc = flat(kc_ref)
    vc = flat(vc_ref)
    kn = kn_ref[...].astype(BF16)
    vn = vn_ref[...].astype(BF16)
    lane_c = lax.broadcasted_iota(jnp.int32, (past, LANES), 1)
    lane_n = lax.broadcasted_iota(jnp.int32, (steps, LANES), 1)
    kr = lax.broadcasted_iota(jnp.int32, (steps, steps), 0)
    qc = lax.broadcasted_iota(jnp.int32, (steps, steps), 1)
    ones_c = jnp.ones((past, LANES), BF16)
    ones_n = jnp.ones((steps, LANES), BF16)
    heads = range(HEADS)
    slab = [slice((hd // 2) * LANES, (hd // 2 + 1) * LANES) for hd in heads]
    in_c = [(lane_c >= HEAD_DIM * (hd % 2)) & (lane_c < HEAD_DIM * (hd % 2 + 1)) for hd in heads]
    in_n = [(lane_n >= HEAD_DIM * (hd % 2)) & (lane_n < HEAD_DIM * (hd % 2 + 1)) for hd in heads]
    qms = [jnp.where(in_n[hd], q[:, slab[hd]], jnp.zeros((steps, LANES), BF16)) for hd in heads]
    s1s = [_dot_nt(kc[:, slab[hd]], qms[hd]) - LOG2E * cum_c[:, hd:hd + 1] for hd in heads]
    s2s = [jnp.where(kr <= qc, _dot_nt(kn[:, slab[hd]], qms[hd]) - LOG2E * cum_n[:, hd:hd + 1], -1e30)
           for hd in heads]
    ms = [jnp.maximum(jnp.max(s1, axis=0, keepdims=True), jnp.max(s2, axis=0, keepdims=True))
          for s1, s2 in zip(s1s, s2s)]
    p1s = [jnp.exp2(s1 - m).astype(BF16) for s1, m in zip(s1s, ms)]
    p2s = [jnp.exp2(s2 - m).astype(BF16) for s2, m in zip(s2s, ms)]
    nums = [_dot_tn(p1s[hd], jnp.where(in_c[hd], vc[:, slab[hd]], jnp.zeros((past, LANES), BF16)))
            + _dot_tn(p2s[hd], jnp.where(in_n[hd], vn[:, slab[hd]], jnp.zeros((steps, LANES), BF16))) for hd in heads]
    dens = [_dot_tn(p1s[hd], ones_c) + _dot_tn(p2s[hd], ones_n) for hd in heads]
    outs = [n / d for n, d in zip(nums, dens)]
    o_ref[...] = jnp.concatenate([outs[2 * pr] + outs[2 * pr + 1] for pr in range(HEADS // 2)], axis=1).astype(BF16)


def _fox_sample(q, kn, vn, small, kc, vc, lfc, batch, steps, past):
    blk = 256
    assert past % blk == 0
    r = np.arange(blk)
    tri = jnp.asarray(r[None, :] <= r[:, None], BF16)
    rn = np.arange(steps)
    trin = jnp.asarray(rn[None, :] <= rn[:, None], BF16)
    row = lambda w: pl.BlockSpec((steps, w), lambda b: (b, 0))
    return pl.pallas_call(
        functools.partial(_fox_sample_kernel, past=past, steps=steps),
        grid=(batch,),
        in_specs=[row(WIDTH), row(WIDTH), row(WIDTH), row(LANES),
                  pl.BlockSpec((1, past, HEADS, HEAD_DIM), lambda b: (b, 0, 0, 0)),
                  pl.BlockSpec((1, past, HEADS, HEAD_DIM), lambda b: (b, 0, 0, 0)),
                  pl.BlockSpec((1, past, HEADS), lambda b: (b, 0, 0)),
                  _const_spec((blk, blk)), _const_spec((steps, steps))],
        out_specs=row(WIDTH),
        out_shape=jax.ShapeDtypeStruct((batch * steps, WIDTH), BF16),
        compiler_params=pltpu.CompilerParams(dimension_semantics=("arbitrary",), vmem_limit_bytes=VMEM_LIMIT),
        name="fox_sample",
    )(q, kn, vn, small, kc, vc, lfc, tri, trin)


def _gdn_kernel(gq_ref, small_ref, hist_ref, s0_ref, convw_ref, tri_ref, hsum_ref, hexp_g_ref, hexp_b_ref,
                o_ref, sfin_ref, xbuf, state, *, chunk):
    c_idx = pl.program_id(1)
    n_chunks = pl.num_programs(1)
    hist_rows = SUBLANES

    @pl.when(c_idx == 0)
    def _():
        xbuf[0:hist_rows, :] = hist_ref[0]
        state[...] = s0_ref[0]

    xbuf[hist_rows:hist_rows + chunk, :] = gq_ref[...]
    w = convw_ref[...]
    base = hist_rows - (CONV_WIDTH - 1)
    y = xbuf[base:base + chunk, :] * w[0:1, :]
    for i in range(1, CONV_WIDTH):
        y = y + xbuf[base + i:base + i + chunk, :] * w[i:i + 1, :]
    xbuf[0:hist_rows, :] = xbuf[chunk:chunk + hist_rows, :]
    y = y * _sigmoid(y)

    hsum = hsum_ref[...]
    hexp_g = hexp_g_ref[...]
    hexp_b = hexp_b_ref[...]

    def l2n(x):
        inv = lax.rsqrt(_dot_sel(x * x, hsum, terms=2) + EPS)
        return x * _dot_sel(inv, hexp_g, terms=2)

    q = l2n(y[:, 0:WIDTH]) * (HEAD_DIM ** -0.5)
    k = l2n(y[:, WIDTH:2 * WIDTH])
    v = y[:, 2 * WIDTH:3 * WIDTH]

    small = small_ref[...]
    gc = _sel_dot(tri_ref[...], small)
    gc_x = _dot_sel(gc, hexp_g)
    beta_x = _dot_sel(small, hexp_b, terms=2)
    egc_x = jnp.exp(gc_x)
    glast_x = gc_x[chunk - 1:chunk, :]
    eglast_x = jnp.exp(glast_x)
    kb = k * beta_x
    vb = v * beta_x
    kbe = kb * egc_x
    qe = q * egc_x
    kdec = k * jnp.exp(glast_x - gc_x)

    pad_rows = LANES - chunk
    gc_sq = jnp.concatenate([gc, jnp.zeros((pad_rows, LANES), F32)], axis=0) if pad_rows else gc
    gc_t = gc_sq.T

    ri = lax.broadcasted_iota(jnp.int32, (chunk, chunk), 0)
    ci = lax.broadcasted_iota(jnp.int32, (chunk, chunk), 1)
    eye = jnp.where(ri == ci, 1.0, 0.0).astype(F32)
    n_double = int(np.log2(chunk)) - 1
    assert 2 ** (n_double + 1) == chunk

    heads = range(HEADS)
    sls = [slice(hd * HEAD_DIM, (hd + 1) * HEAD_DIM) for hd in heads]
    decs = [jnp.where(ri >= ci, jnp.exp(gc[:, L_G + hd:L_G + hd + 1] - gc_t[L_G + hd:L_G + hd + 1, 0:chunk]), 0.0)
            for hd in heads]
    k_hs = [k[:, sl].astype(BF16) for sl in sls]
    pws = [-(_dot_nt(kb[:, sl].astype(BF16), k_h) * jnp.where(ri > ci, dec, 0.0))
           for sl, k_h, dec in zip(sls, k_hs, decs)]
    qks = [(_dot_nt(q[:, sl].astype(BF16), k_h) * dec).astype(BF16) for sl, k_h, dec in zip(sls, k_hs, decs)]
    tinvs = [eye + pw for pw in pws]
    for _ in range(n_double):
        pws = [pw.astype(BF16) for pw in pws]
        pws = [_dot(pw, pw) for pw in pws]
        tinvs = [tinv + _dot(tinv.astype(BF16), pw.astype(BF16)) for tinv, pw in zip(tinvs, pws)]
    uws = [_dot(tinv.astype(BF16), jnp.concatenate([vb[:, sl], kbe[:, sl]], axis=1).astype(BF16))
           for tinv, sl in zip(tinvs, sls)]
    s_fs = [state[hd] for hd in heads]
    s_bs = [s_f.astype(BF16) for s_f in s_fs]
    v_news = [(uw[:, 0:HEAD_DIM] - _dot(uw[:, HEAD_DIM:2 * HEAD_DIM].astype(BF16), s_b)).astype(BF16)
              for uw, s_b in zip(uws, s_bs)]
    outs = [_dot(qe[:, sl].astype(BF16), s_b) + _dot(qk, v_new)
            for sl, s_b, qk, v_new in zip(sls, s_bs, qks, v_news)]
    for hd in heads:
        state[hd] = s_fs[hd] * eglast_x[:, sls[hd]] + _dot_tn(kdec[:, sls[hd]].astype(BF16), v_news[hd])
    o_ref[...] = jnp.concatenate(outs, axis=1)

    @pl.when(c_idx == n_chunks - 1)
    def _():
        sfin_ref[0] = state[...]


def _gdn(gqkv, small, hist, s0, conv_w, batch, seq, chunk):
    assert seq % chunk == 0 and chunk % SUBLANES == 0 and chunk <= LANES
    nc = seq // chunk
    r = np.arange(chunk)
    tri = jnp.asarray(r[None, :] <= r[:, None], BF16)
    head_of = np.arange(WIDTH) // HEAD_DIM
    hsum = jnp.asarray(head_of[:, None] == (np.arange(LANES)[None, :] - L_G), BF16)
    hexp_g = jnp.asarray((np.arange(LANES)[:, None] - L_G) == head_of[None, :], BF16)
    hexp_b = jnp.asarray((np.arange(LANES)[:, None] - L_BETA) == head_of[None, :], BF16)
    conv_w8 = jnp.concatenate([conv_w, jnp.zeros((SUBLANES - CONV_WIDTH, CONV_DIM), F32)], axis=0)
    return pl.pallas_call(
        functools.partial(_gdn_kernel, chunk=chunk),
        grid=(batch, nc),
        in_specs=[pl.BlockSpec((chunk, CONV_DIM), lambda b, c: (b * nc + c, 0)),
                  pl.BlockSpec((chunk, LANES), lambda b, c: (b * nc + c, 0)),
                  pl.BlockSpec((1, SUBLANES, CONV_DIM), lambda b, c: (b, 0, 0)),
                  pl.BlockSpec((1, HEADS, HEAD_DIM, HEAD_DIM), lambda b, c: (b, 0, 0, 0)),
                  _const_spec((SUBLANES, CONV_DIM)), _const_spec((chunk, chunk)),
                  _const_spec((WIDTH, LANES)), _const_spec((LANES, WIDTH)), _const_spec((LANES, WIDTH))],
        out_specs=(pl.BlockSpec((chunk, WIDTH), lambda b, c: (b * nc + c, 0)),
                   pl.BlockSpec((1, HEADS, HEAD_DIM, HEAD_DIM), lambda b, c: (b, 0, 0, 0))),
        out_shape=(jax.ShapeDtypeStruct((batch * seq, WIDTH), F32),
                   jax.ShapeDtypeStruct((batch, HEADS, HEAD_DIM, HEAD_DIM), F32)),
        scratch_shapes=[pltpu.VMEM((chunk + SUBLANES, CONV_DIM), F32),
                        pltpu.VMEM((HEADS, HEAD_DIM, HEAD_DIM), F32)],
        compiler_params=pltpu.CompilerParams(dimension_semantics=("arbitrary", "arbitrary"),
                                             vmem_limit_bytes=VMEM_LIMIT),
        name="gdn",
    )(gqkv, small, hist, s0, conv_w8, tri, hsum, hexp_g, hexp_b)


GROUP = 4
GROUP_W = GROUP * HEAD_DIM


def _stack_heads(x, lane):
    return jnp.concatenate([jnp.where(lane == hh, x, jnp.zeros_like(x)) for hh in range(GROUP)], axis=0)


def _fold_heads(x, rows):
    out = x[0:rows]
    for hh in range(1, GROUP):
        out = out + x[hh * rows:(hh + 1) * rows]
    return out


def _gdn_local_kernel(gq_ref, prev_ref, small_ref, convw_ref, tri_ref, hsum_ref, hexp_g_ref, hexp_b_ref, dexp_ref,
                      u_ref, w_ref, qe_ref, kdec_ref, qkw_ref, egl_ref, xbuf, *, chunk):
    i = pl.program_id(1)
    tm = gq_ref.shape[0]
    hist_rows = SUBLANES
    prev = prev_ref[...]
    xbuf[0:hist_rows, :] = jnp.where(i == 0, jnp.zeros_like(prev), prev)
    xbuf[hist_rows:hist_rows + tm, :] = gq_ref[...]
    w = convw_ref[...]
    base = hist_rows - (CONV_WIDTH - 1)
    y = xbuf[base:base + tm, :] * w[0:1, :]
    for t in range(1, CONV_WIDTH):
        y = y + xbuf[base + t:base + t + tm, :] * w[t:t + 1, :]
    y = y * _sigmoid(y)

    hsum = hsum_ref[...]
    hexp_g = hexp_g_ref[...]
    hexp_b = hexp_b_ref[...]

    def l2n(x):
        inv = lax.rsqrt(_dot_sel(x * x, hsum, terms=2) + EPS)
        return x * _dot_sel(inv, hexp_g, terms=2)

    q = l2n(y[:, 0:WIDTH]) * (HEAD_DIM ** -0.5)
    k = l2n(y[:, WIDTH:2 * WIDTH])
    v = y[:, 2 * WIDTH:3 * WIDTH]

    small = small_ref[...]
    gc = _sel_dot(tri_ref[...], small)
    gc_x = _dot_sel(gc, hexp_g)
    beta_x = _dot_sel(small, hexp_b, terms=2)
    n_chunks = tm // chunk
    lasts = [gc_x[(c + 1) * chunk - 1:(c + 1) * chunk, :] for c in range(n_chunks)]
    glast_x = jnp.concatenate([jnp.broadcast_to(r, (chunk, WIDTH)) for r in lasts], axis=0)
    egc_x = jnp.exp(gc_x)
    kb = k * beta_x
    vb = (v * beta_x).astype(BF16)
    kbe = (kb * egc_x).astype(BF16)
    kb = kb.astype(BF16)
    qe_ref[...] = (q * egc_x).astype(BF16)
    kdec_ref[...] = (k * jnp.exp(glast_x - gc_x)).astype(BF16)
    for c in range(n_chunks):
        egl_ref[c] = jnp.broadcast_to(jnp.exp(lasts[c]), (SUBLANES, WIDTH))
    q = q.astype(BF16)
    k = k.astype(BF16)

    rows = GROUP * chunk
    lane = lax.broadcasted_iota(jnp.int32, (chunk, GROUP_W), 1)
    head_of_lane = lane // HEAD_DIM
    ri = lax.broadcasted_iota(jnp.int32, (rows, rows), 0)
    ci = lax.broadcasted_iota(jnp.int32, (rows, rows), 1)
    same = (ri // chunk) == (ci // chunk)
    incl = same & (ri >= ci)
    strict = same & (ri > ci)
    eye = jnp.where(ri == ci, 1.0, 0.0).astype(F32)
    n_double = int(np.log2(chunk)) - 1

    placed = None
    for t, part in enumerate(_split3(gc)):
        term = _dot(part, dexp_ref[t])
        placed = term if placed is None else placed + term
    lane_d = lax.broadcasted_iota(jnp.int32, (chunk, LANES), 1)
    head_d = jnp.where(lane_d < 6 * HEADS, lane_d // 6, -1)
    used = lane_d[0:1, :] < 6 * HEADS
    hi_half = lane_d[0:1, :] % 6 >= 3
    diff_l = (placed[:, 0:LANES] + jnp.where(used & hi_half, 1.0, 0.0)).astype(BF16)
    diff_r = (jnp.where(used & ~hi_half, 1.0, 0.0) - placed[:, LANES:2 * LANES]).astype(BF16)

    probs = [(slice(c * chunk, (c + 1) * chunk), slice(g * GROUP_W, (g + 1) * GROUP_W))
             for c in range(n_chunks) for g in range(HEADS // GROUP)]
    stack = lambda x, pr: _stack_heads(x[pr[0], pr[1]], head_of_lane)
    stack_d = lambda x, c, g: jnp.concatenate(
        [jnp.where(head_d == GROUP * g + hh, x[c * chunk:(c + 1) * chunk], jnp.zeros((chunk, LANES), BF16))
         for hh in range(GROUP)], axis=0)
    diffs = [_dot_nt(stack_d(diff_l, c, g), stack_d(diff_r, c, g))
             for c in range(n_chunks) for g in range(HEADS // GROUP)]
    aqs = [_dot_nt(jnp.concatenate([stack(kb, pr), stack(q, pr)], axis=0), stack(k, pr)) for pr in probs]
    decays = [jnp.exp(jnp.where(incl, d, -1e30)) for d in diffs]
    for pr, aq, decay in zip(probs, aqs, decays):
        qkw_ref[pr[0], pr[1]] = _fold_heads(aq[rows:2 * rows] * decay, chunk).astype(BF16)
    pws = [-(aq[0:rows] * jnp.where(strict, decay, 0.0)) for aq, decay in zip(aqs, decays)]
    tinvs = [eye + pw for pw in pws]
    pws = [pw.astype(BF16) for pw in pws]
    pws = [_dot(pw, pw).astype(BF16) for pw in pws]
    for it in range(n_double):
        if it < n_double - 1:
            both = [_dot(jnp.concatenate([pw, tinv.astype(BF16)], axis=0), pw) for pw, tinv in zip(pws, tinvs)]
            pws = [b[0:rows].astype(BF16) for b in both]
            tinvs = [tinv + b[rows:2 * rows] for tinv, b in zip(tinvs, both)]
        else:
            tinvs = [tinv + _dot(tinv.astype(BF16), pw) for pw, tinv in zip(pws, tinvs)]
    uws = [_dot(tinv.astype(BF16), jnp.concatenate([stack(vb, pr), stack(kbe, pr)], axis=1))
           for pr, tinv in zip(probs, tinvs)]
    for pr, uw in zip(probs, uws):
        u_ref[pr[0], pr[1]] = _fold_heads(uw[:, 0:GROUP_W], chunk).astype(BF16)
        w_ref[pr[0], pr[1]] = _fold_heads(uw[:, GROUP_W:2 * GROUP_W], chunk).astype(BF16)


def _gdn_scan_kernel(u_ref, w_ref, qe_ref, kdec_ref, qkw_ref, egl_ref, o_ref, sfin_ref, state, *, chunk):
    c_idx = pl.program_id(0)
    batch = u_ref.shape[0]

    @pl.when(c_idx == 0)
    def _():
        state[...] = jnp.zeros_like(state)

    rows = GROUP * chunk
    lane = lax.broadcasted_iota(jnp.int32, (chunk, GROUP_W), 1) // HEAD_DIM
    ri = lax.broadcasted_iota(jnp.int32, (GROUP_W, GROUP_W), 0)
    ci = lax.broadcasted_iota(jnp.int32, (GROUP_W, GROUP_W), 1)
    same = (ri // HEAD_DIM) == (ci // HEAD_DIM)
    probs = [(b, g, slice(g * GROUP_W, (g + 1) * GROUP_W)) for b in range(batch) for g in range(HEADS // GROUP)]
    s_fs = [state[b, g] for b, g, _ in probs]
    s_bs = [s_f.astype(BF16) for s_f in s_fs]
    v_news = [(u_ref[b, :, ls].astype(F32) - _dot(w_ref[b, :, ls], s_b)).astype(BF16)
              for (b, _, ls), s_b in zip(probs, s_bs)]
    for (b, _, ls), s_b, v_new in zip(probs, s_bs, v_news):
        o_ref[b, :, ls] = _dot(jnp.concatenate([qe_ref[b, :, ls], qkw_ref[b, :, ls]], axis=1),
                               jnp.concatenate([s_b, _stack_heads(v_new, lane)], axis=0))
    kvs = [_dot_tn(kdec_ref[b, :, ls], v_new) for (b, _, ls), v_new in zip(probs, v_news)]
    for (b, g, ls), s_f, kv in zip(probs, s_fs, kvs):
        state[b, g] = s_f * egl_ref[b, 0, 0:1, ls] + jnp.where(same, kv, 0.0)

    @pl.when(c_idx == pl.num_programs(0) - 1)
    def _():
        for b in range(batch):
            for hd in range(HEADS):
                g, hh = divmod(hd, GROUP)
                sl = slice(hh * HEAD_DIM, (hh + 1) * HEAD_DIM)
                sfin_ref[b, hd] = state[b, g, sl, sl]


def _gdn_long(gqkv, small, conv_w, batch, seq, chunk):
    tm = TOKEN_TILE
    assert seq % tm == 0 and tm % chunk == 0 and GROUP * chunk == GROUP_W
    nt = seq // tm
    nc = seq // chunk
    r = np.arange(tm)
    tri = jnp.asarray((r[None, :] <= r[:, None]) & ((r[None, :] // chunk) == (r[:, None] // chunk)), BF16)
    head_of = np.arange(WIDTH) // HEAD_DIM
    hsum = jnp.asarray(head_of[:, None] == (np.arange(LANES)[None, :] - L_G), BF16)
    hexp_g = jnp.asarray((np.arange(LANES)[:, None] - L_G) == head_of[None, :], BF16)
    hexp_b = jnp.asarray((np.arange(LANES)[:, None] - L_BETA) == head_of[None, :], BF16)
    conv_w8 = jnp.concatenate([conv_w, jnp.zeros((SUBLANES - CONV_WIDTH, CONV_DIM), F32)], axis=0)
    dexp = np.zeros((3, LANES, 2 * LANES), np.float32)
    for t in range(3):
        for hd in range(HEADS):
            dexp[t, L_G + hd, 6 * hd + t] = 1.0
            dexp[t, L_G + hd, LANES + 6 * hd + 3 + t] = 1.0
    dexp = jnp.asarray(dexp, BF16)
    row = lambda w: pl.BlockSpec((tm, w), lambda b, i: (b * nt + i, 0))
    per_tile = tm // chunk
    tok = jax.ShapeDtypeStruct((batch * seq, WIDTH), BF16)
    u, w, qe, kdec, qkw, egl = pl.pallas_call(
        functools.partial(_gdn_local_kernel, chunk=chunk),
        grid=(batch, nt),
        in_specs=[row(CONV_DIM),
                  pl.BlockSpec((SUBLANES, CONV_DIM),
                               lambda b, i: (jnp.maximum((b * nt + i) * (tm // SUBLANES) - 1, 0), 0)),
                  row(LANES), _const_spec((SUBLANES, CONV_DIM)), _const_spec((tm, tm)),
                  _const_spec((WIDTH, LANES)), _const_spec((LANES, WIDTH)), _const_spec((LANES, WIDTH)),
                  _const_spec((3, LANES, 2 * LANES))],
        out_specs=(row(WIDTH), row(WIDTH), row(WIDTH), row(WIDTH), row(WIDTH),
                   pl.BlockSpec((per_tile, SUBLANES, WIDTH), lambda b, i: (b * nt + i, 0, 0))),
        out_shape=(tok, tok, tok, tok, tok, jax.ShapeDtypeStruct((batch * nc, SUBLANES, WIDTH), F32)),
        scratch_shapes=[pltpu.VMEM((tm + SUBLANES, CONV_DIM), F32)],
        compiler_params=pltpu.CompilerParams(dimension_semantics=("arbitrary", "arbitrary"),
                                             vmem_limit_bytes=VMEM_LIMIT),
        name="gdn_local",
    )(gqkv, gqkv, small, conv_w8, tri, hsum, hexp_g, hexp_b, dexp)

    blk = pl.BlockSpec((batch, chunk, WIDTH), lambda c: (0, c, 0))
    as3 = lambda a: a.reshape(batch, seq, WIDTH)
    o, s_fin = pl.pallas_call(
        functools.partial(_gdn_scan_kernel, chunk=chunk),
        grid=(nc,),
        in_specs=[blk, blk, blk, blk, blk,
                  pl.BlockSpec((batch, 1, SUBLANES, WIDTH), lambda c: (0, c, 0, 0))],
        out_specs=(blk, pl.BlockSpec((batch, HEADS, HEAD_DIM, HEAD_DIM), lambda c: (0, 0, 0, 0))),
        out_shape=(jax.ShapeDtypeStruct((batch, seq, WIDTH), F32),
                   jax.ShapeDtypeStruct((batch, HEADS, HEAD_DIM, HEAD_DIM), F32)),
        scratch_shapes=[pltpu.VMEM((batch, HEADS // GROUP, GROUP_W, GROUP_W), F32)],
        compiler_params=pltpu.CompilerParams(dimension_semantics=("arbitrary",), vmem_limit_bytes=VMEM_LIMIT),
        name="gdn_scan",
    )(as3(u), as3(w), as3(qe), as3(kdec), as3(qkw), egl.reshape(batch, nc, SUBLANES, WIDTH))
    return o.reshape(batch * seq, WIDTH), s_fin


def _post_kernel(x_ref, fox_ref, gdn_ref, sgg_ref, sga_ref, sgb_ref,
                 wpa_ref, wpb_ref, wout_ref, wup_ref, wdown_ref,
                 ng_ref, gpost_ref, gpre2_ref, gpost2_ref, hsum_ref, hexp_ref, y_ref):
    ya = _dot(fox_ref[...], wpa_ref[...])
    o = gdn_ref[...]
    ms = _dot_sel(o * o, hsum_ref[...], terms=2) * (1.0 / HEAD_DIM)
    o = o * _dot_sel(lax.rsqrt(ms + EPS), hexp_ref[...], terms=2) * ng_ref[...] * sgg_ref[...].astype(F32)
    yb = _dot(o.astype(BF16), wpb_ref[...])
    m = sga_ref[...].astype(F32) * ya + sgb_ref[...].astype(F32) * yb
    mix = _dot(m.astype(BF16), wout_ref[...])
    y1 = x_ref[...] + _rms(mix, gpost_ref[...])
    h2 = _rms(y1, gpre2_ref[...]).astype(BF16)
    acc = jnp.zeros(y1.shape, F32)
    for c in range(D_FF // FF_TILE):
        u = jnp.maximum(_dot(h2, wup_ref[:, c * FF_TILE:(c + 1) * FF_TILE]), 0.0)
        acc = acc + _dot((u * u).astype(BF16), wdown_ref[c * FF_TILE:(c + 1) * FF_TILE, :])
    y_ref[...] = y1 + _rms(acc, gpost2_ref[...])


def _post(x, fox, gdn_o, sgg, sga, sgb, w_pa, w_pb, w_out, w_up, w_down, ng, gpost, gpre2, gpost2):
    t_total = x.shape[0]
    tm = min(POST_TILE, t_total)
    assert t_total % tm == 0
    row = lambda w: pl.BlockSpec((tm, w), lambda i: (i, 0))
    head_of = np.arange(WIDTH) // HEAD_DIM
    hsum = jnp.asarray(head_of[:, None] == (np.arange(LANES)[None, :] - L_G), BF16)
    hexp = jnp.asarray((np.arange(LANES)[:, None] - L_G) == head_of[None, :], BF16)
    return pl.pallas_call(
        _post_kernel,
        grid=(t_total // tm,),
        in_specs=[row(D_MODEL), row(WIDTH), row(WIDTH), row(WIDTH), row(D_MODEL), row(D_MODEL),
                  _const_spec((WIDTH, D_MODEL)), _const_spec((WIDTH, D_MODEL)), _const_spec((D_MODEL, D_MODEL)),
                  _const_spec((D_MODEL, D_FF)), _const_spec((D_FF, D_MODEL)),
                  _const_spec((1, WIDTH)), _const_spec((1, D_MODEL)), _const_spec((1, D_MODEL)),
                  _const_spec((1, D_MODEL)), _const_spec((WIDTH, LANES)), _const_spec((LANES, WIDTH))],
        out_specs=row(D_MODEL),
        out_shape=jax.ShapeDtypeStruct((t_total, D_MODEL), F32),
        compiler_params=pltpu.CompilerParams(dimension_semantics=("arbitrary",), vmem_limit_bytes=VMEM_LIMIT),
        name="post",
    )(x, fox, gdn_o, sgg, sga, sgb, w_pa, w_pb, w_out, w_up, w_down, ng, gpost, gpre2, gpost2, hsum, hexp)


def _rearrange_w_in(w_in):
    o_ff = 3 * WIDTH
    o_gqkv = o_ff + HEADS
    o_ga = o_gqkv + CONV_DIM
    o_gb = o_ga + HEADS
    o_gg = o_gb + HEADS
    o_gate_a = o_gg + WIDTH
    o_gate_b = o_gate_a + D_MODEL
    cols = [w_in[:, 0:o_ff], w_in[:, o_gqkv:o_ga], w_in[:, o_gg:o_gate_a], w_in[:, o_gate_a:o_gate_b],
            w_in[:, o_gate_b:o_gate_b + D_MODEL], w_in[:, o_ff:o_gqkv], w_in[:, o_ga:o_gb], w_in[:, o_gb:o_gg],
            jnp.zeros((D_MODEL, LANES - 3 * HEADS), w_in.dtype)]
    return jnp.concatenate(cols, axis=1).astype(BF16)


def _lane_row(*pieces):
    v = jnp.concatenate([p.astype(F32) for p in pieces])
    return jnp.concatenate([v, jnp.zeros((LANES - v.shape[0],), F32)])[None, :]


def _pad_hist(conv_cache):
    b = conv_cache.shape[0]
    return jnp.concatenate([jnp.zeros((b, SUBLANES - (CONV_WIDTH - 1), CONV_DIM), F32), conv_cache.astype(F32)], axis=1)


def kernel(x_prompt, x_sample, cache_fox_k, cache_fox_v, cache_fox_logf, state_gdn, state_gdn_conv,
           w_in, fox_forget_bias, gdn_conv_w, gdn_a_log, gdn_dt_bias, gdn_norm_g,
           w_proj_fox, w_proj_gdn, w_out, norm_mix_pre, norm_mix_post, norm_mlp_pre, norm_mlp_post,
           w_up, w_down):
    depth = w_in.shape[0]
    bp, sp, _ = x_prompt.shape
    bs, ss, _ = x_sample.shape
    past = cache_fox_k.shape[2]
    y_p = x_prompt.reshape(bp * sp, D_MODEL)
    y_s = x_sample.reshape(bs * ss, D_MODEL)
    st_p, st_s = [], []
    for l in range(depth):
        w_r = _rearrange_w_in(w_in[l])
        padd = _lane_row(fox_forget_bias[l], gdn_dt_bias[l])
        palog = _lane_row(jnp.zeros((HEADS,), F32), gdn_a_log[l])
        gain = norm_mix_pre[l][None, :]
        post_w = (w_proj_fox[l].astype(BF16), w_proj_gdn[l].astype(BF16), w_out[l].astype(BF16),
                  w_up[l].astype(BF16), w_down[l].astype(BF16),
                  jnp.tile(gdn_norm_g[l], HEADS)[None, :], norm_mix_post[l][None, :],
                  norm_mlp_pre[l][None, :], norm_mlp_post[l][None, :])

        q, k, vt, fk, fv, gqkv, sgg, sga, sgb, small, cum = _in_proj(y_p, gain, w_r, padd, palog, sp)
        fox = _fox_prompt(q, k, vt, cum, bp, sp)
        gdn_o, s_fin = _gdn_long(gqkv, small, gdn_conv_w[l], bp, sp, GDN_CHUNK)
        y_p_new = _post(y_p, fox, gdn_o, sgg, sga, sgb, *post_w)
        st_p.append((fk.reshape(bp, sp, HEADS, HEAD_DIM), fv.reshape(bp, sp, HEADS, HEAD_DIM),
                     small[:, L_LOGF:L_LOGF + HEADS].reshape(bp, sp, HEADS), s_fin,
                     gqkv.reshape(bp, sp, CONV_DIM)[:, sp - (CONV_WIDTH - 1):, :]))
        y_p = y_p_new

        q, k, vt, fk, fv, gqkv, sgg, sga, sgb, small, cum = _in_proj(y_s, gain, w_r, padd, palog, ss)
        fox = _fox_sample(q, fk.reshape(bs * ss, WIDTH), fv.reshape(bs * ss, WIDTH), small,
                          cache_fox_k[l], cache_fox_v[l], cache_fox_logf[l], bs, ss, past)
        gdn_o, s_new = _gdn(gqkv, small, _pad_hist(state_gdn_conv[l]), state_gdn[l].astype(F32),
                            gdn_conv_w[l], bs, ss, ss)
        y_s_new = _post(y_s, fox, gdn_o, sgg, sga, sgb, *post_w)
        conv_ext = jnp.concatenate([state_gdn_conv[l].astype(F32), gqkv.reshape(bs, ss, CONV_DIM)], axis=1)
        st_s.append((fk.reshape(bs, ss, HEADS, HEAD_DIM), fv.reshape(bs, ss, HEADS, HEAD_DIM),
                     small[:, L_LOGF:L_LOGF + HEADS].reshape(bs, ss, HEADS), s_new,
                     conv_ext[:, conv_ext.shape[1] - (CONV_WIDTH - 1):, :]))
        y_s = y_s_new

    fk_p, fv_p, lf_p, sg_p, cv_p = [jnp.stack(a) for a in zip(*st_p)]
    fk_s, fv_s, lf_s, sg_s, cv_s = [jnp.stack(a) for a in zip(*st_s)]
    return (y_p.reshape(bp, sp, D_MODEL), y_s.reshape(bs, ss, D_MODEL),
            fk_p, fv_p, lf_p, sg_p, cv_p, fk_s, fv_s, lf_s, sg_s, cv_s)
```

```python
import functools

import jax
import jax.numpy as jnp
import numpy as np
from jax import lax
from jax.experimental import pallas as pl
from jax.experimental.pallas import tpu as pltpu

F32 = jnp.float32
BF16 = jnp.bfloat16

D_MODEL = 1024
HEADS = 8
HEAD_DIM = 64
WIDTH = HEADS * HEAD_DIM
CONV_DIM = 3 * WIDTH
CONV_WIDTH = 4
D_FF = 4 * D_MODEL
EPS = 1e-6
LOG2E = 1.4426950408889634

LANES = 128
SUBLANES = 8
TOKEN_TILE = 256
POST_TILE = 512
ATTN_TILE = 256
ATTN_HEADS = 8
KEY_TILE = 512
FF_TILE = 1024
GDN_CHUNK = 64
VMEM_LIMIT = 60 * 1024 * 1024

C_Q, C_K, C_V, C_GQKV, C_GG, C_GA, C_GB, C_SMALL = 0, 512, 1024, 1536, 3072, 3584, 4608, 5632
IN_COLS = C_SMALL + LANES
L_LOGF, L_G, L_BETA = 0, 8, 16


def _split3(x):
    hi = x.astype(BF16)
    r = x - hi.astype(F32)
    mid = r.astype(BF16)
    lo = (r - mid.astype(F32)).astype(BF16)
    return hi, mid, lo


def _dot(a, b):
    return jnp.dot(a, b, preferred_element_type=F32)


def _dot_nt(a, b):
    return lax.dot_general(a, b, (((1,), (1,)), ((), ())), preferred_element_type=F32)


def _dot_tn(a, b):
    return lax.dot_general(a, b, (((0,), (0,)), ((), ())), preferred_element_type=F32)


def _sel_dot(sel, x, terms=3):
    parts = _split3(x)[:terms]
    n = x.shape[1]
    if n % LANES:
        return sum(_dot(sel, p) for p in parts)
    wide = _dot(sel, jnp.concatenate(parts, axis=1))
    return sum(wide[:, t * n:(t + 1) * n] for t in range(terms))


def _dot_sel(x, sel, terms=3):
    parts = _split3(x)[:terms]
    if terms == 1:
        return _dot(parts[0], sel)
    out = _dot(jnp.concatenate(parts[0:2], axis=1), jnp.concatenate([sel, sel], axis=0))
    for p in parts[2:]:
        out = out + _dot(p, sel)
    return out


def _sigmoid(x):
    return 1.0 / (1.0 + jnp.exp(-x))


def _rms(x, gain):
    ms = jnp.mean(x * x, axis=-1, keepdims=True)
    return x * lax.rsqrt(ms + EPS) * gain


def _const_spec(shape):
    nd = len(shape)
    return pl.BlockSpec(shape, lambda *_: (0,) * nd, pipeline_mode=pl.Buffered(1))


def _inproj_kernel(x_ref, gain_ref, w_ref, padd_ref, palog_ref, tri_ref,
                   q_ref, k_ref, vt_ref, fk_ref, fv_ref, gqkv_ref, sgg_ref, sga_ref, sgb_ref,
                   small_ref, cum_ref, carry_ref, *, tiles_per_seg):
    i = pl.program_id(0)
    h = _rms(x_ref[...], gain_ref[...]).astype(BF16)

    def mm(c0, width):
        return _dot(h, w_ref[:, c0:c0 + width])

    zq = mm(C_Q, WIDTH)
    q_ref[...] = (zq * (LOG2E * HEAD_DIM ** -0.5)).astype(BF16)
    zk = mm(C_K, WIDTH)
    k_ref[...] = zk.astype(BF16)
    zv = mm(C_V, WIDTH)
    fk_ref[...] = zk.reshape(fk_ref.shape)
    fv_ref[...] = zv.reshape(fv_ref.shape)
    vt_ref[0] = zv.T.astype(BF16)
    for c in range(3):
        gqkv_ref[:, c * WIDTH:(c + 1) * WIDTH] = mm(C_GQKV + c * WIDTH, WIDTH)
    zg = mm(C_GG, WIDTH)
    sgg_ref[...] = (zg * _sigmoid(zg)).astype(BF16)
    for c in range(2):
        sga_ref[:, c * WIDTH:(c + 1) * WIDTH] = _sigmoid(mm(C_GA + c * WIDTH, WIDTH)).astype(BF16)
        sgb_ref[:, c * WIDTH:(c + 1) * WIDTH] = _sigmoid(mm(C_GB + c * WIDTH, WIDTH)).astype(BF16)

    zs = mm(C_SMALL, LANES)
    t = zs + padd_ref[...]
    l1p = jnp.log1p(jnp.exp(-jnp.abs(t)))
    logf = jnp.minimum(t, 0.0) - l1p
    g = -jnp.exp(palog_ref[...]) * (jnp.maximum(t, 0.0) + l1p)
    beta = _sigmoid(zs)
    lane = lax.broadcasted_iota(jnp.int32, zs.shape, 1)
    small = jnp.where(lane < L_G, logf, jnp.where(lane < L_BETA, g, jnp.where(lane < L_BETA + HEADS, beta, 0.0)))
    small_ref[...] = small

    cum = _sel_dot(tri_ref[...], small)
    if tiles_per_seg > 1:
        @pl.when(i % tiles_per_seg == 0)
        def _():
            carry_ref[...] = jnp.zeros_like(carry_ref)
        cum = cum + carry_ref[0:1, :]
        carry_ref[...] = jnp.broadcast_to(cum[-1:, :], carry_ref.shape)
    cum_ref[...] = cum


def _in_proj(x, gain, w_r, padd, palog, seg_len):
    t_total = x.shape[0]
    tm = TOKEN_TILE
    assert t_total % tm == 0
    nt = t_total // tm
    if seg_len >= tm:
        assert seg_len % tm == 0
        tiles_per_seg = seg_len // tm
        r = np.arange(tm)
        tri = (r[None, :] <= r[:, None])
    else:
        assert tm % seg_len == 0
        tiles_per_seg = 1
        r = np.arange(tm)
        tri = (r[None, :] <= r[:, None]) & ((r[None, :] // seg_len) == (r[:, None] // seg_len))
    tri = jnp.asarray(tri, BF16)

    row = lambda w: pl.BlockSpec((tm, w), lambda i: (i, 0))
    out_shape = (
        jax.ShapeDtypeStruct((t_total, WIDTH), BF16),
        jax.ShapeDtypeStruct((t_total, WIDTH), BF16),
        jax.ShapeDtypeStruct((nt, WIDTH, tm), BF16),
        jax.ShapeDtypeStruct((t_total, HEADS, HEAD_DIM), F32),
        jax.ShapeDtypeStruct((t_total, HEADS, HEAD_DIM), F32),
        jax.ShapeDtypeStruct((t_total, CONV_DIM), F32),
        jax.ShapeDtypeStruct((t_total, WIDTH), BF16),
        jax.ShapeDtypeStruct((t_total, D_MODEL), BF16),
        jax.ShapeDtypeStruct((t_total, D_MODEL), BF16),
        jax.ShapeDtypeStruct((t_total, LANES), F32),
        jax.ShapeDtypeStruct((t_total, LANES), F32),
    )
    per_head = pl.BlockSpec((tm, HEADS, HEAD_DIM), lambda i: (i, 0, 0))
    out_specs = (row(WIDTH), row(WIDTH), pl.BlockSpec((1, WIDTH, tm), lambda i: (i, 0, 0)),
                 per_head, per_head, row(CONV_DIM), row(WIDTH), row(D_MODEL), row(D_MODEL),
                 row(LANES), row(LANES))
    return pl.pallas_call(
        functools.partial(_inproj_kernel, tiles_per_seg=tiles_per_seg),
        grid=(nt,),
        in_specs=[row(D_MODEL), _const_spec((1, D_MODEL)), _const_spec((D_MODEL, IN_COLS)),
                  _const_spec((1, LANES)), _const_spec((1, LANES)), _const_spec((tm, tm))],
        out_specs=out_specs,
        out_shape=out_shape,
        scratch_shapes=[pltpu.VMEM((SUBLANES, LANES), F32)],
        compiler_params=pltpu.CompilerParams(dimension_semantics=("arbitrary",), vmem_limit_bytes=VMEM_LIMIT),
        name="in_proj",
    )(x, gain, w_r, padd, palog, tri)


def _fox_prompt_kernel(q_ref, k_ref, vt_ref, cum_ref, o_ref, bias_ref, s0_ref, s1_ref, p0_ref, p1_ref,
                       *, seq, tq, tk, nh):
    grp = pl.program_id(1)
    qi = pl.program_id(2)
    rows = 512
    vt_tile = vt_ref.shape[2]
    sub = tk // vt_tile
    heads = range(nh)
    s_slots = (s0_ref, s1_ref)
    p_slots = (p0_ref, p1_ref)

    @pl.when(qi == 0)
    def _prep():
        def fill(r, carry):
            c = cum_ref[pl.ds(r * rows, rows), :]
            lane = lax.broadcasted_iota(jnp.int32, c.shape, 1)
            slab = jnp.zeros(c.shape, F32)
            for hh in heads:
                col = jnp.sum(jnp.where(lane == nh * grp + hh, c, 0.0), axis=-1, keepdims=True) * (-LOG2E)
                for t, part in enumerate(_split3(col)):
                    slab = jnp.where(lane == 3 * hh + t, part.astype(F32), slab)
            bias_ref[pl.ds(r * rows, rows), :] = slab.astype(BF16)
            return carry
        lax.fori_loop(0, seq // rows, fill, 0)

    lane_q = lax.broadcasted_iota(jnp.int32, (tq, LANES), 1)
    qh = []
    for hh in heads:
        q2 = q_ref[:, (hh // 2) * LANES:(hh // 2 + 1) * LANES]
        half = hh % 2
        q_m = jnp.where((lane_q >= HEAD_DIM * half) & (lane_q < HEAD_DIM * (half + 1)), q2, jnp.zeros_like(q2))
        ones = jnp.where((lane_q >= 3 * hh) & (lane_q < 3 * hh + 3), 1.0, 0.0)
        qh.append(jnp.concatenate([q_m.astype(F32), ones], axis=1).T.astype(BF16))

    def scores_to(slot, j):
        bias = bias_ref[pl.ds(j * tk, tk), :]
        kjs = [jnp.concatenate([k_ref[pl.ds(j * tk, tk), sl * LANES:(sl + 1) * LANES], bias], axis=1)
               for sl in range(nh // 2)]
        s_ts = [_dot(kjs[hh // 2], qh[hh]) for hh in heads]
        for hh in heads:
            s_slots[slot][hh] = s_ts[hh]
        return tuple(jnp.max(s_t, axis=0, keepdims=True) for s_t in s_ts)

    ones_rows = jnp.ones((2 * SUBLANES, vt_tile), BF16)

    def add_values(slot, j, accs, alphas):
        pvs = [sum(_dot(jnp.concatenate([vt_ref[j * sub + t, HEAD_DIM * hh:HEAD_DIM * (hh + 1), :], ones_rows], axis=0),
                        p_slots[slot][hh, t * vt_tile:(t + 1) * vt_tile, :]) for t in range(sub)) for hh in heads]
        return tuple(alphas[hh] * accs[hh] + pvs[hh] for hh in heads)

    def rescale(ms, bms):
        m_new = tuple(jnp.maximum(ms[hh], bms[hh]) for hh in heads)
        return m_new, tuple(jnp.exp2(ms[hh] - m_new[hh]) for hh in heads)

    def probabilities(slot, hh, m_new, visible_from):
        s_t = s_slots[slot][hh]
        if visible_from is not None:
            s_t = jnp.where(visible_from, s_t, -1e30)
        p_slots[slot][hh] = jnp.exp2(s_t - m_new).astype(BF16)

    def trip(j, rd, carry):
        ms, accs, a_prev, bms = carry
        wr = 1 - rd
        accs = add_values(wr, jnp.maximum(j - 1, 0), accs, a_prev)
        m_new, alphas = rescale(ms, bms)
        for hh in heads:
            probabilities(rd, hh, m_new[hh], None)
        bm_next = scores_to(wr, j + 1)
        return m_new, accs, alphas, bm_next

    n_full = (qi * tq) // tk

    def last(rd, carry):
        ms, accs, a_prev, _ = carry
        accs = add_values(1 - rd, jnp.maximum(n_full - 1, 0), accs, a_prev)
        kr = lax.broadcasted_iota(jnp.int32, (tk, tq), 0)
        qc = lax.broadcasted_iota(jnp.int32, (tk, tq), 1)
        visible = kr <= qc + (qi * tq - n_full * tk)
        bms = tuple(jnp.max(jnp.where(visible, s_slots[rd][hh], -1e30), axis=0, keepdims=True) for hh in heads)
        m_new, alphas = rescale(ms, bms)
        for hh in heads:
            probabilities(rd, hh, m_new[hh], visible)
        accs = add_values(rd, n_full, accs, alphas)
        return jnp.concatenate([accs[hh][0:HEAD_DIM] / accs[hh][HEAD_DIM:HEAD_DIM + 1] for hh in heads], axis=0)

    each = lambda f: tuple(f() for _ in heads)
    p1_ref[...] = jnp.zeros(p1_ref.shape, BF16)
    init = (each(lambda: jnp.full((1, tq), -1e30, F32)), each(lambda: jnp.zeros((HEAD_DIM + 2 * SUBLANES, tq), F32)),
            each(lambda: jnp.ones((1, tq), F32)), scores_to(0, 0))
    carry = lax.fori_loop(
        0, n_full, lambda j, c: lax.cond(j % 2 == 0, lambda c: trip(j, 0, c), lambda c: trip(j, 1, c), c), init)
    o_t = lax.cond(n_full % 2 == 0, lambda c: last(0, c), lambda c: last(1, c), carry)
    o_ref[...] = o_t.T.astype(BF16)


def _fox_prompt(q, k, vt, cum, batch, seq):
    tq, tk, nh = ATTN_TILE, KEY_TILE, ATTN_HEADS
    assert seq % tk == 0 and tk % tq == 0 and tk % TOKEN_TILE == 0 and HEADS % nh == 0 and nh % 2 == 0
    nq = seq // tq
    width = nh * HEAD_DIM
    return pl.pallas_call(
        functools.partial(_fox_prompt_kernel, seq=seq, tq=tq, tk=tk, nh=nh),
        grid=(batch, HEADS // nh, nq),
        in_specs=[pl.BlockSpec((tq, width), lambda b, g, i: (b * nq + i, g)),
                  pl.BlockSpec((seq, width), lambda b, g, i: (b, g), pipeline_mode=pl.Buffered(1)),
                  pl.BlockSpec((seq // TOKEN_TILE, width, TOKEN_TILE), lambda b, g, i: (b, g, 0),
                               pipeline_mode=pl.Buffered(1)),
                  pl.BlockSpec((seq, LANES), lambda b, g, i: (b, 0), pipeline_mode=pl.Buffered(1))],
        out_specs=pl.BlockSpec((tq, width), lambda b, g, i: (b * nq + i, g)),
        out_shape=jax.ShapeDtypeStruct((batch * seq, WIDTH), BF16),
        scratch_shapes=[pltpu.VMEM((seq, LANES), BF16),
                        pltpu.VMEM((nh, tk, tq), F32), pltpu.VMEM((nh, tk, tq), F32),
                        pltpu.VMEM((nh, tk, tq), BF16), pltpu.VMEM((nh, tk, tq), BF16)],
        compiler_params=pltpu.CompilerParams(dimension_semantics=("arbitrary",) * 3, vmem_limit_bytes=VMEM_LIMIT),
        name="fox_prompt",
    )(q, k, vt, cum)


def _fox_sample_kernel(q_ref, kn_ref, vn_ref, small_ref, kc_ref, vc_ref, lfc_ref, tri_ref, trin_ref, o_ref,
                       *, past, steps):
    blk = tri_ref.shape[0]
    carry = jnp.zeros((1, HEADS), F32)
    cums = []
    for r in range(past // blk):
        c = _sel_dot(tri_ref[...], lfc_ref[0, r * blk:(r + 1) * blk, :]) + carry
        cums.append(c)
        carry = c[-1:, :]
    cum_c = jnp.concatenate(cums, axis=0)
    cum_n = _sel_dot(trin_ref[...], small_ref[:, L_LOGF:L_LOGF + HEADS]) + carry

    q = q_ref[...]
    kc = kc_ref[0].astype(BF16)
    vc = vc_ref[0].astype(BF16)
    kn = kn_ref[...].astype(BF16)
    vn = vn_ref[...].astype(BF16)
    lane_c = lax.broadcasted_iota(jnp.int32, (past, LANES), 1)
    lane_n = lax.broadcasted_iota(jnp.int32, (steps, LANES), 1)
    kr = lax.broadcasted_iota(jnp.int32, (steps, steps), 0)
    qc = lax.broadcasted_iota(jnp.int32, (steps, steps), 1)
    ones_c = jnp.ones((past, LANES), BF16)
    ones_n = jnp.ones((steps, LANES), BF16)
    heads = range(HEADS)
    slab = [slice((hd // 2) * LANES, (hd // 2 + 1) * LANES) for hd in heads]
    in_c = [(lane_c >= HEAD_DIM * (hd % 2)) & (lane_c < HEAD_DIM * (hd % 2 + 1)) for hd in heads]
    in_n = [(lane_n >= HEAD_DIM * (hd % 2)) & (lane_n < HEAD_DIM * (hd % 2 + 1)) for hd in heads]
    qms = [jnp.where(in_n[hd], q[:, slab[hd]], jnp.zeros((steps, LANES), BF16)) for hd in heads]
    s1s = [_dot_nt(kc[:, slab[hd]], qms[hd]) - LOG2E * cum_c[:, hd:hd + 1] for hd in heads]
    s2s = [jnp.where(kr <= qc, _dot_nt(kn[:, slab[hd]], qms[hd]) - LOG2E * cum_n[:, hd:hd + 1], -1e30)
           for hd in heads]
    ms = [jnp.maximum(jnp.max(s1, axis=0, keepdims=True), jnp.max(s2, axis=0, keepdims=True))
          for s1, s2 in zip(s1s, s2s)]
    p1s = [jnp.exp2(s1 - m).astype(BF16) for s1, m in zip(s1s, ms)]
    p2s = [jnp.exp2(s2 - m).astype(BF16) for s2, m in zip(s2s, ms)]
    nums = [_dot_tn(p1s[hd], jnp.where(in_c[hd], vc[:, slab[hd]], jnp.zeros((past, LANES), BF16)))
            + _dot_tn(p2s[hd], jnp.where(in_n[hd], vn[:, slab[hd]], jnp.zeros((steps, LANES), BF16))) for hd in heads]
    dens = [_dot_tn(p1s[hd], ones_c) + _dot_tn(p2s[hd], ones_n) for hd in heads]
    outs = [n / d for n, d in zip(nums, dens)]
    o_ref[...] = jnp.concatenate([outs[2 * pr] + outs[2 * pr + 1] for pr in range(HEADS // 2)], axis=1).astype(BF16)


def _fox_sample(q, kn, vn, small, kc, vc, lfc, batch, steps, past):
    blk = 256
    assert past % blk == 0
    r = np.arange(blk)
    tri = jnp.asarray(r[None, :] <= r[:, None], BF16)
    rn = np.arange(steps)
    trin = jnp.asarray(rn[None, :] <= rn[:, None], BF16)
    row = lambda w: pl.BlockSpec((steps, w), lambda b: (b, 0))
    return pl.pallas_call(
        functools.partial(_fox_sample_kernel, past=past, steps=steps),
        grid=(batch,),
        in_specs=[row(WIDTH), row(WIDTH), row(WIDTH), row(LANES),
                  pl.BlockSpec((1, past, WIDTH), lambda b: (b, 0, 0)),
                  pl.BlockSpec((1, past, WIDTH), lambda b: (b, 0, 0)),
                  pl.BlockSpec((1, past, HEADS), lambda b: (b, 0, 0)),
                  _const_spec((blk, blk)), _const_spec((steps, steps))],
        out_specs=row(WIDTH),
        out_shape=jax.ShapeDtypeStruct((batch * steps, WIDTH), BF16),
        compiler_params=pltpu.CompilerParams(dimension_semantics=("arbitrary",), vmem_limit_bytes=VMEM_LIMIT),
        name="fox_sample",
    )(q, kn, vn, small, kc, vc, lfc, tri, trin)


def _gdn_kernel(gq_ref, small_ref, hist_ref, s0_ref, convw_ref, tri_ref, hsum_ref, hexp_g_ref, hexp_b_ref,
                o_ref, sfin_ref, xbuf, state, *, chunk):
    c_idx = pl.program_id(1)
    n_chunks = pl.num_programs(1)
    hist_rows = SUBLANES

    @pl.when(c_idx == 0)
    def _():
        xbuf[0:hist_rows, :] = hist_ref[0]
        state[...] = s0_ref[0]

    xbuf[hist_rows:hist_rows + chunk, :] = gq_ref[...]
    w = convw_ref[...]
    base = hist_rows - (CONV_WIDTH - 1)
    y = xbuf[base:base + chunk, :] * w[0:1, :]
    for i in range(1, CONV_WIDTH):
        y = y + xbuf[base + i:base + i + chunk, :] * w[i:i + 1, :]
    xbuf[0:hist_rows, :] = xbuf[chunk:chunk + hist_rows, :]
    y = y * _sigmoid(y)

    hsum = hsum_ref[...]
    hexp_g = hexp_g_ref[...]
    hexp_b = hexp_b_ref[...]

    def l2n(x):
        inv = lax.rsqrt(_dot_sel(x * x, hsum, terms=2) + EPS)
        return x * _dot_sel(inv, hexp_g, terms=2)

    q = l2n(y[:, 0:WIDTH]) * (HEAD_DIM ** -0.5)
    k = l2n(y[:, WIDTH:2 * WIDTH])
    v = y[:, 2 * WIDTH:3 * WIDTH]

    small = small_ref[...]
    gc = _sel_dot(tri_ref[...], small)
    gc_x = _dot_sel(gc, hexp_g)
    beta_x = _dot_sel(small, hexp_b, terms=2)
    egc_x = jnp.exp(gc_x)
    glast_x = gc_x[chunk - 1:chunk, :]
    eglast_x = jnp.exp(glast_x)
    kb = k * beta_x
    vb = v * beta_x
    kbe = kb * egc_x
    qe = q * egc_x
    kdec = k * jnp.exp(glast_x - gc_x)

    pad_rows = LANES - chunk
    gc_sq = jnp.concatenate([gc, jnp.zeros((pad_rows, LANES), F32)], axis=0) if pad_rows else gc
    gc_t = gc_sq.T

    ri = lax.broadcasted_iota(jnp.int32, (chunk, chunk), 0)
    ci = lax.broadcasted_iota(jnp.int32, (chunk, chunk), 1)
    eye = jnp.where(ri == ci, 1.0, 0.0).astype(F32)
    n_double = int(np.log2(chunk)) - 1
    assert 2 ** (n_double + 1) == chunk

    heads = range(HEADS)
    sls = [slice(hd * HEAD_DIM, (hd + 1) * HEAD_DIM) for hd in heads]
    decs = [jnp.where(ri >= ci, jnp.exp(gc[:, L_G + hd:L_G + hd + 1] - gc_t[L_G + hd:L_G + hd + 1, 0:chunk]), 0.0)
            for hd in heads]
    k_hs = [k[:, sl].astype(BF16) for sl in sls]
    pws = [-(_dot_nt(kb[:, sl].astype(BF16), k_h) * jnp.where(ri > ci, dec, 0.0))
           for sl, k_h, dec in zip(sls, k_hs, decs)]
    qks = [(_dot_nt(q[:, sl].astype(BF16), k_h) * dec).astype(BF16) for sl, k_h, dec in zip(sls, k_hs, decs)]
    tinvs = [eye + pw for pw in pws]
    for _ in range(n_double):
        pws = [pw.astype(BF16) for pw in pws]
        pws = [_dot(pw, pw) for pw in pws]
        tinvs = [tinv + _dot(tinv.astype(BF16), pw.astype(BF16)) for tinv, pw in zip(tinvs, pws)]
    uws = [_dot(tinv.astype(BF16), jnp.concatenate([vb[:, sl], kbe[:, sl]], axis=1).astype(BF16))
           for tinv, sl in zip(tinvs, sls)]
    s_fs = [state[hd] for hd in heads]
    s_bs = [s_f.astype(BF16) for s_f in s_fs]
    v_news = [(uw[:, 0:HEAD_DIM] - _dot(uw[:, HEAD_DIM:2 * HEAD_DIM].astype(BF16), s_b)).astype(BF16)
              for uw, s_b in zip(uws, s_bs)]
    outs = [_dot(qe[:, sl].astype(BF16), s_b) + _dot(qk, v_new)
            for sl, s_b, qk, v_new in zip(sls, s_bs, qks, v_news)]
    for hd in heads:
        state[hd] = s_fs[hd] * eglast_x[:, sls[hd]] + _dot_tn(kdec[:, sls[hd]].astype(BF16), v_news[hd])
    o_ref[...] = jnp.concatenate(outs, axis=1)

    @pl.when(c_idx == n_chunks - 1)
    def _():
        sfin_ref[0] = state[...]


def _gdn(gqkv, small, hist, s0, conv_w, batch, seq, chunk):
    assert seq % chunk == 0 and chunk % SUBLANES == 0 and chunk <= LANES
    nc = seq // chunk
    r = np.arange(chunk)
    tri = jnp.asarray(r[None, :] <= r[:, None], BF16)
    head_of = np.arange(WIDTH) // HEAD_DIM
    hsum = jnp.asarray(head_of[:, None] == (np.arange(LANES)[None, :] - L_G), BF16)
    hexp_g = jnp.asarray((np.arange(LANES)[:, None] - L_G) == head_of[None, :], BF16)
    hexp_b = jnp.asarray((np.arange(LANES)[:, None] - L_BETA) == head_of[None, :], BF16)
    conv_w8 = jnp.concatenate([conv_w, jnp.zeros((SUBLANES - CONV_WIDTH, CONV_DIM), F32)], axis=0)
    return pl.pallas_call(
        functools.partial(_gdn_kernel, chunk=chunk),
        grid=(batch, nc),
        in_specs=[pl.BlockSpec((chunk, CONV_DIM), lambda b, c: (b * nc + c, 0)),
                  pl.BlockSpec((chunk, LANES), lambda b, c: (b * nc + c, 0)),
                  pl.BlockSpec((1, SUBLANES, CONV_DIM), lambda b, c: (b, 0, 0)),
                  pl.BlockSpec((1, HEADS, HEAD_DIM, HEAD_DIM), lambda b, c: (b, 0, 0, 0)),
                  _const_spec((SUBLANES, CONV_DIM)), _const_spec((chunk, chunk)),
                  _const_spec((WIDTH, LANES)), _const_spec((LANES, WIDTH)), _const_spec((LANES, WIDTH))],
        out_specs=(pl.BlockSpec((chunk, WIDTH), lambda b, c: (b * nc + c, 0)),
                   pl.BlockSpec((1, HEADS, HEAD_DIM, HEAD_DIM), lambda b, c: (b, 0, 0, 0))),
        out_shape=(jax.ShapeDtypeStruct((batch * seq, WIDTH), F32),
                   jax.ShapeDtypeStruct((batch, HEADS, HEAD_DIM, HEAD_DIM), F32)),
        scratch_shapes=[pltpu.VMEM((chunk + SUBLANES, CONV_DIM), F32),
                        pltpu.VMEM((HEADS, HEAD_DIM, HEAD_DIM), F32)],
        compiler_params=pltpu.CompilerParams(dimension_semantics=("arbitrary", "arbitrary"),
                                             vmem_limit_bytes=VMEM_LIMIT),
        name="gdn",
    )(gqkv, small, hist, s0, conv_w8, tri, hsum, hexp_g, hexp_b)


GROUP = 4
GROUP_W = GROUP * HEAD_DIM


def _stack_heads(x, lane):
    return jnp.concatenate([jnp.where(lane == hh, x, jnp.zeros_like(x)) for hh in range(GROUP)], axis=0)


def _fold_heads(x, rows):
    out = x[0:rows]
    for hh in range(1, GROUP):
        out = out + x[hh * rows:(hh + 1) * rows]
    return out


def _gdn_local_kernel(gq_ref, prev_ref, small_ref, convw_ref, tri_ref, hsum_ref, hexp_g_ref, hexp_b_ref, dexp_ref,
                      u_ref, w_ref, qe_ref, kdec_ref, qkw_ref, egl_ref, xbuf, *, chunk):
    i = pl.program_id(1)
    tm = gq_ref.shape[0]
    hist_rows = SUBLANES
    prev = prev_ref[...]
    xbuf[0:hist_rows, :] = jnp.where(i == 0, jnp.zeros_like(prev), prev)
    xbuf[hist_rows:hist_rows + tm, :] = gq_ref[...]
    w = convw_ref[...]
    base = hist_rows - (CONV_WIDTH - 1)
    y = xbuf[base:base + tm, :] * w[0:1, :]
    for t in range(1, CONV_WIDTH):
        y = y + xbuf[base + t:base + t + tm, :] * w[t:t + 1, :]
    y = y * _sigmoid(y)

    hsum = hsum_ref[...]
    hexp_g = hexp_g_ref[...]
    hexp_b = hexp_b_ref[...]

    def l2n(x):
        inv = lax.rsqrt(_dot_sel(x * x, hsum, terms=2) + EPS)
        return x * _dot_sel(inv, hexp_g, terms=2)

    q = l2n(y[:, 0:WIDTH]) * (HEAD_DIM ** -0.5)
    k = l2n(y[:, WIDTH:2 * WIDTH])
    v = y[:, 2 * WIDTH:3 * WIDTH]

    small = small_ref[...]
    gc = _sel_dot(tri_ref[...], small)
    gc_x = _dot_sel(gc, hexp_g)
    beta_x = _dot_sel(small, hexp_b, terms=2)
    n_chunks = tm // chunk
    lasts = [gc_x[(c + 1) * chunk - 1:(c + 1) * chunk, :] for c in range(n_chunks)]
    glast_x = jnp.concatenate([jnp.broadcast_to(r, (chunk, WIDTH)) for r in lasts], axis=0)
    egc_x = jnp.exp(gc_x)
    kb = k * beta_x
    vb = (v * beta_x).astype(BF16)
    kbe = (kb * egc_x).astype(BF16)
    kb = kb.astype(BF16)
    qe_ref[...] = (q * egc_x).astype(BF16)
    kdec_ref[...] = (k * jnp.exp(glast_x - gc_x)).astype(BF16)
    for c in range(n_chunks):
        egl_ref[c] = jnp.broadcast_to(jnp.exp(lasts[c]), (SUBLANES, WIDTH))
    q = q.astype(BF16)
    k = k.astype(BF16)

    rows = GROUP * chunk
    lane = lax.broadcasted_iota(jnp.int32, (chunk, GROUP_W), 1)
    head_of_lane = lane // HEAD_DIM
    ri = lax.broadcasted_iota(jnp.int32, (rows, rows), 0)
    ci = lax.broadcasted_iota(jnp.int32, (rows, rows), 1)
    same = (ri // chunk) == (ci // chunk)
    incl = same & (ri >= ci)
    strict = same & (ri > ci)
    eye = jnp.where(ri == ci, 1.0, 0.0).astype(F32)
    n_double = int(np.log2(chunk)) - 1

    placed = None
    for t, part in enumerate(_split3(gc)):
        term = _dot(part, dexp_ref[t])
        placed = term if placed is None else placed + term
    lane_d = lax.broadcasted_iota(jnp.int32, (chunk, LANES), 1)
    head_d = jnp.where(lane_d < 6 * HEADS, lane_d // 6, -1)
    used = lane_d[0:1, :] < 6 * HEADS
    hi_half = lane_d[0:1, :] % 6 >= 3
    diff_l = (placed[:, 0:LANES] + jnp.where(used & hi_half, 1.0, 0.0)).astype(BF16)
    diff_r = (jnp.where(used & ~hi_half, 1.0, 0.0) - placed[:, LANES:2 * LANES]).astype(BF16)

    probs = [(slice(c * chunk, (c + 1) * chunk), slice(g * GROUP_W, (g + 1) * GROUP_W))
             for c in range(n_chunks) for g in range(HEADS // GROUP)]
    stack = lambda x, pr: _stack_heads(x[pr[0], pr[1]], head_of_lane)
    stack_d = lambda x, c, g: jnp.concatenate(
        [jnp.where(head_d == GROUP * g + hh, x[c * chunk:(c + 1) * chunk], jnp.zeros((chunk, LANES), BF16))
         for hh in range(GROUP)], axis=0)
    diffs = [_dot_nt(stack_d(diff_l, c, g), stack_d(diff_r, c, g))
             for c in range(n_chunks) for g in range(HEADS // GROUP)]
    aqs = [_dot_nt(jnp.concatenate([stack(kb, pr), stack(q, pr)], axis=0), stack(k, pr)) for pr in probs]
    decays = [jnp.exp(jnp.where(incl, d, -1e30)) for d in diffs]
    for pr, aq, decay in zip(probs, aqs, decays):
        qkw_ref[pr[0], pr[1]] = _fold_heads(aq[rows:2 * rows] * decay, chunk).astype(BF16)
    pws = [-(aq[0:rows] * jnp.where(strict, decay, 0.0)) for aq, decay in zip(aqs, decays)]
    tinvs = [eye + pw for pw in pws]
    pws = [pw.astype(BF16) for pw in pws]
    pws = [_dot(pw, pw).astype(BF16) for pw in pws]
    for it in range(n_double):
        if it < n_double - 1:
            both = [_dot(jnp.concatenate([pw, tinv.astype(BF16)], axis=0), pw) for pw, tinv in zip(pws, tinvs)]
            pws = [b[0:rows].astype(BF16) for b in both]
            tinvs = [tinv + b[rows:2 * rows] for tinv, b in zip(tinvs, both)]
        else:
            tinvs = [tinv + _dot(tinv.astype(BF16), pw) for pw, tinv in zip(pws, tinvs)]
    uws = [_dot(tinv.astype(BF16), jnp.concatenate([stack(vb, pr), stack(kbe, pr)], axis=1))
           for pr, tinv in zip(probs, tinvs)]
    for pr, uw in zip(probs, uws):
        u_ref[pr[0], pr[1]] = _fold_heads(uw[:, 0:GROUP_W], chunk).astype(BF16)
        w_ref[pr[0], pr[1]] = _fold_heads(uw[:, GROUP_W:2 * GROUP_W], chunk).astype(BF16)


def _gdn_scan_kernel(u_ref, w_ref, qe_ref, kdec_ref, qkw_ref, egl_ref, o_ref, sfin_ref, state, *, chunk):
    c_idx = pl.program_id(0)
    batch = u_ref.shape[0]

    @pl.when(c_idx == 0)
    def _():
        state[...] = jnp.zeros_like(state)

    rows = GROUP * chunk
    lane = lax.broadcasted_iota(jnp.int32, (chunk, GROUP_W), 1) // HEAD_DIM
    ri = lax.broadcasted_iota(jnp.int32, (GROUP_W, GROUP_W), 0)
    ci = lax.broadcasted_iota(jnp.int32, (GROUP_W, GROUP_W), 1)
    same = (ri // HEAD_DIM) == (ci // HEAD_DIM)
    probs = [(b, g, slice(g * GROUP_W, (g + 1) * GROUP_W)) for b in range(batch) for g in range(HEADS // GROUP)]
    s_fs = [state[b, g] for b, g, _ in probs]
    s_bs = [s_f.astype(BF16) for s_f in s_fs]
    v_news = [(u_ref[b, :, ls].astype(F32) - _dot(w_ref[b, :, ls], s_b)).astype(BF16)
              for (b, _, ls), s_b in zip(probs, s_bs)]
    for (b, _, ls), s_b, v_new in zip(probs, s_bs, v_news):
        o_ref[b, :, ls] = _dot(jnp.concatenate([qe_ref[b, :, ls], qkw_ref[b, :, ls]], axis=1),
                               jnp.concatenate([s_b, _stack_heads(v_new, lane)], axis=0))
    kvs = [_dot_tn(kdec_ref[b, :, ls], v_new) for (b, _, ls), v_new in zip(probs, v_news)]
    for (b, g, ls), s_f, kv in zip(probs, s_fs, kvs):
        state[b, g] = s_f * egl_ref[b, 0, 0:1, ls] + jnp.where(same, kv, 0.0)

    @pl.when(c_idx == pl.num_programs(0) - 1)
    def _():
        for b in range(batch):
            for hd in range(HEADS):
                g, hh = divmod(hd, GROUP)
                sl = slice(hh * HEAD_DIM, (hh + 1) * HEAD_DIM)
                sfin_ref[b, hd] = state[b, g, sl, sl]


def _gdn_long(gqkv, small, conv_w, batch, seq, chunk):
    tm = TOKEN_TILE
    assert seq % tm == 0 and tm % chunk == 0 and GROUP * chunk == GROUP_W
    nt = seq // tm
    nc = seq // chunk
    r = np.arange(tm)
    tri = jnp.asarray((r[None, :] <= r[:, None]) & ((r[None, :] // chunk) == (r[:, None] // chunk)), BF16)
    head_of = np.arange(WIDTH) // HEAD_DIM
    hsum = jnp.asarray(head_of[:, None] == (np.arange(LANES)[None, :] - L_G), BF16)
    hexp_g = jnp.asarray((np.arange(LANES)[:, None] - L_G) == head_of[None, :], BF16)
    hexp_b = jnp.asarray((np.arange(LANES)[:, None] - L_BETA) == head_of[None, :], BF16)
    conv_w8 = jnp.concatenate([conv_w, jnp.zeros((SUBLANES - CONV_WIDTH, CONV_DIM), F32)], axis=0)
    dexp = np.zeros((3, LANES, 2 * LANES), np.float32)
    for t in range(3):
        for hd in range(HEADS):
            dexp[t, L_G + hd, 6 * hd + t] = 1.0
            dexp[t, L_G + hd, LANES + 6 * hd + 3 + t] = 1.0
    dexp = jnp.asarray(dexp, BF16)
    row = lambda w: pl.BlockSpec((tm, w), lambda b, i: (b * nt + i, 0))
    per_tile = tm // chunk
    tok = jax.ShapeDtypeStruct((batch * seq, WIDTH), BF16)
    u, w, qe, kdec, qkw, egl = pl.pallas_call(
        functools.partial(_gdn_local_kernel, chunk=chunk),
        grid=(batch, nt),
        in_specs=[row(CONV_DIM),
                  pl.BlockSpec((SUBLANES, CONV_DIM),
                               lambda b, i: (jnp.maximum((b * nt + i) * (tm // SUBLANES) - 1, 0), 0)),
                  row(LANES), _const_spec((SUBLANES, CONV_DIM)), _const_spec((tm, tm)),
                  _const_spec((WIDTH, LANES)), _const_spec((LANES, WIDTH)), _const_spec((LANES, WIDTH)),
                  _const_spec((3, LANES, 2 * LANES))],
        out_specs=(row(WIDTH), row(WIDTH), row(WIDTH), row(WIDTH), row(WIDTH),
                   pl.BlockSpec((per_tile, SUBLANES, WIDTH), lambda b, i: (b * nt + i, 0, 0))),
        out_shape=(tok, tok, tok, tok, tok, jax.ShapeDtypeStruct((batch * nc, SUBLANES, WIDTH), F32)),
        scratch_shapes=[pltpu.VMEM((tm + SUBLANES, CONV_DIM), F32)],
        compiler_params=pltpu.CompilerParams(dimension_semantics=("arbitrary", "arbitrary"),
                                             vmem_limit_bytes=VMEM_LIMIT),
        name="gdn_local",
    )(gqkv, gqkv, small, conv_w8, tri, hsum, hexp_g, hexp_b, dexp)

    blk = pl.BlockSpec((batch, chunk, WIDTH), lambda c: (0, c, 0))
    as3 = lambda a: a.reshape(batch, seq, WIDTH)
    o, s_fin = pl.pallas_call(
        functools.partial(_gdn_scan_kernel, chunk=chunk),
        grid=(nc,),
        in_specs=[blk, blk, blk, blk, blk,
                  pl.BlockSpec((batch, 1, SUBLANES, WIDTH), lambda c: (0, c, 0, 0))],
        out_specs=(blk, pl.BlockSpec((batch, HEADS, HEAD_DIM, HEAD_DIM), lambda c: (0, 0, 0, 0))),
        out_shape=(jax.ShapeDtypeStruct((batch, seq, WIDTH), F32),
                   jax.ShapeDtypeStruct((batch, HEADS, HEAD_DIM, HEAD_DIM), F32)),
        scratch_shapes=[pltpu.VMEM((batch, HEADS // GROUP, GROUP_W, GROUP_W), F32)],
        compiler_params=pltpu.CompilerParams(dimension_semantics=("arbitrary",), vmem_limit_bytes=VMEM_LIMIT),
        name="gdn_scan",
    )(as3(u), as3(w), as3(qe), as3(kdec), as3(qkw), egl.reshape(batch, nc, SUBLANES, WIDTH))
    return o.reshape(batch * seq, WIDTH), s_fin


def _post_kernel(x_ref, fox_ref, gdn_ref, sgg_ref, sga_ref, sgb_ref,
                 wpa_ref, wpb_ref, wout_ref, wup_ref, wdown_ref,
                 ng_ref, gpost_ref, gpre2_ref, gpost2_ref, hsum_ref, hexp_ref, y_ref):
    ya = _dot(fox_ref[...], wpa_ref[...])
    o = gdn_ref[...]
    ms = _dot_sel(o * o, hsum_ref[...], terms=2) * (1.0 / HEAD_DIM)
    o = o * _dot_sel(lax.rsqrt(ms + EPS), hexp_ref[...], terms=2) * ng_ref[...] * sgg_ref[...].astype(F32)
    yb = _dot(o.astype(BF16), wpb_ref[...])
    m = sga_ref[...].astype(F32) * ya + sgb_ref[...].astype(F32) * yb
    mix = _dot(m.astype(BF16), wout_ref[...])
    y1 = x_ref[...] + _rms(mix, gpost_ref[...])
    h2 = _rms(y1, gpre2_ref[...]).astype(BF16)
    acc = jnp.zeros(y1.shape, F32)
    for c in range(D_FF // FF_TILE):
        u = jnp.maximum(_dot(h2, wup_ref[:, c * FF_TILE:(c + 1) * FF_TILE]), 0.0)
        acc = acc + _dot((u * u).astype(BF16), wdown_ref[c * FF_TILE:(c + 1) * FF_TILE, :])
    y_ref[...] = y1 + _rms(acc, gpost2_ref[...])


def _post(x, fox, gdn_o, sgg, sga, sgb, w_pa, w_pb, w_out, w_up, w_down, ng, gpost, gpre2, gpost2):
    t_total = x.shape[0]
    tm = min(POST_TILE, t_total)
    assert t_total % tm == 0
    row = lambda w: pl.BlockSpec((tm, w), lambda i: (i, 0))
    head_of = np.arange(WIDTH) // HEAD_DIM
    hsum = jnp.asarray(head_of[:, None] == (np.arange(LANES)[None, :] - L_G), BF16)
    hexp = jnp.asarray((np.arange(LANES)[:, None] - L_G) == head_of[None, :], BF16)
    return pl.pallas_call(
        _post_kernel,
        grid=(t_total // tm,),
        in_specs=[row(D_MODEL), row(WIDTH), row(WIDTH), row(WIDTH), row(D_MODEL), row(D_MODEL),
                  _const_spec((WIDTH, D_MODEL)), _const_spec((WIDTH, D_MODEL)), _const_spec((D_MODEL, D_MODEL)),
                  _const_spec((D_MODEL, D_FF)), _const_spec((D_FF, D_MODEL)),
                  _const_spec((1, WIDTH)), _const_spec((1, D_MODEL)), _const_spec((1, D_MODEL)),
                  _const_spec((1, D_MODEL)), _const_spec((WIDTH, LANES)), _const_spec((LANES, WIDTH))],
        out_specs=row(D_MODEL),
        out_shape=jax.ShapeDtypeStruct((t_total, D_MODEL), F32),
        compiler_params=pltpu.CompilerParams(dimension_semantics=("arbitrary",), vmem_limit_bytes=VMEM_LIMIT),
        name="post",
    )(x, fox, gdn_o, sgg, sga, sgb, w_pa, w_pb, w_out, w_up, w_down, ng, gpost, gpre2, gpost2, hsum, hexp)


def _rearrange_w_in(w_in):
    o_ff = 3 * WIDTH
    o_gqkv = o_ff + HEADS
    o_ga = o_gqkv + CONV_DIM
    o_gb = o_ga + HEADS
    o_gg = o_gb + HEADS
    o_gate_a = o_gg + WIDTH
    o_gate_b = o_gate_a + D_MODEL
    cols = [w_in[:, 0:o_ff], w_in[:, o_gqkv:o_ga], w_in[:, o_gg:o_gate_a], w_in[:, o_gate_a:o_gate_b],
            w_in[:, o_gate_b:o_gate_b + D_MODEL], w_in[:, o_ff:o_gqkv], w_in[:, o_ga:o_gb], w_in[:, o_gb:o_gg],
            jnp.zeros((D_MODEL, LANES - 3 * HEADS), w_in.dtype)]
    return jnp.concatenate(cols, axis=1).astype(BF16)


def _lane_row(*pieces):
    v = jnp.concatenate([p.astype(F32) for p in pieces])
    return jnp.concatenate([v, jnp.zeros((LANES - v.shape[0],), F32)])[None, :]


def _pad_hist(conv_cache):
    b = conv_cache.shape[0]
    return jnp.concatenate([jnp.zeros((b, SUBLANES - (CONV_WIDTH - 1), CONV_DIM), F32), conv_cache.astype(F32)], axis=1)


def kernel(x_prompt, x_sample, cache_fox_k, cache_fox_v, cache_fox_logf, state_gdn, state_gdn_conv,
           w_in, fox_forget_bias, gdn_conv_w, gdn_a_log, gdn_dt_bias, gdn_norm_g,
           w_proj_fox, w_proj_gdn, w_out, norm_mix_pre, norm_mix_post, norm_mlp_pre, norm_mlp_post,
           w_up, w_down):
    depth = w_in.shape[0]
    bp, sp, _ = x_prompt.shape
    bs, ss, _ = x_sample.shape
    past = cache_fox_k.shape[2]
    y_p = x_prompt.reshape(bp * sp, D_MODEL)
    y_s = x_sample.reshape(bs * ss, D_MODEL)
    st_p, st_s = [], []
    for l in range(depth):
        w_r = _rearrange_w_in(w_in[l])
        padd = _lane_row(fox_forget_bias[l], gdn_dt_bias[l])
        palog = _lane_row(jnp.zeros((HEADS,), F32), gdn_a_log[l])
        gain = norm_mix_pre[l][None, :]
        post_w = (w_proj_fox[l].astype(BF16), w_proj_gdn[l].astype(BF16), w_out[l].astype(BF16),
                  w_up[l].astype(BF16), w_down[l].astype(BF16),
                  jnp.tile(gdn_norm_g[l], HEADS)[None, :], norm_mix_post[l][None, :],
                  norm_mlp_pre[l][None, :], norm_mlp_post[l][None, :])

        q, k, vt, fk, fv, gqkv, sgg, sga, sgb, small, cum = _in_proj(y_p, gain, w_r, padd, palog, sp)
        fox = _fox_prompt(q, k, vt, cum, bp, sp)
        gdn_o, s_fin = _gdn_long(gqkv, small, gdn_conv_w[l], bp, sp, GDN_CHUNK)
        y_p_new = _post(y_p, fox, gdn_o, sgg, sga, sgb, *post_w)
        st_p.append((fk.reshape(bp, sp, HEADS, HEAD_DIM), fv.reshape(bp, sp, HEADS, HEAD_DIM),
                     small[:, L_LOGF:L_LOGF + HEADS].reshape(bp, sp, HEADS), s_fin,
                     gqkv.reshape(bp, sp, CONV_DIM)[:, sp - (CONV_WIDTH - 1):, :]))
        y_p = y_p_new

        q, k, vt, fk, fv, gqkv, sgg, sga, sgb, small, cum = _in_proj(y_s, gain, w_r, padd, palog, ss)
        fox = _fox_sample(q, fk.reshape(bs * ss, WIDTH), fv.reshape(bs * ss, WIDTH), small,
                          cache_fox_k[l].reshape(bs, past, WIDTH), cache_fox_v[l].reshape(bs, past, WIDTH),
                          cache_fox_logf[l], bs, ss, past)
        gdn_o, s_new = _gdn(gqkv, small, _pad_hist(state_gdn_conv[l]), state_gdn[l].astype(F32),
                            gdn_conv_w[l], bs, ss, ss)
        y_s_new = _post(y_s, fox, gdn_o, sgg, sga, sgb, *post_w)
        conv_ext = jnp.concatenate([state_gdn_conv[l].astype(F32), gqkv.reshape(bs, ss, CONV_DIM)], axis=1)
        st_s.append((fk.reshape(bs, ss, HEADS, HEAD_DIM), fv.reshape(bs, ss, HEADS, HEAD_DIM),
                     small[:, L_LOGF:L_LOGF + HEADS].reshape(bs, ss, HEADS), s_new,
                     conv_ext[:, conv_ext.shape[1] - (CONV_WIDTH - 1):, :]))
        y_s = y_s_new

    fk_p, fv_p, lf_p, sg_p, cv_p = [jnp.stack(a) for a in zip(*st_p)]
    fk_s, fv_s, lf_s, sg_s, cv_s = [jnp.stack(a) for a in zip(*st_s)]
    return (y_p.reshape(bp, sp, D_MODEL), y_s.reshape(bs, ss, D_MODEL),
            fk_p, fv_p, lf_p, sg_p, cv_p, fk_s, fv_s, lf_s, sg_s, cv_s)
```

```python
import functools

import jax
import jax.numpy as jnp
import numpy as np
from jax import lax
from jax.experimental import pallas as pl
from jax.experimental.pallas import tpu as pltpu

F32 = jnp.float32
BF16 = jnp.bfloat16

D_MODEL = 1024
HEADS = 8
HEAD_DIM = 64
WIDTH = HEADS * HEAD_DIM
CONV_DIM = 3 * WIDTH
CONV_WIDTH = 4
D_FF = 4 * D_MODEL
EPS = 1e-6
LOG2E = 1.4426950408889634

LANES = 128
SUBLANES = 8
TOKEN_TILE = 256
POST_TILE = 512
ATTN_TILE = 256
ATTN_HEADS = 8
KEY_TILE = 512
FF_TILE = 1024
GDN_CHUNK = 64
VMEM_LIMIT = 60 * 1024 * 1024

C_Q, C_K, C_V, C_GQKV, C_GG, C_GA, C_GB, C_SMALL = 0, 512, 1024, 1536, 3072, 3584, 4608, 5632
IN_COLS = C_SMALL + LANES
L_LOGF, L_G, L_BETA = 0, 8, 16


def _split3(x):
    hi = x.astype(BF16)
    r = x - hi.astype(F32)
    mid = r.astype(BF16)
    lo = (r - mid.astype(F32)).astype(BF16)
    return hi, mid, lo


def _dot(a, b):
    return jnp.dot(a, b, preferred_element_type=F32)


def _dot_nt(a, b):
    return lax.dot_general(a, b, (((1,), (1,)), ((), ())), preferred_element_type=F32)


def _dot_tn(a, b):
    return lax.dot_general(a, b, (((0,), (0,)), ((), ())), preferred_element_type=F32)


def _sel_dot(sel, x, terms=3):
    parts = _split3(x)[:terms]
    n = x.shape[1]
    if n % LANES:
        return sum(_dot(sel, p) for p in parts)
    wide = _dot(sel, jnp.concatenate(parts, axis=1))
    return sum(wide[:, t * n:(t + 1) * n] for t in range(terms))


def _dot_sel(x, sel, terms=3):
    parts = _split3(x)[:terms]
    if terms == 1:
        return _dot(parts[0], sel)
    out = _dot(jnp.concatenate(parts[0:2], axis=1), jnp.concatenate([sel, sel], axis=0))
    for p in parts[2:]:
        out = out + _dot(p, sel)
    return out


def _sigmoid(x):
    return 1.0 / (1.0 + jnp.exp(-x))


def _rms(x, gain):
    ms = jnp.mean(x * x, axis=-1, keepdims=True)
    return x * lax.rsqrt(ms + EPS) * gain


def _const_spec(shape):
    nd = len(shape)
    return pl.BlockSpec(shape, lambda *_: (0,) * nd, pipeline_mode=pl.Buffered(1))


def _inproj_kernel(x_ref, gain_ref, w_ref, padd_ref, palog_ref, tri_ref,
                   q_ref, k_ref, vt_ref, fk_ref, fv_ref, gqkv_ref, sgg_ref, sga_ref, sgb_ref,
                   small_ref, cum_ref, carry_ref, *, tiles_per_seg):
    i = pl.program_id(0)
    h = _rms(x_ref[...], gain_ref[...]).astype(BF16)

    def mm(c0, width):
        return _dot(h, w_ref[:, c0:c0 + width])

    zq = mm(C_Q, WIDTH)
    q_ref[...] = (zq * (LOG2E * HEAD_DIM ** -0.5)).astype(BF16)
    zk = mm(C_K, WIDTH)
    k_ref[...] = zk.astype(BF16)
    zv = mm(C_V, WIDTH)
    fk_ref[...] = zk.reshape(fk_ref.shape)
    fv_ref[...] = zv.reshape(fv_ref.shape)
    vt_ref[0] = zv.T.astype(BF16)
    for c in range(3):
        gqkv_ref[:, c * WIDTH:(c + 1) * WIDTH] = mm(C_GQKV + c * WIDTH, WIDTH)
    zg = mm(C_GG, WIDTH)
    sgg_ref[...] = (zg * _sigmoid(zg)).astype(BF16)
    for c in range(2):
        sga_ref[:, c * WIDTH:(c + 1) * WIDTH] = _sigmoid(mm(C_GA + c * WIDTH, WIDTH)).astype(BF16)
        sgb_ref[:, c * WIDTH:(c + 1) * WIDTH] = _sigmoid(mm(C_GB + c * WIDTH, WIDTH)).astype(BF16)

    zs = mm(C_SMALL, LANES)
    t = zs + padd_ref[...]
    l1p = jnp.log1p(jnp.exp(-jnp.abs(t)))
    logf = jnp.minimum(t, 0.0) - l1p
    g = -jnp.exp(palog_ref[...]) * (jnp.maximum(t, 0.0) + l1p)
    beta = _sigmoid(zs)
    lane = lax.broadcasted_iota(jnp.int32, zs.shape, 1)
    small = jnp.where(lane < L_G, logf, jnp.where(lane < L_BETA, g, jnp.where(lane < L_BETA + HEADS, beta, 0.0)))
    small_ref[...] = small

    cum = _sel_dot(tri_ref[...], small)
    if tiles_per_seg > 1:
        @pl.when(i % tiles_per_seg == 0)
        def _():
            carry_ref[...] = jnp.zeros_like(carry_ref)
        cum = cum + carry_ref[0:1, :]
        carry_ref[...] = jnp.broadcast_to(cum[-1:, :], carry_ref.shape)
    cum_ref[...] = cum


def _in_proj(x, gain, w_r, padd, palog, seg_len):
    t_total = x.shape[0]
    tm = TOKEN_TILE
    assert t_total % tm == 0
    nt = t_total // tm
    if seg_len >= tm:
        assert seg_len % tm == 0
        tiles_per_seg = seg_len // tm
        r = np.arange(tm)
        tri = (r[None, :] <= r[:, None])
    else:
        assert tm % seg_len == 0
        tiles_per_seg = 1
        r = np.arange(tm)
        tri = (r[None, :] <= r[:, None]) & ((r[None, :] // seg_len) == (r[:, None] // seg_len))
    tri = jnp.asarray(tri, BF16)

    row = lambda w: pl.BlockSpec((tm, w), lambda i: (i, 0))
    out_shape = (
        jax.ShapeDtypeStruct((t_total, WIDTH), BF16),
        jax.ShapeDtypeStruct((t_total, WIDTH), BF16),
        jax.ShapeDtypeStruct((nt, WIDTH, tm), BF16),
        jax.ShapeDtypeStruct((t_total, HEADS, HEAD_DIM), F32),
        jax.ShapeDtypeStruct((t_total, HEADS, HEAD_DIM), F32),
        jax.ShapeDtypeStruct((t_total, CONV_DIM), F32),
        jax.ShapeDtypeStruct((t_total, WIDTH), BF16),
        jax.ShapeDtypeStruct((t_total, D_MODEL), BF16),
        jax.ShapeDtypeStruct((t_total, D_MODEL), BF16),
        jax.ShapeDtypeStruct((t_total, LANES), F32),
        jax.ShapeDtypeStruct((t_total, LANES), F32),
    )
    per_head = pl.BlockSpec((tm, HEADS, HEAD_DIM), lambda i: (i, 0, 0))
    out_specs = (row(WIDTH), row(WIDTH), pl.BlockSpec((1, WIDTH, tm), lambda i: (i, 0, 0)),
                 per_head, per_head, row(CONV_DIM), row(WIDTH), row(D_MODEL), row(D_MODEL),
                 row(LANES), row(LANES))
    return pl.pallas_call(
        functools.partial(_inproj_kernel, tiles_per_seg=tiles_per_seg),
        grid=(nt,),
        in_specs=[row(D_MODEL), _const_spec((1, D_MODEL)), _const_spec((D_MODEL, IN_COLS)),
                  _const_spec((1, LANES)), _const_spec((1, LANES)), _const_spec((tm, tm))],
        out_specs=out_specs,
        out_shape=out_shape,
        scratch_shapes=[pltpu.VMEM((SUBLANES, LANES), F32)],
        compiler_params=pltpu.CompilerParams(dimension_semantics=("arbitrary",), vmem_limit_bytes=VMEM_LIMIT),
        name="in_proj",
    )(x, gain, w_r, padd, palog, tri)


def _fox_prompt_kernel(q_ref, k_ref, vt_ref, cum_ref, o_ref, bias_ref, s0_ref, s1_ref, p0_ref, p1_ref, acc_ref,
                       *, seq, tq, tk, nh):
    grp = pl.program_id(1)
    qi = pl.program_id(2)
    rows = 512
    vt_tile = vt_ref.shape[2]
    sub = tk // vt_tile
    heads = range(nh)
    s_slots = (s0_ref, s1_ref)
    p_slots = (p0_ref, p1_ref)

    @pl.when(qi == 0)
    def _prep():
        def fill(r, carry):
            c = cum_ref[pl.ds(r * rows, rows), :]
            lane = lax.broadcasted_iota(jnp.int32, c.shape, 1)
            slab = jnp.zeros(c.shape, F32)
            for hh in heads:
                col = jnp.sum(jnp.where(lane == nh * grp + hh, c, 0.0), axis=-1, keepdims=True) * (-LOG2E)
                for t, part in enumerate(_split3(col)):
                    slab = jnp.where(lane == 3 * hh + t, part.astype(F32), slab)
            bias_ref[pl.ds(r * rows, rows), :] = slab.astype(BF16)
            return carry
        lax.fori_loop(0, seq // rows, fill, 0)

    lane_q = lax.broadcasted_iota(jnp.int32, (tq, LANES), 1)
    qh = []
    for hh in heads:
        q2 = q_ref[:, (hh // 2) * LANES:(hh // 2 + 1) * LANES]
        half = hh % 2
        q_m = jnp.where((lane_q >= HEAD_DIM * half) & (lane_q < HEAD_DIM * (half + 1)), q2, jnp.zeros_like(q2))
        ones = jnp.where((lane_q >= 3 * hh) & (lane_q < 3 * hh + 3), 1.0, 0.0)
        qh.append(jnp.concatenate([q_m.astype(F32), ones], axis=1).T.astype(BF16))

    def scores_to(slot, j):
        bias = bias_ref[pl.ds(j * tk, tk), :]
        kjs = [jnp.concatenate([k_ref[pl.ds(j * tk, tk), sl * LANES:(sl + 1) * LANES], bias], axis=1)
               for sl in range(nh // 2)]
        s_ts = [_dot(kjs[hh // 2], qh[hh]) for hh in heads]
        for hh in heads:
            s_slots[slot][hh] = s_ts[hh]
        return tuple(jnp.max(s_t, axis=0, keepdims=True) for s_t in s_ts)

    ones_rows = jnp.ones((2 * SUBLANES, vt_tile), BF16)

    def add_values(slot, j, alphas):
        pvs = [sum(_dot(jnp.concatenate([vt_ref[j * sub + t, HEAD_DIM * hh:HEAD_DIM * (hh + 1), :], ones_rows], axis=0),
                        p_slots[slot][hh, t * vt_tile:(t + 1) * vt_tile, :]) for t in range(sub)) for hh in heads]
        for hh in heads:
            acc_ref[hh] = alphas[hh] * acc_ref[hh] + pvs[hh]

    def rescale(ms, bms):
        m_new = tuple(jnp.maximum(ms[hh], bms[hh]) for hh in heads)
        return m_new, tuple(jnp.exp2(ms[hh] - m_new[hh]) for hh in heads)

    def probabilities(slot, hh, m_new, visible_from):
        s_t = s_slots[slot][hh]
        if visible_from is not None:
            s_t = jnp.where(visible_from, s_t, -1e30)
        p_slots[slot][hh] = jnp.exp2(s_t - m_new).astype(BF16)

    def trip(j, rd, carry):
        ms, a_prev, bms = carry
        wr = 1 - rd
        add_values(wr, jnp.maximum(j - 1, 0), a_prev)
        m_new, alphas = rescale(ms, bms)
        for hh in heads:
            probabilities(rd, hh, m_new[hh], None)
        bm_next = scores_to(wr, j + 1)
        return m_new, alphas, bm_next

    n_full = (qi * tq) // tk

    def last(rd, carry):
        ms, a_prev, _ = carry
        add_values(1 - rd, jnp.maximum(n_full - 1, 0), a_prev)
        kr = lax.broadcasted_iota(jnp.int32, (tk, tq), 0)
        qc = lax.broadcasted_iota(jnp.int32, (tk, tq), 1)
        visible = kr <= qc + (qi * tq - n_full * tk)
        bms = tuple(jnp.max(jnp.where(visible, s_slots[rd][hh], -1e30), axis=0, keepdims=True) for hh in heads)
        m_new, alphas = rescale(ms, bms)
        for hh in heads:
            probabilities(rd, hh, m_new[hh], visible)
        add_values(rd, n_full, alphas)
        return jnp.concatenate([acc_ref[hh, 0:HEAD_DIM, :] / acc_ref[hh, HEAD_DIM:HEAD_DIM + 1, :] for hh in heads],
                               axis=0)

    each = lambda f: tuple(f() for _ in heads)
    p1_ref[...] = jnp.zeros(p1_ref.shape, BF16)
    acc_ref[...] = jnp.zeros(acc_ref.shape, F32)
    init = (each(lambda: jnp.full((1, tq), -1e30, F32)), each(lambda: jnp.ones((1, tq), F32)), scores_to(0, 0))
    carry = lax.fori_loop(
        0, n_full, lambda j, c: lax.cond(j % 2 == 0, lambda c: trip(j, 0, c), lambda c: trip(j, 1, c), c), init)
    o_t = lax.cond(n_full % 2 == 0, lambda c: last(0, c), lambda c: last(1, c), carry)
    o_ref[...] = o_t.T.astype(BF16)


def _fox_prompt(q, k, vt, cum, batch, seq):
    tq, tk, nh = ATTN_TILE, KEY_TILE, ATTN_HEADS
    assert seq % tk == 0 and tk % tq == 0 and tk % TOKEN_TILE == 0 and HEADS % nh == 0 and nh % 2 == 0
    nq = seq // tq
    width = nh * HEAD_DIM
    return pl.pallas_call(
        functools.partial(_fox_prompt_kernel, seq=seq, tq=tq, tk=tk, nh=nh),
        grid=(batch, HEADS // nh, nq),
        in_specs=[pl.BlockSpec((tq, width), lambda b, g, i: (b * nq + i, g)),
                  pl.BlockSpec((seq, width), lambda b, g, i: (b, g), pipeline_mode=pl.Buffered(1)),
                  pl.BlockSpec((seq // TOKEN_TILE, width, TOKEN_TILE), lambda b, g, i: (b, g, 0),
                               pipeline_mode=pl.Buffered(1)),
                  pl.BlockSpec((seq, LANES), lambda b, g, i: (b, 0), pipeline_mode=pl.Buffered(1))],
        out_specs=pl.BlockSpec((tq, width), lambda b, g, i: (b * nq + i, g)),
        out_shape=jax.ShapeDtypeStruct((batch * seq, WIDTH), BF16),
        scratch_shapes=[pltpu.VMEM((seq, LANES), BF16),
                        pltpu.VMEM((nh, tk, tq), F32), pltpu.VMEM((nh, tk, tq), F32),
                        pltpu.VMEM((nh, tk, tq), BF16), pltpu.VMEM((nh, tk, tq), BF16),
                        pltpu.VMEM((nh, HEAD_DIM + 2 * SUBLANES, tq), F32)],
        compiler_params=pltpu.CompilerParams(dimension_semantics=("arbitrary",) * 3, vmem_limit_bytes=VMEM_LIMIT),
        name="fox_prompt",
    )(q, k, vt, cum)


def _fox_sample_kernel(q_ref, kn_ref, vn_ref, small_ref, kc_ref, vc_ref, lfc_ref, tri_ref, trin_ref, o_ref,
                       *, past, steps):
    blk = tri_ref.shape[0]
    carry = jnp.zeros((1, HEADS), F32)
    cums = []
    for r in range(past // blk):
        c = _sel_dot(tri_ref[...], lfc_ref[0, r * blk:(r + 1) * blk, :]) + carry
        cums.append(c)
        carry = c[-1:, :]
    cum_c = jnp.concatenate(cums, axis=0)
    cum_n = _sel_dot(trin_ref[...], small_ref[:, L_LOGF:L_LOGF + HEADS]) + carry

    q = q_ref[...]
    kc = kc_ref[0].astype(BF16)
    vc = vc_ref[0].astype(BF16)
    kn = kn_ref[...].astype(BF16)
    vn = vn_ref[...].astype(BF16)
    lane_c = lax.broadcasted_iota(jnp.int32, (past, LANES), 1)
    lane_n = lax.broadcasted_iota(jnp.int32, (steps, LANES), 1)
    kr = lax.broadcasted_iota(jnp.int32, (steps, steps), 0)
    qc = lax.broadcasted_iota(jnp.int32, (steps, steps), 1)
    ones_c = jnp.ones((past, LANES), BF16)
    ones_n = jnp.ones((steps, LANES), BF16)
    heads = range(HEADS)
    slab = [slice((hd // 2) * LANES, (hd // 2 + 1) * LANES) for hd in heads]
    in_c = [(lane_c >= HEAD_DIM * (hd % 2)) & (lane_c < HEAD_DIM * (hd % 2 + 1)) for hd in heads]
    in_n = [(lane_n >= HEAD_DIM * (hd % 2)) & (lane_n < HEAD_DIM * (hd % 2 + 1)) for hd in heads]
    qms = [jnp.where(in_n[hd], q[:, slab[hd]], jnp.zeros((steps, LANES), BF16)) for hd in heads]
    s1s = [_dot_nt(kc[:, slab[hd]], qms[hd]) - LOG2E * cum_c[:, hd:hd + 1] for hd in heads]
    s2s = [jnp.where(kr <= qc, _dot_nt(kn[:, slab[hd]], qms[hd]) - LOG2E * cum_n[:, hd:hd + 1], -1e30)
           for hd in heads]
    ms = [jnp.maximum(jnp.max(s1, axis=0, keepdims=True), jnp.max(s2, axis=0, keepdims=True))
          for s1, s2 in zip(s1s, s2s)]
    p1s = [jnp.exp2(s1 - m).astype(BF16) for s1, m in zip(s1s, ms)]
    p2s = [jnp.exp2(s2 - m).astype(BF16) for s2, m in zip(s2s, ms)]
    nums = [_dot_tn(p1s[hd], jnp.where(in_c[hd], vc[:, slab[hd]], jnp.zeros((past, LANES), BF16)))
            + _dot_tn(p2s[hd], jnp.where(in_n[hd], vn[:, slab[hd]], jnp.zeros((steps, LANES), BF16))) for hd in heads]
    dens = [_dot_tn(p1s[hd], ones_c) + _dot_tn(p2s[hd], ones_n) for hd in heads]
    outs = [n / d for n, d in zip(nums, dens)]
    o_ref[...] = jnp.concatenate([outs[2 * pr] + outs[2 * pr + 1] for pr in range(HEADS // 2)], axis=1).astype(BF16)


def _fox_sample(q, kn, vn, small, kc, vc, lfc, batch, steps, past):
    blk = 256
    assert past % blk == 0
    r = np.arange(blk)
    tri = jnp.asarray(r[None, :] <= r[:, None], BF16)
    rn = np.arange(steps)
    trin = jnp.asarray(rn[None, :] <= rn[:, None], BF16)
    row = lambda w: pl.BlockSpec((steps, w), lambda b: (b, 0))
    return pl.pallas_call(
        functools.partial(_fox_sample_kernel, past=past, steps=steps),
        grid=(batch,),
        in_specs=[row(WIDTH), row(WIDTH), row(WIDTH), row(LANES),
                  pl.BlockSpec((1, past, WIDTH), lambda b: (b, 0, 0)),
                  pl.BlockSpec((1, past, WIDTH), lambda b: (b, 0, 0)),
                  pl.BlockSpec((1, past, HEADS), lambda b: (b, 0, 0)),
                  _const_spec((blk, blk)), _const_spec((steps, steps))],
        out_specs=row(WIDTH),
        out_shape=jax.ShapeDtypeStruct((batch * steps, WIDTH), BF16),
        compiler_params=pltpu.CompilerParams(dimension_semantics=("arbitrary",), vmem_limit_bytes=VMEM_LIMIT),
        name="fox_sample",
    )(q, kn, vn, small, kc, vc, lfc, tri, trin)


def _gdn_kernel(gq_ref, small_ref, hist_ref, s0_ref, convw_ref, tri_ref, hsum_ref, hexp_g_ref, hexp_b_ref,
                o_ref, sfin_ref, xbuf, state, *, chunk):
    c_idx = pl.program_id(1)
    n_chunks = pl.num_programs(1)
    hist_rows = SUBLANES

    @pl.when(c_idx == 0)
    def _():
        xbuf[0:hist_rows, :] = hist_ref[0]
        state[...] = s0_ref[0]

    xbuf[hist_rows:hist_rows + chunk, :] = gq_ref[...]
    w = convw_ref[...]
    base = hist_rows - (CONV_WIDTH - 1)
    y = xbuf[base:base + chunk, :] * w[0:1, :]
    for i in range(1, CONV_WIDTH):
        y = y + xbuf[base + i:base + i + chunk, :] * w[i:i + 1, :]
    xbuf[0:hist_rows, :] = xbuf[chunk:chunk + hist_rows, :]
    y = y * _sigmoid(y)

    hsum = hsum_ref[...]
    hexp_g = hexp_g_ref[...]
    hexp_b = hexp_b_ref[...]

    def l2n(x):
        inv = lax.rsqrt(_dot_sel(x * x, hsum, terms=2) + EPS)
        return x * _dot_sel(inv, hexp_g, terms=2)

    q = l2n(y[:, 0:WIDTH]) * (HEAD_DIM ** -0.5)
    k = l2n(y[:, WIDTH:2 * WIDTH])
    v = y[:, 2 * WIDTH:3 * WIDTH]

    small = small_ref[...]
    gc = _sel_dot(tri_ref[...], small)
    gc_x = _dot_sel(gc, hexp_g)
    beta_x = _dot_sel(small, hexp_b, terms=2)
    egc_x = jnp.exp(gc_x)
    glast_x = gc_x[chunk - 1:chunk, :]
    eglast_x = jnp.exp(glast_x)
    kb = k * beta_x
    vb = v * beta_x
    kbe = kb * egc_x
    qe = q * egc_x
    kdec = k * jnp.exp(glast_x - gc_x)

    pad_rows = LANES - chunk
    gc_sq = jnp.concatenate([gc, jnp.zeros((pad_rows, LANES), F32)], axis=0) if pad_rows else gc
    gc_t = gc_sq.T

    ri = lax.broadcasted_iota(jnp.int32, (chunk, chunk), 0)
    ci = lax.broadcasted_iota(jnp.int32, (chunk, chunk), 1)
    eye = jnp.where(ri == ci, 1.0, 0.0).astype(F32)
    n_double = int(np.log2(chunk)) - 1
    assert 2 ** (n_double + 1) == chunk

    heads = range(HEADS)
    sls = [slice(hd * HEAD_DIM, (hd + 1) * HEAD_DIM) for hd in heads]
    decs = [jnp.where(ri >= ci, jnp.exp(gc[:, L_G + hd:L_G + hd + 1] - gc_t[L_G + hd:L_G + hd + 1, 0:chunk]), 0.0)
            for hd in heads]
    k_hs = [k[:, sl].astype(BF16) for sl in sls]
    pws = [-(_dot_nt(kb[:, sl].astype(BF16), k_h) * jnp.where(ri > ci, dec, 0.0))
           for sl, k_h, dec in zip(sls, k_hs, decs)]
    qks = [(_dot_nt(q[:, sl].astype(BF16), k_h) * dec).astype(BF16) for sl, k_h, dec in zip(sls, k_hs, decs)]
    tinvs = [eye + pw for pw in pws]
    for _ in range(n_double):
        pws = [pw.astype(BF16) for pw in pws]
        pws = [_dot(pw, pw) for pw in pws]
        tinvs = [tinv + _dot(tinv.astype(BF16), pw.astype(BF16)) for tinv, pw in zip(tinvs, pws)]
    uws = [_dot(tinv.astype(BF16), jnp.concatenate([vb[:, sl], kbe[:, sl]], axis=1).astype(BF16))
           for tinv, sl in zip(tinvs, sls)]
    s_fs = [state[hd] for hd in heads]
    s_bs = [s_f.astype(BF16) for s_f in s_fs]
    v_news = [(uw[:, 0:HEAD_DIM] - _dot(uw[:, HEAD_DIM:2 * HEAD_DIM].astype(BF16), s_b)).astype(BF16)
              for uw, s_b in zip(uws, s_bs)]
    outs = [_dot(qe[:, sl].astype(BF16), s_b) + _dot(qk, v_new)
            for sl, s_b, qk, v_new in zip(sls, s_bs, qks, v_news)]
    for hd in heads:
        state[hd] = s_fs[hd] * eglast_x[:, sls[hd]] + _dot_tn(kdec[:, sls[hd]].astype(BF16), v_news[hd])
    o_ref[...] = jnp.concatenate(outs, axis=1)

    @pl.when(c_idx == n_chunks - 1)
    def _():
        sfin_ref[0] = state[...]


def _gdn(gqkv, small, hist, s0, conv_w, batch, seq, chunk):
    assert seq % chunk == 0 and chunk % SUBLANES == 0 and chunk <= LANES
    nc = seq // chunk
    r = np.arange(chunk)
    tri = jnp.asarray(r[None, :] <= r[:, None], BF16)
    head_of = np.arange(WIDTH) // HEAD_DIM
    hsum = jnp.asarray(head_of[:, None] == (np.arange(LANES)[None, :] - L_G), BF16)
    hexp_g = jnp.asarray((np.arange(LANES)[:, None] - L_G) == head_of[None, :], BF16)
    hexp_b = jnp.asarray((np.arange(LANES)[:, None] - L_BETA) == head_of[None, :], BF16)
    conv_w8 = jnp.concatenate([conv_w, jnp.zeros((SUBLANES - CONV_WIDTH, CONV_DIM), F32)], axis=0)
    return pl.pallas_call(
        functools.partial(_gdn_kernel, chunk=chunk),
        grid=(batch, nc),
        in_specs=[pl.BlockSpec((chunk, CONV_DIM), lambda b, c: (b * nc + c, 0)),
                  pl.BlockSpec((chunk, LANES), lambda b, c: (b * nc + c, 0)),
                  pl.BlockSpec((1, SUBLANES, CONV_DIM), lambda b, c: (b, 0, 0)),
                  pl.BlockSpec((1, HEADS, HEAD_DIM, HEAD_DIM), lambda b, c: (b, 0, 0, 0)),
                  _const_spec((SUBLANES, CONV_DIM)), _const_spec((chunk, chunk)),
                  _const_spec((WIDTH, LANES)), _const_spec((LANES, WIDTH)), _const_spec((LANES, WIDTH))],
        out_specs=(pl.BlockSpec((chunk, WIDTH), lambda b, c: (b * nc + c, 0)),
                   pl.BlockSpec((1, HEADS, HEAD_DIM, HEAD_DIM), lambda b, c: (b, 0, 0, 0))),
        out_shape=(jax.ShapeDtypeStruct((batch * seq, WIDTH), F32),
                   jax.ShapeDtypeStruct((batch, HEADS, HEAD_DIM, HEAD_DIM), F32)),
        scratch_shapes=[pltpu.VMEM((chunk + SUBLANES, CONV_DIM), F32),
                        pltpu.VMEM((HEADS, HEAD_DIM, HEAD_DIM), F32)],
        compiler_params=pltpu.CompilerParams(dimension_semantics=("arbitrary", "arbitrary"),
                                             vmem_limit_bytes=VMEM_LIMIT),
        name="gdn",
    )(gqkv, small, hist, s0, conv_w8, tri, hsum, hexp_g, hexp_b)


GROUP = 4
GROUP_W = GROUP * HEAD_DIM


def _stack_heads(x, lane):
    return jnp.concatenate([jnp.where(lane == hh, x, jnp.zeros_like(x)) for hh in range(GROUP)], axis=0)


def _fold_heads(x, rows):
    out = x[0:rows]
    for hh in range(1, GROUP):
        out = out + x[hh * rows:(hh + 1) * rows]
    return out


def _gdn_local_kernel(gq_ref, prev_ref, small_ref, convw_ref, tri_ref, hsum_ref, hexp_g_ref, hexp_b_ref, dexp_ref,
                      u_ref, w_ref, qe_ref, kdec_ref, qkw_ref, egl_ref, xbuf, *, chunk):
    i = pl.program_id(1)
    tm = gq_ref.shape[0]
    hist_rows = SUBLANES
    prev = prev_ref[...]
    xbuf[0:hist_rows, :] = jnp.where(i == 0, jnp.zeros_like(prev), prev)
    xbuf[hist_rows:hist_rows + tm, :] = gq_ref[...]
    w = convw_ref[...]
    base = hist_rows - (CONV_WIDTH - 1)
    y = xbuf[base:base + tm, :] * w[0:1, :]
    for t in range(1, CONV_WIDTH):
        y = y + xbuf[base + t:base + t + tm, :] * w[t:t + 1, :]
    y = y * _sigmoid(y)

    hsum = hsum_ref[...]
    hexp_g = hexp_g_ref[...]
    hexp_b = hexp_b_ref[...]

    def l2n(x):
        inv = lax.rsqrt(_dot_sel(x * x, hsum, terms=2) + EPS)
        return x * _dot_sel(inv, hexp_g, terms=2)

    q = l2n(y[:, 0:WIDTH]) * (HEAD_DIM ** -0.5)
    k = l2n(y[:, WIDTH:2 * WIDTH])
    v = y[:, 2 * WIDTH:3 * WIDTH]

    small = small_ref[...]
    gc = _sel_dot(tri_ref[...], small)
    gc_x = _dot_sel(gc, hexp_g)
    beta_x = _dot_sel(small, hexp_b, terms=2)
    n_chunks = tm // chunk
    lasts = [gc_x[(c + 1) * chunk - 1:(c + 1) * chunk, :] for c in range(n_chunks)]
    glast_x = jnp.concatenate([jnp.broadcast_to(r, (chunk, WIDTH)) for r in lasts], axis=0)
    egc_x = jnp.exp(gc_x)
    kb = k * beta_x
    vb = (v * beta_x).astype(BF16)
    kbe = (kb * egc_x).astype(BF16)
    kb = kb.astype(BF16)
    qe_ref[...] = (q * egc_x).astype(BF16)
    kdec_ref[...] = (k * jnp.exp(glast_x - gc_x)).astype(BF16)
    for c in range(n_chunks):
        egl_ref[c] = jnp.broadcast_to(jnp.exp(lasts[c]), (SUBLANES, WIDTH))
    q = q.astype(BF16)
    k = k.astype(BF16)

    rows = GROUP * chunk
    lane = lax.broadcasted_iota(jnp.int32, (chunk, GROUP_W), 1)
    head_of_lane = lane // HEAD_DIM
    ri = lax.broadcasted_iota(jnp.int32, (rows, rows), 0)
    ci = lax.broadcasted_iota(jnp.int32, (rows, rows), 1)
    same = (ri // chunk) == (ci // chunk)
    incl = same & (ri >= ci)
    strict = same & (ri > ci)
    eye = jnp.where(ri == ci, 1.0, 0.0).astype(F32)
    n_double = int(np.log2(chunk)) - 1

    placed = None
    for t, part in enumerate(_split3(gc)):
        term = _dot(part, dexp_ref[t])
        placed = term if placed is None else placed + term
    lane_d = lax.broadcasted_iota(jnp.int32, (chunk, LANES), 1)
    head_d = jnp.where(lane_d < 6 * HEADS, lane_d // 6, -1)
    used = lane_d[0:1, :] < 6 * HEADS
    hi_half = lane_d[0:1, :] % 6 >= 3
    diff_l = (placed[:, 0:LANES] + jnp.where(used & hi_half, 1.0, 0.0)).astype(BF16)
    diff_r = (jnp.where(used & ~hi_half, 1.0, 0.0) - placed[:, LANES:2 * LANES]).astype(BF16)

    probs = [(slice(c * chunk, (c + 1) * chunk), slice(g * GROUP_W, (g + 1) * GROUP_W))
             for c in range(n_chunks) for g in range(HEADS // GROUP)]
    stack = lambda x, pr: _stack_heads(x[pr[0], pr[1]], head_of_lane)
    stack_d = lambda x, c, g: jnp.concatenate(
        [jnp.where(head_d == GROUP * g + hh, x[c * chunk:(c + 1) * chunk], jnp.zeros((chunk, LANES), BF16))
         for hh in range(GROUP)], axis=0)
    diffs = [_dot_nt(stack_d(diff_l, c, g), stack_d(diff_r, c, g))
             for c in range(n_chunks) for g in range(HEADS // GROUP)]
    aqs = [_dot_nt(jnp.concatenate([stack(kb, pr), stack(q, pr)], axis=0), stack(k, pr)) for pr in probs]
    decays = [jnp.exp(jnp.where(incl, d, -1e30)) for d in diffs]
    for pr, aq, decay in zip(probs, aqs, decays):
        qkw_ref[pr[0], pr[1]] = _fold_heads(aq[rows:2 * rows] * decay, chunk).astype(BF16)
    pws = [-(aq[0:rows] * jnp.where(strict, decay, 0.0)) for aq, decay in zip(aqs, decays)]
    tinvs = [eye + pw for pw in pws]
    pws = [pw.astype(BF16) for pw in pws]
    pws = [_dot(pw, pw).astype(BF16) for pw in pws]
    for it in range(n_double):
        if it < n_double - 1:
            both = [_dot(jnp.concatenate([pw, tinv.astype(BF16)], axis=0), pw) for pw, tinv in zip(pws, tinvs)]
            pws = [b[0:rows].astype(BF16) for b in both]
            tinvs = [tinv + b[rows:2 * rows] for tinv, b in zip(tinvs, both)]
        else:
            tinvs = [tinv + _dot(tinv.astype(BF16), pw) for pw, tinv in zip(pws, tinvs)]
    uws = [_dot(tinv.astype(BF16), jnp.concatenate([stack(vb, pr), stack(kbe, pr)], axis=1))
           for pr, tinv in zip(probs, tinvs)]
    for pr, uw in zip(probs, uws):
        u_ref[pr[0], pr[1]] = _fold_heads(uw[:, 0:GROUP_W], chunk).astype(BF16)
        w_ref[pr[0], pr[1]] = _fold_heads(uw[:, GROUP_W:2 * GROUP_W], chunk).astype(BF16)


def _gdn_scan_kernel(u_ref, w_ref, qe_ref, kdec_ref, qkw_ref, egl_ref, o_ref, sfin_ref, state, *, chunk):
    c_idx = pl.program_id(0)
    batch = u_ref.shape[0]

    @pl.when(c_idx == 0)
    def _():
        state[...] = jnp.zeros_like(state)

    rows = GROUP * chunk
    lane = lax.broadcasted_iota(jnp.int32, (chunk, GROUP_W), 1) // HEAD_DIM
    ri = lax.broadcasted_iota(jnp.int32, (GROUP_W, GROUP_W), 0)
    ci = lax.broadcasted_iota(jnp.int32, (GROUP_W, GROUP_W), 1)
    same = (ri // HEAD_DIM) == (ci // HEAD_DIM)
    probs = [(b, g, slice(g * GROUP_W, (g + 1) * GROUP_W)) for b in range(batch) for g in range(HEADS // GROUP)]
    s_fs = [state[b, g] for b, g, _ in probs]
    s_bs = [s_f.astype(BF16) for s_f in s_fs]
    v_news = [(u_ref[b, :, ls].astype(F32) - _dot(w_ref[b, :, ls], s_b)).astype(BF16)
              for (b, _, ls), s_b in zip(probs, s_bs)]
    for (b, _, ls), s_b, v_new in zip(probs, s_bs, v_news):
        o_ref[b, :, ls] = _dot(jnp.concatenate([qe_ref[b, :, ls], qkw_ref[b, :, ls]], axis=1),
                               jnp.concatenate([s_b, _stack_heads(v_new, lane)], axis=0))
    kvs = [_dot_tn(kdec_ref[b, :, ls], v_new) for (b, _, ls), v_new in zip(probs, v_news)]
    for (b, g, ls), s_f, kv in zip(probs, s_fs, kvs):
        state[b, g] = s_f * egl_ref[b, 0, 0:1, ls] + jnp.where(same, kv, 0.0)

    @pl.when(c_idx == pl.num_programs(0) - 1)
    def _():
        for b in range(batch):
            for hd in range(HEADS):
                g, hh = divmod(hd, GROUP)
                sl = slice(hh * HEAD_DIM, (hh + 1) * HEAD_DIM)
                sfin_ref[b, hd] = state[b, g, sl, sl]


def _gdn_long(gqkv, small, conv_w, batch, seq, chunk):
    tm = TOKEN_TILE
    assert seq % tm == 0 and tm % chunk == 0 and GROUP * chunk == GROUP_W
    nt = seq // tm
    nc = seq // chunk
    r = np.arange(tm)
    tri = jnp.asarray((r[None, :] <= r[:, None]) & ((r[None, :] // chunk) == (r[:, None] // chunk)), BF16)
    head_of = np.arange(WIDTH) // HEAD_DIM
    hsum = jnp.asarray(head_of[:, None] == (np.arange(LANES)[None, :] - L_G), BF16)
    hexp_g = jnp.asarray((np.arange(LANES)[:, None] - L_G) == head_of[None, :], BF16)
    hexp_b = jnp.asarray((np.arange(LANES)[:, None] - L_BETA) == head_of[None, :], BF16)
    conv_w8 = jnp.concatenate([conv_w, jnp.zeros((SUBLANES - CONV_WIDTH, CONV_DIM), F32)], axis=0)
    dexp = np.zeros((3, LANES, 2 * LANES), np.float32)
    for t in range(3):
        for hd in range(HEADS):
            dexp[t, L_G + hd, 6 * hd + t] = 1.0
            dexp[t, L_G + hd, LANES + 6 * hd + 3 + t] = 1.0
    dexp = jnp.asarray(dexp, BF16)
    row = lambda w: pl.BlockSpec((tm, w), lambda b, i: (b * nt + i, 0))
    per_tile = tm // chunk
    tok = jax.ShapeDtypeStruct((batch * seq, WIDTH), BF16)
    u, w, qe, kdec, qkw, egl = pl.pallas_call(
        functools.partial(_gdn_local_kernel, chunk=chunk),
        grid=(batch, nt),
        in_specs=[row(CONV_DIM),
                  pl.BlockSpec((SUBLANES, CONV_DIM),
                               lambda b, i: (jnp.maximum((b * nt + i) * (tm // SUBLANES) - 1, 0), 0)),
                  row(LANES), _const_spec((SUBLANES, CONV_DIM)), _const_spec((tm, tm)),
                  _const_spec((WIDTH, LANES)), _const_spec((LANES, WIDTH)), _const_spec((LANES, WIDTH)),
                  _const_spec((3, LANES, 2 * LANES))],
        out_specs=(row(WIDTH), row(WIDTH), row(WIDTH), row(WIDTH), row(WIDTH),
                   pl.BlockSpec((per_tile, SUBLANES, WIDTH), lambda b, i: (b * nt + i, 0, 0))),
        out_shape=(tok, tok, tok, tok, tok, jax.ShapeDtypeStruct((batch * nc, SUBLANES, WIDTH), F32)),
        scratch_shapes=[pltpu.VMEM((tm + SUBLANES, CONV_DIM), F32)],
        compiler_params=pltpu.CompilerParams(dimension_semantics=("arbitrary", "arbitrary"),
                                             vmem_limit_bytes=VMEM_LIMIT),
        name="gdn_local",
    )(gqkv, gqkv, small, conv_w8, tri, hsum, hexp_g, hexp_b, dexp)

    blk = pl.BlockSpec((batch, chunk, WIDTH), lambda c: (0, c, 0))
    as3 = lambda a: a.reshape(batch, seq, WIDTH)
    o, s_fin = pl.pallas_call(
        functools.partial(_gdn_scan_kernel, chunk=chunk),
        grid=(nc,),
        in_specs=[blk, blk, blk, blk, blk,
                  pl.BlockSpec((batch, 1, SUBLANES, WIDTH), lambda c: (0, c, 0, 0))],
        out_specs=(blk, pl.BlockSpec((batch, HEADS, HEAD_DIM, HEAD_DIM), lambda c: (0, 0, 0, 0))),
        out_shape=(jax.ShapeDtypeStruct((batch, seq, WIDTH), F32),
                   jax.ShapeDtypeStruct((batch, HEADS, HEAD_DIM, HEAD_DIM), F32)),
        scratch_shapes=[pltpu.VMEM((batch, HEADS // GROUP, GROUP_W, GROUP_W), F32)],
        compiler_params=pltpu.CompilerParams(dimension_semantics=("arbitrary",), vmem_limit_bytes=VMEM_LIMIT),
        name="gdn_scan",
    )(as3(u), as3(w), as3(qe), as3(kdec), as3(qkw), egl.reshape(batch, nc, SUBLANES, WIDTH))
    return o.reshape(batch * seq, WIDTH), s_fin


def _post_kernel(x_ref, fox_ref, gdn_ref, sgg_ref, sga_ref, sgb_ref,
                 wpa_ref, wpb_ref, wout_ref, wup_ref, wdown_ref,
                 ng_ref, gpost_ref, gpre2_ref, gpost2_ref, hsum_ref, hexp_ref, y_ref):
    ya = _dot(fox_ref[...], wpa_ref[...])
    o = gdn_ref[...]
    ms = _dot_sel(o * o, hsum_ref[...], terms=2) * (1.0 / HEAD_DIM)
    o = o * _dot_sel(lax.rsqrt(ms + EPS), hexp_ref[...], terms=2) * ng_ref[...] * sgg_ref[...].astype(F32)
    yb = _dot(o.astype(BF16), wpb_ref[...])
    m = sga_ref[...].astype(F32) * ya + sgb_ref[...].astype(F32) * yb
    mix = _dot(m.astype(BF16), wout_ref[...])
    y1 = x_ref[...] + _rms(mix, gpost_ref[...])
    h2 = _rms(y1, gpre2_ref[...]).astype(BF16)
    acc = jnp.zeros(y1.shape, F32)
    for c in range(D_FF // FF_TILE):
        u = jnp.maximum(_dot(h2, wup_ref[:, c * FF_TILE:(c + 1) * FF_TILE]), 0.0)
        acc = acc + _dot((u * u).astype(BF16), wdown_ref[c * FF_TILE:(c + 1) * FF_TILE, :])
    y_ref[...] = y1 + _rms(acc, gpost2_ref[...])


def _post(x, fox, gdn_o, sgg, sga, sgb, w_pa, w_pb, w_out, w_up, w_down, ng, gpost, gpre2, gpost2):
    t_total = x.shape[0]
    tm = min(POST_TILE, t_total)
    assert t_total % tm == 0
    row = lambda w: pl.BlockSpec((tm, w), lambda i: (i, 0))
    head_of = np.arange(WIDTH) // HEAD_DIM
    hsum = jnp.asarray(head_of[:, None] == (np.arange(LANES)[None, :] - L_G), BF16)
    hexp = jnp.asarray((np.arange(LANES)[:, None] - L_G) == head_of[None, :], BF16)
    return pl.pallas_call(
        _post_kernel,
        grid=(t_total // tm,),
        in_specs=[row(D_MODEL), row(WIDTH), row(WIDTH), row(WIDTH), row(D_MODEL), row(D_MODEL),
                  _const_spec((WIDTH, D_MODEL)), _const_spec((WIDTH, D_MODEL)), _const_spec((D_MODEL, D_MODEL)),
                  _const_spec((D_MODEL, D_FF)), _const_spec((D_FF, D_MODEL)),
                  _const_spec((1, WIDTH)), _const_spec((1, D_MODEL)), _const_spec((1, D_MODEL)),
                  _const_spec((1, D_MODEL)), _const_spec((WIDTH, LANES)), _const_spec((LANES, WIDTH))],
        out_specs=row(D_MODEL),
        out_shape=jax.ShapeDtypeStruct((t_total, D_MODEL), F32),
        compiler_params=pltpu.CompilerParams(dimension_semantics=("arbitrary",), vmem_limit_bytes=VMEM_LIMIT),
        name="post",
    )(x, fox, gdn_o, sgg, sga, sgb, w_pa, w_pb, w_out, w_up, w_down, ng, gpost, gpre2, gpost2, hsum, hexp)


def _rearrange_w_in(w_in):
    o_ff = 3 * WIDTH
    o_gqkv = o_ff + HEADS
    o_ga = o_gqkv + CONV_DIM
    o_gb = o_ga + HEADS
    o_gg = o_gb + HEADS
    o_gate_a = o_gg + WIDTH
    o_gate_b = o_gate_a + D_MODEL
    cols = [w_in[:, 0:o_ff], w_in[:, o_gqkv:o_ga], w_in[:, o_gg:o_gate_a], w_in[:, o_gate_a:o_gate_b],
            w_in[:, o_gate_b:o_gate_b + D_MODEL], w_in[:, o_ff:o_gqkv], w_in[:, o_ga:o_gb], w_in[:, o_gb:o_gg],
            jnp.zeros((D_MODEL, LANES - 3 * HEADS), w_in.dtype)]
    return jnp.concatenate(cols, axis=1).astype(BF16)


def _lane_row(*pieces):
    v = jnp.concatenate([p.astype(F32) for p in pieces])
    return jnp.concatenate([v, jnp.zeros((LANES - v.shape[0],), F32)])[None, :]


def _pad_hist(conv_cache):
    b = conv_cache.shape[0]
    return jnp.concatenate([jnp.zeros((b, SUBLANES - (CONV_WIDTH - 1), CONV_DIM), F32), conv_cache.astype(F32)], axis=1)


def kernel(x_prompt, x_sample, cache_fox_k, cache_fox_v, cache_fox_logf, state_gdn, state_gdn_conv,
           w_in, fox_forget_bias, gdn_conv_w, gdn_a_log, gdn_dt_bias, gdn_norm_g,
           w_proj_fox, w_proj_gdn, w_out, norm_mix_pre, norm_mix_post, norm_mlp_pre, norm_mlp_post,
           w_up, w_down):
    depth = w_in.shape[0]
    bp, sp, _ = x_prompt.shape
    bs, ss, _ = x_sample.shape
    past = cache_fox_k.shape[2]
    y_p = x_prompt.reshape(bp * sp, D_MODEL)
    y_s = x_sample.reshape(bs * ss, D_MODEL)
    st_p, st_s = [], []
    for l in range(depth):
        w_r = _rearrange_w_in(w_in[l])
        padd = _lane_row(fox_forget_bias[l], gdn_dt_bias[l])
        palog = _lane_row(jnp.zeros((HEADS,), F32), gdn_a_log[l])
        gain = norm_mix_pre[l][None, :]
        post_w = (w_proj_fox[l].astype(BF16), w_proj_gdn[l].astype(BF16), w_out[l].astype(BF16),
                  w_up[l].astype(BF16), w_down[l].astype(BF16),
                  jnp.tile(gdn_norm_g[l], HEADS)[None, :], norm_mix_post[l][None, :],
                  norm_mlp_pre[l][None, :], norm_mlp_post[l][None, :])

        q, k, vt, fk, fv, gqkv, sgg, sga, sgb, small, cum = _in_proj(y_p, gain, w_r, padd, palog, sp)
        fox = _fox_prompt(q, k, vt, cum, bp, sp)
        gdn_o, s_fin = _gdn_long(gqkv, small, gdn_conv_w[l], bp, sp, GDN_CHUNK)
        y_p_new = _post(y_p, fox, gdn_o, sgg, sga, sgb, *post_w)
        st_p.append((fk.reshape(bp, sp, HEADS, HEAD_DIM), fv.reshape(bp, sp, HEADS, HEAD_DIM),
                     small[:, L_LOGF:L_LOGF + HEADS].reshape(bp, sp, HEADS), s_fin,
                     gqkv.reshape(bp, sp, CONV_DIM)[:, sp - (CONV_WIDTH - 1):, :]))
        y_p = y_p_new

        q, k, vt, fk, fv, gqkv, sgg, sga, sgb, small, cum = _in_proj(y_s, gain, w_r, padd, palog, ss)
        fox = _fox_sample(q, fk.reshape(bs * ss, WIDTH), fv.reshape(bs * ss, WIDTH), small,
                          cache_fox_k[l].reshape(bs, past, WIDTH), cache_fox_v[l].reshape(bs, past, WIDTH),
                          cache_fox_logf[l], bs, ss, past)
        gdn_o, s_new = _gdn(gqkv, small, _pad_hist(state_gdn_conv[l]), state_gdn[l].astype(F32),
                            gdn_conv_w[l], bs, ss, ss)
        y_s_new = _post(y_s, fox, gdn_o, sgg, sga, sgb, *post_w)
        conv_ext = jnp.concatenate([state_gdn_conv[l].astype(F32), gqkv.reshape(bs, ss, CONV_DIM)], axis=1)
        st_s.append((fk.reshape(bs, ss, HEADS, HEAD_DIM), fv.reshape(bs, ss, HEADS, HEAD_DIM),
                     small[:, L_LOGF:L_LOGF + HEADS].reshape(bs, ss, HEADS), s_new,
                     conv_ext[:, conv_ext.shape[1] - (CONV_WIDTH - 1):, :]))
        y_s = y_s_new

    fk_p, fv_p, lf_p, sg_p, cv_p = [jnp.stack(a) for a in zip(*st_p)]
    fk_s, fv_s, lf_s, sg_s, cv_s = [jnp.stack(a) for a in zip(*st_s)]
    return (y_p.reshape(bp, sp, D_MODEL), y_s.reshape(bs, ss, D_MODEL),
            fk_p, fv_p, lf_p, sg_p, cv_p, fk_s, fv_s, lf_s, sg_s, cv_s)
```

```python
import functools

import jax
import jax.numpy as jnp
import numpy as np
from jax import lax
from jax.experimental import pallas as pl
from jax.experimental.pallas import tpu as pltpu

F32 = jnp.float32
BF16 = jnp.bfloat16

D_MODEL = 1024
HEADS = 8
HEAD_DIM = 64
WIDTH = HEADS * HEAD_DIM
CONV_DIM = 3 * WIDTH
CONV_WIDTH = 4
D_FF = 4 * D_MODEL
EPS = 1e-6
LOG2E = 1.4426950408889634

LANES = 128
SUBLANES = 8
TOKEN_TILE = 256
POST_TILE = 512
ATTN_TILE = 256
ATTN_HEADS = 8
KEY_TILE = 512
FF_TILE = 1024
GDN_CHUNK = 64
PREP_ROWS = 512
CUM_BLOCK = 256
DECAY_LANES = 6
VMEM_LIMIT = 60 * 1024 * 1024

C_Q, C_K, C_V, C_GQKV, C_GG, C_GA, C_GB, C_SMALL = 0, 512, 1024, 1536, 3072, 3584, 4608, 5632
IN_COLS = C_SMALL + LANES
L_LOGF, L_G, L_BETA = 0, 8, 16


def _split3(x):
    hi = x.astype(BF16)
    r = x - hi.astype(F32)
    mid = r.astype(BF16)
    lo = (r - mid.astype(F32)).astype(BF16)
    return hi, mid, lo


def _dot(a, b):
    return jnp.dot(a, b, preferred_element_type=F32)


def _dot_nt(a, b):
    return lax.dot_general(a, b, (((1,), (1,)), ((), ())), preferred_element_type=F32)


def _dot_tn(a, b):
    return lax.dot_general(a, b, (((0,), (0,)), ((), ())), preferred_element_type=F32)


def _sel_dot(sel, x, terms=3):
    parts = _split3(x)[:terms]
    n = x.shape[1]
    if n % LANES:
        return sum(_dot(sel, p) for p in parts)
    wide = _dot(sel, jnp.concatenate(parts, axis=1))
    return sum(wide[:, t * n:(t + 1) * n] for t in range(terms))


def _dot_sel(x, sel, terms=3):
    parts = _split3(x)[:terms]
    if terms == 1:
        return _dot(parts[0], sel)
    out = _dot(jnp.concatenate(parts[0:2], axis=1), jnp.concatenate([sel, sel], axis=0))
    for p in parts[2:]:
        out = out + _dot(p, sel)
    return out


def _sigmoid(x):
    return 1.0 / (1.0 + jnp.exp(-x))


def _rms(x, gain):
    ms = jnp.mean(x * x, axis=-1, keepdims=True)
    return x * lax.rsqrt(ms + EPS) * gain


def _const_spec(shape):
    nd = len(shape)
    return pl.BlockSpec(shape, lambda *_: (0,) * nd, pipeline_mode=pl.Buffered(1))


def _inproj_kernel(x_ref, gain_ref, w_ref, padd_ref, palog_ref, tri_ref,
                   q_ref, k_ref, vt_ref, fk_ref, fv_ref, gqkv_ref, sgg_ref, sga_ref, sgb_ref,
                   small_ref, cum_ref, carry_ref, *, tiles_per_seg):
    i = pl.program_id(0)
    h = _rms(x_ref[...], gain_ref[...]).astype(BF16)

    def mm(c0, width):
        return _dot(h, w_ref[:, c0:c0 + width])

    zq = mm(C_Q, WIDTH)
    q_ref[...] = (zq * (LOG2E * HEAD_DIM ** -0.5)).astype(BF16)
    zk = mm(C_K, WIDTH)
    k_ref[...] = zk.astype(BF16)
    zv = mm(C_V, WIDTH)
    fk_ref[...] = zk.reshape(fk_ref.shape)
    fv_ref[...] = zv.reshape(fv_ref.shape)
    vt_ref[0] = zv.T.astype(BF16)
    for c in range(3):
        gqkv_ref[:, c * WIDTH:(c + 1) * WIDTH] = mm(C_GQKV + c * WIDTH, WIDTH)
    zg = mm(C_GG, WIDTH)
    sgg_ref[...] = (zg * _sigmoid(zg)).astype(BF16)
    for c in range(2):
        sga_ref[:, c * WIDTH:(c + 1) * WIDTH] = _sigmoid(mm(C_GA + c * WIDTH, WIDTH)).astype(BF16)
        sgb_ref[:, c * WIDTH:(c + 1) * WIDTH] = _sigmoid(mm(C_GB + c * WIDTH, WIDTH)).astype(BF16)

    zs = mm(C_SMALL, LANES)
    t = zs + padd_ref[...]
    l1p = jnp.log1p(jnp.exp(-jnp.abs(t)))
    logf = jnp.minimum(t, 0.0) - l1p
    g = -jnp.exp(palog_ref[...]) * (jnp.maximum(t, 0.0) + l1p)
    beta = _sigmoid(zs)
    lane = lax.broadcasted_iota(jnp.int32, zs.shape, 1)
    small = jnp.where(lane < L_G, logf, jnp.where(lane < L_BETA, g, jnp.where(lane < L_BETA + HEADS, beta, 0.0)))
    small_ref[...] = small

    cum = _sel_dot(tri_ref[...], small)
    if tiles_per_seg > 1:
        @pl.when(i % tiles_per_seg == 0)
        def _():
            carry_ref[...] = jnp.zeros_like(carry_ref)
        cum = cum + carry_ref[0:1, :]
        carry_ref[...] = jnp.broadcast_to(cum[-1:, :], carry_ref.shape)
    cum_ref[...] = cum


def _in_proj(x, gain, w_r, padd, palog, seg_len):
    t_total = x.shape[0]
    tm = TOKEN_TILE
    assert t_total % tm == 0
    nt = t_total // tm
    if seg_len >= tm:
        assert seg_len % tm == 0
        tiles_per_seg = seg_len // tm
        r = np.arange(tm)
        tri = (r[None, :] <= r[:, None])
    else:
        assert tm % seg_len == 0
        tiles_per_seg = 1
        r = np.arange(tm)
        tri = (r[None, :] <= r[:, None]) & ((r[None, :] // seg_len) == (r[:, None] // seg_len))
    tri = jnp.asarray(tri, BF16)

    row = lambda w: pl.BlockSpec((tm, w), lambda i: (i, 0))
    out_shape = (
        jax.ShapeDtypeStruct((t_total, WIDTH), BF16),
        jax.ShapeDtypeStruct((t_total, WIDTH), BF16),
        jax.ShapeDtypeStruct((nt, WIDTH, tm), BF16),
        jax.ShapeDtypeStruct((t_total, HEADS, HEAD_DIM), F32),
        jax.ShapeDtypeStruct((t_total, HEADS, HEAD_DIM), F32),
        jax.ShapeDtypeStruct((t_total, CONV_DIM), F32),
        jax.ShapeDtypeStruct((t_total, WIDTH), BF16),
        jax.ShapeDtypeStruct((t_total, D_MODEL), BF16),
        jax.ShapeDtypeStruct((t_total, D_MODEL), BF16),
        jax.ShapeDtypeStruct((t_total, LANES), F32),
        jax.ShapeDtypeStruct((t_total, LANES), F32),
    )
    per_head = pl.BlockSpec((tm, HEADS, HEAD_DIM), lambda i: (i, 0, 0))
    out_specs = (row(WIDTH), row(WIDTH), pl.BlockSpec((1, WIDTH, tm), lambda i: (i, 0, 0)),
                 per_head, per_head, row(CONV_DIM), row(WIDTH), row(D_MODEL), row(D_MODEL),
                 row(LANES), row(LANES))
    return pl.pallas_call(
        functools.partial(_inproj_kernel, tiles_per_seg=tiles_per_seg),
        grid=(nt,),
        in_specs=[row(D_MODEL), _const_spec((1, D_MODEL)), _const_spec((D_MODEL, IN_COLS)),
                  _const_spec((1, LANES)), _const_spec((1, LANES)), _const_spec((tm, tm))],
        out_specs=out_specs,
        out_shape=out_shape,
        scratch_shapes=[pltpu.VMEM((SUBLANES, LANES), F32)],
        compiler_params=pltpu.CompilerParams(dimension_semantics=("arbitrary",), vmem_limit_bytes=VMEM_LIMIT),
        name="in_proj",
    )(x, gain, w_r, padd, palog, tri)


def _fox_prompt_kernel(q_ref, k_ref, vt_ref, cum_ref, o_ref, bias_ref, s0_ref, s1_ref, p0_ref, p1_ref, acc_ref,
                       *, seq, tq, tk, nh):
    grp = pl.program_id(1)
    qi = pl.program_id(2)
    rows = PREP_ROWS
    vt_tile = vt_ref.shape[2]
    sub = tk // vt_tile
    heads = range(nh)
    s_slots = (s0_ref, s1_ref)
    p_slots = (p0_ref, p1_ref)

    @pl.when(qi == 0)
    def _prep():
        def fill(r, carry):
            c = cum_ref[pl.ds(r * rows, rows), :]
            lane = lax.broadcasted_iota(jnp.int32, c.shape, 1)
            slab = jnp.zeros(c.shape, F32)
            for hh in heads:
                col = jnp.sum(jnp.where(lane == nh * grp + hh, c, 0.0), axis=-1, keepdims=True) * (-LOG2E)
                for t, part in enumerate(_split3(col)):
                    slab = jnp.where(lane == 3 * hh + t, part.astype(F32), slab)
            bias_ref[pl.ds(r * rows, rows), :] = slab.astype(BF16)
            return carry
        lax.fori_loop(0, seq // rows, fill, 0)

    lane_q = lax.broadcasted_iota(jnp.int32, (tq, LANES), 1)
    qh = []
    for hh in heads:
        q2 = q_ref[:, (hh // 2) * LANES:(hh // 2 + 1) * LANES]
        half = hh % 2
        q_m = jnp.where((lane_q >= HEAD_DIM * half) & (lane_q < HEAD_DIM * (half + 1)), q2, jnp.zeros_like(q2))
        ones = jnp.where((lane_q >= 3 * hh) & (lane_q < 3 * hh + 3), 1.0, 0.0)
        qh.append(jnp.concatenate([q_m.astype(F32), ones], axis=1).T.astype(BF16))

    def scores_to(slot, j):
        bias = bias_ref[pl.ds(j * tk, tk), :]
        kjs = [jnp.concatenate([k_ref[pl.ds(j * tk, tk), sl * LANES:(sl + 1) * LANES], bias], axis=1)
               for sl in range(nh // 2)]
        s_ts = [_dot(kjs[hh // 2], qh[hh]) for hh in heads]
        for hh in heads:
            s_slots[slot][hh] = s_ts[hh]
        return tuple(jnp.max(s_t, axis=0, keepdims=True) for s_t in s_ts)

    ones_rows = jnp.ones((2 * SUBLANES, vt_tile), BF16)

    def add_values(slot, j, alphas):
        pvs = [sum(_dot(jnp.concatenate([vt_ref[j * sub + t, HEAD_DIM * hh:HEAD_DIM * (hh + 1), :], ones_rows], axis=0),
                        p_slots[slot][hh, t * vt_tile:(t + 1) * vt_tile, :]) for t in range(sub)) for hh in heads]
        for hh in heads:
            acc_ref[hh] = alphas[hh] * acc_ref[hh] + pvs[hh]

    def rescale(ms, bms):
        m_new = tuple(jnp.maximum(ms[hh], bms[hh]) for hh in heads)
        return m_new, tuple(jnp.exp2(ms[hh] - m_new[hh]) for hh in heads)

    def probabilities(slot, hh, m_new):
        p_slots[slot][hh] = jnp.exp2(s_slots[slot][hh] - m_new).astype(BF16)

    def trip(j, rd, carry):
        ms, a_prev, bms = carry
        wr = 1 - rd
        add_values(wr, jnp.maximum(j - 1, 0), a_prev)
        m_new, alphas = rescale(ms, bms)
        for hh in heads:
            probabilities(rd, hh, m_new[hh])
        bm_next = scores_to(wr, j + 1)
        return m_new, alphas, bm_next

    n_full = (qi * tq) // tk

    def last(rd, carry):
        ms, a_prev, _ = carry
        add_values(1 - rd, jnp.maximum(n_full - 1, 0), a_prev)
        kr = lax.broadcasted_iota(jnp.int32, (tk, tq), 0)
        qc = lax.broadcasted_iota(jnp.int32, (tk, tq), 1)
        visible = kr <= qc + (qi * tq - n_full * tk)
        for hh in heads:
            s_slots[rd][hh] = jnp.where(visible, s_slots[rd][hh], -1e30)
        bms = tuple(jnp.max(s_slots[rd][hh], axis=0, keepdims=True) for hh in heads)
        m_new, alphas = rescale(ms, bms)
        for hh in heads:
            probabilities(rd, hh, m_new[hh])
        add_values(rd, n_full, alphas)
        return jnp.concatenate([acc_ref[hh, 0:HEAD_DIM, :] / acc_ref[hh, HEAD_DIM:HEAD_DIM + 1, :] for hh in heads],
                               axis=0)

    each = lambda f: tuple(f() for _ in heads)
    p1_ref[...] = jnp.zeros(p1_ref.shape, BF16)
    acc_ref[...] = jnp.zeros(acc_ref.shape, F32)
    init = (each(lambda: jnp.full((1, tq), -1e30, F32)), each(lambda: jnp.ones((1, tq), F32)), scores_to(0, 0))
    carry = lax.fori_loop(
        0, n_full, lambda j, c: lax.cond(j % 2 == 0, lambda c: trip(j, 0, c), lambda c: trip(j, 1, c), c), init)
    o_t = lax.cond(n_full % 2 == 0, lambda c: last(0, c), lambda c: last(1, c), carry)
    o_ref[...] = o_t.T.astype(BF16)


def _fox_prompt(q, k, vt, cum, batch, seq):
    tq, tk, nh = ATTN_TILE, KEY_TILE, ATTN_HEADS
    assert seq % tk == 0 and tk % tq == 0 and tk % TOKEN_TILE == 0 and HEADS % nh == 0 and nh % 2 == 0
    assert seq % PREP_ROWS == 0
    nq = seq // tq
    width = nh * HEAD_DIM
    return pl.pallas_call(
        functools.partial(_fox_prompt_kernel, seq=seq, tq=tq, tk=tk, nh=nh),
        grid=(batch, HEADS // nh, nq),
        in_specs=[pl.BlockSpec((tq, width), lambda b, g, i: (b * nq + i, g)),
                  pl.BlockSpec((seq, width), lambda b, g, i: (b, g), pipeline_mode=pl.Buffered(1)),
                  pl.BlockSpec((seq // TOKEN_TILE, width, TOKEN_TILE), lambda b, g, i: (b, g, 0),
                               pipeline_mode=pl.Buffered(1)),
                  pl.BlockSpec((seq, LANES), lambda b, g, i: (b, 0), pipeline_mode=pl.Buffered(1))],
        out_specs=pl.BlockSpec((tq, width), lambda b, g, i: (b * nq + i, g)),
        out_shape=jax.ShapeDtypeStruct((batch * seq, WIDTH), BF16),
        scratch_shapes=[pltpu.VMEM((seq, LANES), BF16),
                        pltpu.VMEM((nh, tk, tq), F32), pltpu.VMEM((nh, tk, tq), F32),
                        pltpu.VMEM((nh, tk, tq), BF16), pltpu.VMEM((nh, tk, tq), BF16),
                        pltpu.VMEM((nh, HEAD_DIM + 2 * SUBLANES, tq), F32)],
        compiler_params=pltpu.CompilerParams(dimension_semantics=("arbitrary",) * 3, vmem_limit_bytes=VMEM_LIMIT),
        name="fox_prompt",
    )(q, k, vt, cum)


def _fox_sample_kernel(q_ref, kn_ref, vn_ref, small_ref, kc_ref, vc_ref, lfc_ref, tri_ref, trin_ref, o_ref,
                       *, past, steps):
    blk = tri_ref.shape[0]
    carry = jnp.zeros((1, HEADS), F32)
    cums = []
    for r in range(past // blk):
        c = _sel_dot(tri_ref[...], lfc_ref[0, r * blk:(r + 1) * blk, :]) + carry
        cums.append(c)
        carry = c[-1:, :]
    cum_c = jnp.concatenate(cums, axis=0)
    cum_n = _sel_dot(trin_ref[...], small_ref[:, L_LOGF:L_LOGF + HEADS]) + carry

    q = q_ref[...]
    kc = kc_ref[0].astype(BF16)
    vc = vc_ref[0].astype(BF16)
    kn = kn_ref[...].astype(BF16)
    vn = vn_ref[...].astype(BF16)
    lane_c = lax.broadcasted_iota(jnp.int32, (past, LANES), 1)
    lane_n = lax.broadcasted_iota(jnp.int32, (steps, LANES), 1)
    kr = lax.broadcasted_iota(jnp.int32, (steps, steps), 0)
    qc = lax.broadcasted_iota(jnp.int32, (steps, steps), 1)
    ones_c = jnp.ones((past, LANES), BF16)
    ones_n = jnp.ones((steps, LANES), BF16)
    heads = range(HEADS)
    slab = [slice((hd // 2) * LANES, (hd // 2 + 1) * LANES) for hd in heads]
    in_c = [(lane_c >= HEAD_DIM * (hd % 2)) & (lane_c < HEAD_DIM * (hd % 2 + 1)) for hd in heads]
    in_n = [(lane_n >= HEAD_DIM * (hd % 2)) & (lane_n < HEAD_DIM * (hd % 2 + 1)) for hd in heads]
    qms = [jnp.where(in_n[hd], q[:, slab[hd]], jnp.zeros((steps, LANES), BF16)) for hd in heads]
    s1s = [_dot_nt(kc[:, slab[hd]], qms[hd]) - LOG2E * cum_c[:, hd:hd + 1] for hd in heads]
    s2s = [jnp.where(kr <= qc, _dot_nt(kn[:, slab[hd]], qms[hd]) - LOG2E * cum_n[:, hd:hd + 1], -1e30)
           for hd in heads]
    ms = [jnp.maximum(jnp.max(s1, axis=0, keepdims=True), jnp.max(s2, axis=0, keepdims=True))
          for s1, s2 in zip(s1s, s2s)]
    p1s = [jnp.exp2(s1 - m).astype(BF16) for s1, m in zip(s1s, ms)]
    p2s = [jnp.exp2(s2 - m).astype(BF16) for s2, m in zip(s2s, ms)]
    nums = [_dot_tn(p1s[hd], jnp.where(in_c[hd], vc[:, slab[hd]], jnp.zeros((past, LANES), BF16)))
            + _dot_tn(p2s[hd], jnp.where(in_n[hd], vn[:, slab[hd]], jnp.zeros((steps, LANES), BF16))) for hd in heads]
    dens = [_dot_tn(p1s[hd], ones_c) + _dot_tn(p2s[hd], ones_n) for hd in heads]
    outs = [n / d for n, d in zip(nums, dens)]
    o_ref[...] = jnp.concatenate([outs[2 * pr] + outs[2 * pr + 1] for pr in range(HEADS // 2)], axis=1).astype(BF16)


def _fox_sample(q, kn, vn, small, kc, vc, lfc, batch, steps, past):
    blk = CUM_BLOCK
    assert past % blk == 0
    r = np.arange(blk)
    tri = jnp.asarray(r[None, :] <= r[:, None], BF16)
    rn = np.arange(steps)
    trin = jnp.asarray(rn[None, :] <= rn[:, None], BF16)
    row = lambda w: pl.BlockSpec((steps, w), lambda b: (b, 0))
    return pl.pallas_call(
        functools.partial(_fox_sample_kernel, past=past, steps=steps),
        grid=(batch,),
        in_specs=[row(WIDTH), row(WIDTH), row(WIDTH), row(LANES),
                  pl.BlockSpec((1, past, WIDTH), lambda b: (b, 0, 0)),
                  pl.BlockSpec((1, past, WIDTH), lambda b: (b, 0, 0)),
                  pl.BlockSpec((1, past, HEADS), lambda b: (b, 0, 0)),
                  _const_spec((blk, blk)), _const_spec((steps, steps))],
        out_specs=row(WIDTH),
        out_shape=jax.ShapeDtypeStruct((batch * steps, WIDTH), BF16),
        compiler_params=pltpu.CompilerParams(dimension_semantics=("arbitrary",), vmem_limit_bytes=VMEM_LIMIT),
        name="fox_sample",
    )(q, kn, vn, small, kc, vc, lfc, tri, trin)


def _gdn_kernel(gq_ref, small_ref, hist_ref, s0_ref, convw_ref, tri_ref, hsum_ref, hexp_g_ref, hexp_b_ref,
                o_ref, sfin_ref, xbuf, state, *, chunk):
    c_idx = pl.program_id(1)
    n_chunks = pl.num_programs(1)
    hist_rows = SUBLANES

    @pl.when(c_idx == 0)
    def _():
        xbuf[0:hist_rows, :] = hist_ref[0]
        state[...] = s0_ref[0]

    xbuf[hist_rows:hist_rows + chunk, :] = gq_ref[...]
    w = convw_ref[...]
    base = hist_rows - (CONV_WIDTH - 1)
    y = xbuf[base:base + chunk, :] * w[0:1, :]
    for i in range(1, CONV_WIDTH):
        y = y + xbuf[base + i:base + i + chunk, :] * w[i:i + 1, :]
    xbuf[0:hist_rows, :] = xbuf[chunk:chunk + hist_rows, :]
    y = y * _sigmoid(y)

    hsum = hsum_ref[...]
    hexp_g = hexp_g_ref[...]
    hexp_b = hexp_b_ref[...]

    def l2n(x):
        inv = lax.rsqrt(_dot_sel(x * x, hsum, terms=2) + EPS)
        return x * _dot_sel(inv, hexp_g, terms=2)

    q = l2n(y[:, 0:WIDTH]) * (HEAD_DIM ** -0.5)
    k = l2n(y[:, WIDTH:2 * WIDTH])
    v = y[:, 2 * WIDTH:3 * WIDTH]

    small = small_ref[...]
    gc = _sel_dot(tri_ref[...], small)
    gc_x = _dot_sel(gc, hexp_g)
    beta_x = _dot_sel(small, hexp_b, terms=2)
    egc_x = jnp.exp(gc_x)
    glast_x = gc_x[chunk - 1:chunk, :]
    eglast_x = jnp.exp(glast_x)
    kb = k * beta_x
    vb = v * beta_x
    kbe = kb * egc_x
    qe = q * egc_x
    kdec = k * jnp.exp(glast_x - gc_x)

    pad_rows = LANES - chunk
    gc_sq = jnp.concatenate([gc, jnp.zeros((pad_rows, LANES), F32)], axis=0) if pad_rows else gc
    gc_t = gc_sq.T

    ri = lax.broadcasted_iota(jnp.int32, (chunk, chunk), 0)
    ci = lax.broadcasted_iota(jnp.int32, (chunk, chunk), 1)
    eye = jnp.where(ri == ci, 1.0, 0.0).astype(F32)
    n_double = int(np.log2(chunk)) - 1
    assert 2 ** (n_double + 1) == chunk

    heads = range(HEADS)
    sls = [slice(hd * HEAD_DIM, (hd + 1) * HEAD_DIM) for hd in heads]
    decs = [jnp.where(ri >= ci, jnp.exp(gc[:, L_G + hd:L_G + hd + 1] - gc_t[L_G + hd:L_G + hd + 1, 0:chunk]), 0.0)
            for hd in heads]
    k_hs = [k[:, sl].astype(BF16) for sl in sls]
    pws = [-(_dot_nt(kb[:, sl].astype(BF16), k_h) * jnp.where(ri > ci, dec, 0.0))
           for sl, k_h, dec in zip(sls, k_hs, decs)]
    qks = [(_dot_nt(q[:, sl].astype(BF16), k_h) * dec).astype(BF16) for sl, k_h, dec in zip(sls, k_hs, decs)]
    tinvs = [eye + pw for pw in pws]
    for _ in range(n_double):
        pws = [pw.astype(BF16) for pw in pws]
        pws = [_dot(pw, pw) for pw in pws]
        tinvs = [tinv + _dot(tinv.astype(BF16), pw.astype(BF16)) for tinv, pw in zip(tinvs, pws)]
    uws = [_dot(tinv.astype(BF16), jnp.concatenate([vb[:, sl], kbe[:, sl]], axis=1).astype(BF16))
           for tinv, sl in zip(tinvs, sls)]
    s_fs = [state[hd] for hd in heads]
    s_bs = [s_f.astype(BF16) for s_f in s_fs]
    v_news = [(uw[:, 0:HEAD_DIM] - _dot(uw[:, HEAD_DIM:2 * HEAD_DIM].astype(BF16), s_b)).astype(BF16)
              for uw, s_b in zip(uws, s_bs)]
    outs = [_dot(qe[:, sl].astype(BF16), s_b) + _dot(qk, v_new)
            for sl, s_b, qk, v_new in zip(sls, s_bs, qks, v_news)]
    for hd in heads:
        state[hd] = s_fs[hd] * eglast_x[:, sls[hd]] + _dot_tn(kdec[:, sls[hd]].astype(BF16), v_news[hd])
    o_ref[...] = jnp.concatenate(outs, axis=1)

    @pl.when(c_idx == n_chunks - 1)
    def _():
        sfin_ref[0] = state[...]


def _gdn(gqkv, small, hist, s0, conv_w, batch, seq, chunk):
    assert seq % chunk == 0 and chunk % SUBLANES == 0 and chunk <= LANES
    nc = seq // chunk
    r = np.arange(chunk)
    tri = jnp.asarray(r[None, :] <= r[:, None], BF16)
    head_of = np.arange(WIDTH) // HEAD_DIM
    hsum = jnp.asarray(head_of[:, None] == (np.arange(LANES)[None, :] - L_G), BF16)
    hexp_g = jnp.asarray((np.arange(LANES)[:, None] - L_G) == head_of[None, :], BF16)
    hexp_b = jnp.asarray((np.arange(LANES)[:, None] - L_BETA) == head_of[None, :], BF16)
    conv_w8 = jnp.concatenate([conv_w, jnp.zeros((SUBLANES - CONV_WIDTH, CONV_DIM), F32)], axis=0)
    return pl.pallas_call(
        functools.partial(_gdn_kernel, chunk=chunk),
        grid=(batch, nc),
        in_specs=[pl.BlockSpec((chunk, CONV_DIM), lambda b, c: (b * nc + c, 0)),
                  pl.BlockSpec((chunk, LANES), lambda b, c: (b * nc + c, 0)),
                  pl.BlockSpec((1, SUBLANES, CONV_DIM), lambda b, c: (b, 0, 0)),
                  pl.BlockSpec((1, HEADS, HEAD_DIM, HEAD_DIM), lambda b, c: (b, 0, 0, 0)),
                  _const_spec((SUBLANES, CONV_DIM)), _const_spec((chunk, chunk)),
                  _const_spec((WIDTH, LANES)), _const_spec((LANES, WIDTH)), _const_spec((LANES, WIDTH))],
        out_specs=(pl.BlockSpec((chunk, WIDTH), lambda b, c: (b * nc + c, 0)),
                   pl.BlockSpec((1, HEADS, HEAD_DIM, HEAD_DIM), lambda b, c: (b, 0, 0, 0))),
        out_shape=(jax.ShapeDtypeStruct((batch * seq, WIDTH), F32),
                   jax.ShapeDtypeStruct((batch, HEADS, HEAD_DIM, HEAD_DIM), F32)),
        scratch_shapes=[pltpu.VMEM((chunk + SUBLANES, CONV_DIM), F32),
                        pltpu.VMEM((HEADS, HEAD_DIM, HEAD_DIM), F32)],
        compiler_params=pltpu.CompilerParams(dimension_semantics=("arbitrary", "arbitrary"),
                                             vmem_limit_bytes=VMEM_LIMIT),
        name="gdn",
    )(gqkv, small, hist, s0, conv_w8, tri, hsum, hexp_g, hexp_b)


GROUP = 4
GROUP_W = GROUP * HEAD_DIM


def _stack_heads(x, lane):
    return jnp.concatenate([jnp.where(lane == hh, x, jnp.zeros_like(x)) for hh in range(GROUP)], axis=0)


def _fold_heads(x, rows):
    out = x[0:rows]
    for hh in range(1, GROUP):
        out = out + x[hh * rows:(hh + 1) * rows]
    return out


def _gdn_local_kernel(gq_ref, prev_ref, small_ref, convw_ref, tri_ref, hsum_ref, hexp_g_ref, hexp_b_ref, dexp_ref,
                      u_ref, w_ref, qe_ref, kdec_ref, qkw_ref, egl_ref, xbuf, *, chunk):
    i = pl.program_id(1)
    tm = gq_ref.shape[0]
    hist_rows = SUBLANES
    prev = prev_ref[...]
    xbuf[0:hist_rows, :] = jnp.where(i == 0, jnp.zeros_like(prev), prev)
    xbuf[hist_rows:hist_rows + tm, :] = gq_ref[...]
    w = convw_ref[...]
    base = hist_rows - (CONV_WIDTH - 1)
    y = xbuf[base:base + tm, :] * w[0:1, :]
    for t in range(1, CONV_WIDTH):
        y = y + xbuf[base + t:base + t + tm, :] * w[t:t + 1, :]
    y = y * _sigmoid(y)

    hsum = hsum_ref[...]
    hexp_g = hexp_g_ref[...]
    hexp_b = hexp_b_ref[...]

    def l2n(x):
        inv = lax.rsqrt(_dot_sel(x * x, hsum, terms=2) + EPS)
        return x * _dot_sel(inv, hexp_g, terms=2)

    q = l2n(y[:, 0:WIDTH]) * (HEAD_DIM ** -0.5)
    k = l2n(y[:, WIDTH:2 * WIDTH])
    v = y[:, 2 * WIDTH:3 * WIDTH]

    small = small_ref[...]
    gc = _sel_dot(tri_ref[...], small)
    gc_x = _dot_sel(gc, hexp_g)
    beta_x = _dot_sel(small, hexp_b, terms=2)
    n_chunks = tm // chunk
    lasts = [gc_x[(c + 1) * chunk - 1:(c + 1) * chunk, :] for c in range(n_chunks)]
    glast_x = jnp.concatenate([jnp.broadcast_to(r, (chunk, WIDTH)) for r in lasts], axis=0)
    egc_x = jnp.exp(gc_x)
    kb = k * beta_x
    vb = (v * beta_x).astype(BF16)
    kbe = (kb * egc_x).astype(BF16)
    kb = kb.astype(BF16)
    qe_ref[...] = (q * egc_x).astype(BF16)
    kdec_ref[...] = (k * jnp.exp(glast_x - gc_x)).astype(BF16)
    for c in range(n_chunks):
        egl_ref[c] = jnp.broadcast_to(jnp.exp(lasts[c]), (SUBLANES, WIDTH))
    q = q.astype(BF16)
    k = k.astype(BF16)

    rows = GROUP * chunk
    lane = lax.broadcasted_iota(jnp.int32, (chunk, GROUP_W), 1)
    head_of_lane = lane // HEAD_DIM
    ri = lax.broadcasted_iota(jnp.int32, (rows, rows), 0)
    ci = lax.broadcasted_iota(jnp.int32, (rows, rows), 1)
    same = (ri // chunk) == (ci // chunk)
    incl = same & (ri >= ci)
    strict = same & (ri > ci)
    eye = jnp.where(ri == ci, 1.0, 0.0).astype(F32)
    n_double = int(np.log2(chunk)) - 1

    placed = None
    for t, part in enumerate(_split3(gc)):
        term = _dot(part, dexp_ref[t])
        placed = term if placed is None else placed + term
    lane_d = lax.broadcasted_iota(jnp.int32, (chunk, LANES), 1)
    head_d = jnp.where(lane_d < DECAY_LANES * HEADS, lane_d // DECAY_LANES, -1)
    used = lane_d[0:1, :] < DECAY_LANES * HEADS
    hi_half = lane_d[0:1, :] % DECAY_LANES >= DECAY_LANES // 2
    diff_l = (placed[:, 0:LANES] + jnp.where(used & hi_half, 1.0, 0.0)).astype(BF16)
    diff_r = (jnp.where(used & ~hi_half, 1.0, 0.0) - placed[:, LANES:2 * LANES]).astype(BF16)

    probs = [(slice(c * chunk, (c + 1) * chunk), slice(g * GROUP_W, (g + 1) * GROUP_W))
             for c in range(n_chunks) for g in range(HEADS // GROUP)]
    stack = lambda x, pr: _stack_heads(x[pr[0], pr[1]], head_of_lane)
    stack_d = lambda x, c, g: jnp.concatenate(
        [jnp.where(head_d == GROUP * g + hh, x[c * chunk:(c + 1) * chunk], jnp.zeros((chunk, LANES), BF16))
         for hh in range(GROUP)], axis=0)
    diffs = [_dot_nt(stack_d(diff_l, c, g), stack_d(diff_r, c, g))
             for c in range(n_chunks) for g in range(HEADS // GROUP)]
    aqs = [_dot_nt(jnp.concatenate([stack(kb, pr), stack(q, pr)], axis=0), stack(k, pr)) for pr in probs]
    decays = [jnp.exp(jnp.where(incl, d, -1e30)) for d in diffs]
    for pr, aq, decay in zip(probs, aqs, decays):
        qkw_ref[pr[0], pr[1]] = _fold_heads(aq[rows:2 * rows] * decay, chunk).astype(BF16)
    pws = [-(aq[0:rows] * jnp.where(strict, decay, 0.0)) for aq, decay in zip(aqs, decays)]
    tinvs = [eye + pw for pw in pws]
    pws = [pw.astype(BF16) for pw in pws]
    pws = [_dot(pw, pw).astype(BF16) for pw in pws]
    for it in range(n_double):
        if it < n_double - 1:
            both = [_dot(jnp.concatenate([pw, tinv.astype(BF16)], axis=0), pw) for pw, tinv in zip(pws, tinvs)]
            pws = [b[0:rows].astype(BF16) for b in both]
            tinvs = [tinv + b[rows:2 * rows] for tinv, b in zip(tinvs, both)]
        else:
            tinvs = [tinv + _dot(tinv.astype(BF16), pw) for pw, tinv in zip(pws, tinvs)]
    uws = [_dot(tinv.astype(BF16), jnp.concatenate([stack(vb, pr), stack(kbe, pr)], axis=1))
           for pr, tinv in zip(probs, tinvs)]
    for pr, uw in zip(probs, uws):
        u_ref[pr[0], pr[1]] = _fold_heads(uw[:, 0:GROUP_W], chunk).astype(BF16)
        w_ref[pr[0], pr[1]] = _fold_heads(uw[:, GROUP_W:2 * GROUP_W], chunk).astype(BF16)


def _gdn_scan_kernel(u_ref, w_ref, qe_ref, kdec_ref, qkw_ref, egl_ref, o_ref, sfin_ref, state, *, chunk):
    c_idx = pl.program_id(0)
    batch = u_ref.shape[0]

    @pl.when(c_idx == 0)
    def _():
        state[...] = jnp.zeros_like(state)

    rows = GROUP * chunk
    lane = lax.broadcasted_iota(jnp.int32, (chunk, GROUP_W), 1) // HEAD_DIM
    ri = lax.broadcasted_iota(jnp.int32, (GROUP_W, GROUP_W), 0)
    ci = lax.broadcasted_iota(jnp.int32, (GROUP_W, GROUP_W), 1)
    same = (ri // HEAD_DIM) == (ci // HEAD_DIM)
    probs = [(b, g, slice(g * GROUP_W, (g + 1) * GROUP_W)) for b in range(batch) for g in range(HEADS // GROUP)]
    s_fs = [state[b, g] for b, g, _ in probs]
    s_bs = [s_f.astype(BF16) for s_f in s_fs]
    v_news = [(u_ref[b, :, ls].astype(F32) - _dot(w_ref[b, :, ls], s_b)).astype(BF16)
              for (b, _, ls), s_b in zip(probs, s_bs)]
    for (b, _, ls), s_b, v_new in zip(probs, s_bs, v_news):
        o_ref[b, :, ls] = _dot(jnp.concatenate([qe_ref[b, :, ls], qkw_ref[b, :, ls]], axis=1),
                               jnp.concatenate([s_b, _stack_heads(v_new, lane)], axis=0))
    kvs = [_dot_tn(kdec_ref[b, :, ls], v_new) for (b, _, ls), v_new in zip(probs, v_news)]
    for (b, g, ls), s_f, kv in zip(probs, s_fs, kvs):
        state[b, g] = s_f * egl_ref[b, 0, 0:1, ls] + jnp.where(same, kv, 0.0)

    @pl.when(c_idx == pl.num_programs(0) - 1)
    def _():
        for b in range(batch):
            for hd in range(HEADS):
                g, hh = divmod(hd, GROUP)
                sl = slice(hh * HEAD_DIM, (hh + 1) * HEAD_DIM)
                sfin_ref[b, hd] = state[b, g, sl, sl]


def _gdn_long(gqkv, small, conv_w, batch, seq, chunk):
    tm = TOKEN_TILE
    assert seq % tm == 0 and tm % chunk == 0 and GROUP * chunk == GROUP_W
    nt = seq // tm
    nc = seq // chunk
    r = np.arange(tm)
    tri = jnp.asarray((r[None, :] <= r[:, None]) & ((r[None, :] // chunk) == (r[:, None] // chunk)), BF16)
    head_of = np.arange(WIDTH) // HEAD_DIM
    hsum = jnp.asarray(head_of[:, None] == (np.arange(LANES)[None, :] - L_G), BF16)
    hexp_g = jnp.asarray((np.arange(LANES)[:, None] - L_G) == head_of[None, :], BF16)
    hexp_b = jnp.asarray((np.arange(LANES)[:, None] - L_BETA) == head_of[None, :], BF16)
    conv_w8 = jnp.concatenate([conv_w, jnp.zeros((SUBLANES - CONV_WIDTH, CONV_DIM), F32)], axis=0)
    dexp = np.zeros((3, LANES, 2 * LANES), np.float32)
    for t in range(3):
        for hd in range(HEADS):
            dexp[t, L_G + hd, DECAY_LANES * hd + t] = 1.0
            dexp[t, L_G + hd, LANES + DECAY_LANES * hd + DECAY_LANES // 2 + t] = 1.0
    dexp = jnp.asarray(dexp, BF16)
    row = lambda w: pl.BlockSpec((tm, w), lambda b, i: (b * nt + i, 0))
    per_tile = tm // chunk
    tok = jax.ShapeDtypeStruct((batch * seq, WIDTH), BF16)
    u, w, qe, kdec, qkw, egl = pl.pallas_call(
        functools.partial(_gdn_local_kernel, chunk=chunk),
        grid=(batch, nt),
        in_specs=[row(CONV_DIM),
                  pl.BlockSpec((SUBLANES, CONV_DIM),
                               lambda b, i: (jnp.maximum((b * nt + i) * (tm // SUBLANES) - 1, 0), 0)),
                  row(LANES), _const_spec((SUBLANES, CONV_DIM)), _const_spec((tm, tm)),
                  _const_spec((WIDTH, LANES)), _const_spec((LANES, WIDTH)), _const_spec((LANES, WIDTH)),
                  _const_spec((3, LANES, 2 * LANES))],
        out_specs=(row(WIDTH), row(WIDTH), row(WIDTH), row(WIDTH), row(WIDTH),
                   pl.BlockSpec((per_tile, SUBLANES, WIDTH), lambda b, i: (b * nt + i, 0, 0))),
        out_shape=(tok, tok, tok, tok, tok, jax.ShapeDtypeStruct((batch * nc, SUBLANES, WIDTH), F32)),
        scratch_shapes=[pltpu.VMEM((tm + SUBLANES, CONV_DIM), F32)],
        compiler_params=pltpu.CompilerParams(dimension_semantics=("arbitrary", "arbitrary"),
                                             vmem_limit_bytes=VMEM_LIMIT),
        name="gdn_local",
    )(gqkv, gqkv, small, conv_w8, tri, hsum, hexp_g, hexp_b, dexp)

    blk = pl.BlockSpec((batch, chunk, WIDTH), lambda c: (0, c, 0))
    as3 = lambda a: a.reshape(batch, seq, WIDTH)
    o, s_fin = pl.pallas_call(
        functools.partial(_gdn_scan_kernel, chunk=chunk),
        grid=(nc,),
        in_specs=[blk, blk, blk, blk, blk,
                  pl.BlockSpec((batch, 1, SUBLANES, WIDTH), lambda c: (0, c, 0, 0))],
        out_specs=(blk, pl.BlockSpec((batch, HEADS, HEAD_DIM, HEAD_DIM), lambda c: (0, 0, 0, 0))),
        out_shape=(jax.ShapeDtypeStruct((batch, seq, WIDTH), F32),
                   jax.ShapeDtypeStruct((batch, HEADS, HEAD_DIM, HEAD_DIM), F32)),
        scratch_shapes=[pltpu.VMEM((batch, HEADS // GROUP, GROUP_W, GROUP_W), F32)],
        compiler_params=pltpu.CompilerParams(dimension_semantics=("arbitrary",), vmem_limit_bytes=VMEM_LIMIT),
        name="gdn_scan",
    )(as3(u), as3(w), as3(qe), as3(kdec), as3(qkw), egl.reshape(batch, nc, SUBLANES, WIDTH))
    return o.reshape(batch * seq, WIDTH), s_fin


def _post_kernel(x_ref, fox_ref, gdn_ref, sgg_ref, sga_ref, sgb_ref,
                 wpa_ref, wpb_ref, wout_ref, wup_ref, wdown_ref,
                 ng_ref, gpost_ref, gpre2_ref, gpost2_ref, hsum_ref, hexp_ref, y_ref):
    ya = _dot(fox_ref[...], wpa_ref[...])
    o = gdn_ref[...]
    ms = _dot_sel(o * o, hsum_ref[...], terms=2) * (1.0 / HEAD_DIM)
    o = o * _dot_sel(lax.rsqrt(ms + EPS), hexp_ref[...], terms=2) * ng_ref[...] * sgg_ref[...].astype(F32)
    yb = _dot(o.astype(BF16), wpb_ref[...])
    m = sga_ref[...].astype(F32) * ya + sgb_ref[...].astype(F32) * yb
    mix = _dot(m.astype(BF16), wout_ref[...])
    y1 = x_ref[...] + _rms(mix, gpost_ref[...])
    h2 = _rms(y1, gpre2_ref[...]).astype(BF16)
    acc = jnp.zeros(y1.shape, F32)
    for c in range(D_FF // FF_TILE):
        u = jnp.maximum(_dot(h2, wup_ref[:, c * FF_TILE:(c + 1) * FF_TILE]), 0.0)
        acc = acc + _dot((u * u).astype(BF16), wdown_ref[c * FF_TILE:(c + 1) * FF_TILE, :])
    y_ref[...] = y1 + _rms(acc, gpost2_ref[...])


def _post(x, fox, gdn_o, sgg, sga, sgb, w_pa, w_pb, w_out, w_up, w_down, ng, gpost, gpre2, gpost2):
    t_total = x.shape[0]
    tm = min(POST_TILE, t_total)
    assert t_total % tm == 0
    row = lambda w: pl.BlockSpec((tm, w), lambda i: (i, 0))
    head_of = np.arange(WIDTH) // HEAD_DIM
    hsum = jnp.asarray(head_of[:, None] == (np.arange(LANES)[None, :] - L_G), BF16)
    hexp = jnp.asarray((np.arange(LANES)[:, None] - L_G) == head_of[None, :], BF16)
    return pl.pallas_call(
        _post_kernel,
        grid=(t_total // tm,),
        in_specs=[row(D_MODEL), row(WIDTH), row(WIDTH), row(WIDTH), row(D_MODEL), row(D_MODEL),
                  _const_spec((WIDTH, D_MODEL)), _const_spec((WIDTH, D_MODEL)), _const_spec((D_MODEL, D_MODEL)),
                  _const_spec((D_MODEL, D_FF)), _const_spec((D_FF, D_MODEL)),
                  _const_spec((1, WIDTH)), _const_spec((1, D_MODEL)), _const_spec((1, D_MODEL)),
                  _const_spec((1, D_MODEL)), _const_spec((WIDTH, LANES)), _const_spec((LANES, WIDTH))],
        out_specs=row(D_MODEL),
        out_shape=jax.ShapeDtypeStruct((t_total, D_MODEL), F32),
        compiler_params=pltpu.CompilerParams(dimension_semantics=("arbitrary",), vmem_limit_bytes=VMEM_LIMIT),
        name="post",
    )(x, fox, gdn_o, sgg, sga, sgb, w_pa, w_pb, w_out, w_up, w_down, ng, gpost, gpre2, gpost2, hsum, hexp)


def _rearrange_w_in(w_in):
    o_ff = 3 * WIDTH
    o_gqkv = o_ff + HEADS
    o_ga = o_gqkv + CONV_DIM
    o_gb = o_ga + HEADS
    o_gg = o_gb + HEADS
    o_gate_a = o_gg + WIDTH
    o_gate_b = o_gate_a + D_MODEL
    w_in = w_in.astype(BF16)
    cols = [w_in[:, 0:o_ff], w_in[:, o_gqkv:o_ga], w_in[:, o_gg:o_gate_a], w_in[:, o_gate_a:o_gate_b],
            w_in[:, o_gate_b:o_gate_b + D_MODEL], w_in[:, o_ff:o_gqkv], w_in[:, o_ga:o_gb], w_in[:, o_gb:o_gg],
            jnp.zeros((D_MODEL, LANES - 3 * HEADS), BF16)]
    return jnp.concatenate(cols, axis=1)


def _lane_row(*pieces):
    v = jnp.concatenate([p.astype(F32) for p in pieces])
    return jnp.concatenate([v, jnp.zeros((LANES - v.shape[0],), F32)])[None, :]


def _pad_hist(conv_cache):
    b = conv_cache.shape[0]
    return jnp.concatenate([jnp.zeros((b, SUBLANES - (CONV_WIDTH - 1), CONV_DIM), F32), conv_cache.astype(F32)], axis=1)


def kernel(x_prompt, x_sample, cache_fox_k, cache_fox_v, cache_fox_logf, state_gdn, state_gdn_conv,
           w_in, fox_forget_bias, gdn_conv_w, gdn_a_log, gdn_dt_bias, gdn_norm_g,
           w_proj_fox, w_proj_gdn, w_out, norm_mix_pre, norm_mix_post, norm_mlp_pre, norm_mlp_post,
           w_up, w_down):
    depth = w_in.shape[0]
    bp, sp, _ = x_prompt.shape
    bs, ss, _ = x_sample.shape
    past = cache_fox_k.shape[2]
    y_p = x_prompt.reshape(bp * sp, D_MODEL)
    y_s = x_sample.reshape(bs * ss, D_MODEL)
    st_p, st_s = [], []
    for l in range(depth):
        w_r = _rearrange_w_in(w_in[l])
        padd = _lane_row(fox_forget_bias[l], gdn_dt_bias[l])
        palog = _lane_row(jnp.zeros((HEADS,), F32), gdn_a_log[l])
        gain = norm_mix_pre[l][None, :]
        post_w = (w_proj_fox[l].astype(BF16), w_proj_gdn[l].astype(BF16), w_out[l].astype(BF16),
                  w_up[l].astype(BF16), w_down[l].astype(BF16),
                  jnp.tile(gdn_norm_g[l], HEADS)[None, :], norm_mix_post[l][None, :],
                  norm_mlp_pre[l][None, :], norm_mlp_post[l][None, :])

        q, k, vt, fk, fv, gqkv, sgg, sga, sgb, small, cum = _in_proj(y_p, gain, w_r, padd, palog, sp)
        fox = _fox_prompt(q, k, vt, cum, bp, sp)
        gdn_o, s_fin = _gdn_long(gqkv, small, gdn_conv_w[l], bp, sp, GDN_CHUNK)
        y_p_new = _post(y_p, fox, gdn_o, sgg, sga, sgb, *post_w)
        st_p.append((fk.reshape(bp, sp, HEADS, HEAD_DIM), fv.reshape(bp, sp, HEADS, HEAD_DIM),
                     small[:, L_LOGF:L_LOGF + HEADS].reshape(bp, sp, HEADS), s_fin,
                     gqkv.reshape(bp, sp, CONV_DIM)[:, sp - (CONV_WIDTH - 1):, :]))
        y_p = y_p_new

        q, k, vt, fk, fv, gqkv, sgg, sga, sgb, small, cum = _in_proj(y_s, gain, w_r, padd, palog, ss)
        fox = _fox_sample(q, fk.reshape(bs * ss, WIDTH), fv.reshape(bs * ss, WIDTH), small,
                          cache_fox_k[l].reshape(bs, past, WIDTH), cache_fox_v[l].reshape(bs, past, WIDTH),
                          cache_fox_logf[l], bs, ss, past)
        gdn_o, s_new = _gdn(gqkv, small, _pad_hist(state_gdn_conv[l]), state_gdn[l].astype(F32),
                            gdn_conv_w[l], bs, ss, ss)
        y_s_new = _post(y_s, fox, gdn_o, sgg, sga, sgb, *post_w)
        conv_ext = jnp.concatenate([state_gdn_conv[l].astype(F32), gqkv.reshape(bs, ss, CONV_DIM)], axis=1)
        st_s.append((fk.reshape(bs, ss, HEADS, HEAD_DIM), fv.reshape(bs, ss, HEADS, HEAD_DIM),
                     small[:, L_LOGF:L_LOGF + HEADS].reshape(bs, ss, HEADS), s_new,
                     conv_ext[:, conv_ext.shape[1] - (CONV_WIDTH - 1):, :]))
        y_s = y_s_new

    fk_p, fv_p, lf_p, sg_p, cv_p = [jnp.stack(a) for a in zip(*st_p)]
    fk_s, fv_s, lf_s, sg_s, cv_s = [jnp.stack(a) for a in zip(*st_s)]
    return (y_p.reshape(bp, sp, D_MODEL), y_s.reshape(bs, ss, D_MODEL),
            fk_p, fv_p, lf_p, sg_p, cv_p, fk_s, fv_s, lf_s, sg_s, cv_s)
```

```python
import functools

import jax
import jax.numpy as jnp
import numpy as np
from jax import lax
from jax.experimental import pallas as pl
from jax.experimental.pallas import tpu as pltpu

F32 = jnp.float32
BF16 = jnp.bfloat16

D_MODEL = 1024
HEADS = 8
HEAD_DIM = 64
WIDTH = HEADS * HEAD_DIM
CONV_DIM = 3 * WIDTH
CONV_WIDTH = 4
D_FF = 4 * D_MODEL
EPS = 1e-6
LOG2E = 1.4426950408889634

LANES = 128
SUBLANES = 8
TOKEN_TILE = 256
POST_TILE = 512
ATTN_TILE = 256
ATTN_HEADS = 8
KEY_TILE = 512
FF_TILE = 1024
GDN_CHUNK = 64
PREP_ROWS = 512
CUM_BLOCK = 256
DECAY_LANES = 6
VMEM_LIMIT = 60 * 1024 * 1024

C_Q, C_K, C_V, C_GQKV, C_GG, C_GA, C_GB, C_SMALL = 0, 512, 1024, 1536, 3072, 3584, 4608, 5632
IN_COLS = C_SMALL + LANES
L_LOGF, L_G, L_BETA = 0, 8, 16


def _split3(x):
    hi = x.astype(BF16)
    r = x - hi.astype(F32)
    mid = r.astype(BF16)
    lo = (r - mid.astype(F32)).astype(BF16)
    return hi, mid, lo


def _dot(a, b):
    return jnp.dot(a, b, preferred_element_type=F32)


def _dot_nt(a, b):
    return lax.dot_general(a, b, (((1,), (1,)), ((), ())), preferred_element_type=F32)


def _dot_tn(a, b):
    return lax.dot_general(a, b, (((0,), (0,)), ((), ())), preferred_element_type=F32)


def _sel_dot(sel, x, terms=3):
    parts = _split3(x)[:terms]
    n = x.shape[1]
    if n % LANES:
        return sum(_dot(sel, p) for p in parts)
    wide = _dot(sel, jnp.concatenate(parts, axis=1))
    return sum(wide[:, t * n:(t + 1) * n] for t in range(terms))


def _dot_sel(x, sel, terms=3):
    parts = _split3(x)[:terms]
    if terms == 1:
        return _dot(parts[0], sel)
    out = _dot(jnp.concatenate(parts[0:2], axis=1), jnp.concatenate([sel, sel], axis=0))
    for p in parts[2:]:
        out = out + _dot(p, sel)
    return out


def _sigmoid(x):
    return 1.0 / (1.0 + jnp.exp(-x))


def _rms(x, gain):
    ms = jnp.mean(x * x, axis=-1, keepdims=True)
    return x * lax.rsqrt(ms + EPS) * gain


def _const_spec(shape):
    nd = len(shape)
    return pl.BlockSpec(shape, lambda *_: (0,) * nd, pipeline_mode=pl.Buffered(1))


def _inproj_kernel(x_ref, gain_ref, w_ref, padd_ref, palog_ref, tri_ref,
                   q_ref, k_ref, vt_ref, fk_ref, fv_ref, gqkv_ref, sgg_ref, sga_ref, sgb_ref,
                   small_ref, cum_ref, carry_ref, *, tiles_per_seg):
    i = pl.program_id(0)
    h = _rms(x_ref[...], gain_ref[...]).astype(BF16)

    def mm(c0, width):
        return _dot(h, w_ref[:, c0:c0 + width])

    zq = mm(C_Q, WIDTH)
    q_ref[...] = (zq * (LOG2E * HEAD_DIM ** -0.5)).astype(BF16)
    zk = mm(C_K, WIDTH)
    k_ref[...] = zk.astype(BF16)
    zv = mm(C_V, WIDTH)
    fk_ref[...] = zk.reshape(fk_ref.shape)
    fv_ref[...] = zv.reshape(fv_ref.shape)
    vt_ref[0] = zv.T.astype(BF16)
    for c in range(3):
        gqkv_ref[:, c * WIDTH:(c + 1) * WIDTH] = mm(C_GQKV + c * WIDTH, WIDTH)
    zg = mm(C_GG, WIDTH)
    sgg_ref[...] = (zg * _sigmoid(zg)).astype(BF16)
    for c in range(2):
        sga_ref[:, c * WIDTH:(c + 1) * WIDTH] = _sigmoid(mm(C_GA + c * WIDTH, WIDTH)).astype(BF16)
        sgb_ref[:, c * WIDTH:(c + 1) * WIDTH] = _sigmoid(mm(C_GB + c * WIDTH, WIDTH)).astype(BF16)

    zs = mm(C_SMALL, LANES)
    t = zs + padd_ref[...]
    l1p = jnp.log1p(jnp.exp(-jnp.abs(t)))
    logf = jnp.minimum(t, 0.0) - l1p
    g = -jnp.exp(palog_ref[...]) * (jnp.maximum(t, 0.0) + l1p)
    beta = _sigmoid(zs)
    lane = lax.broadcasted_iota(jnp.int32, zs.shape, 1)
    small = jnp.where(lane < L_G, logf, jnp.where(lane < L_BETA, g, jnp.where(lane < L_BETA + HEADS, beta, 0.0)))
    small_ref[...] = small

    cum = _sel_dot(tri_ref[...], small)
    if tiles_per_seg > 1:
        @pl.when(i % tiles_per_seg == 0)
        def _():
            carry_ref[...] = jnp.zeros_like(carry_ref)
        cum = cum + carry_ref[0:1, :]
        carry_ref[...] = jnp.broadcast_to(cum[-1:, :], carry_ref.shape)
    cum_ref[...] = cum


def _in_proj(x, gain, w_r, padd, palog, seg_len):
    t_total = x.shape[0]
    tm = TOKEN_TILE
    assert t_total % tm == 0
    nt = t_total // tm
    if seg_len >= tm:
        assert seg_len % tm == 0
        tiles_per_seg = seg_len // tm
        r = np.arange(tm)
        tri = (r[None, :] <= r[:, None])
    else:
        assert tm % seg_len == 0
        tiles_per_seg = 1
        r = np.arange(tm)
        tri = (r[None, :] <= r[:, None]) & ((r[None, :] // seg_len) == (r[:, None] // seg_len))
    tri = jnp.asarray(tri, BF16)

    row = lambda w: pl.BlockSpec((tm, w), lambda i: (i, 0))
    out_shape = (
        jax.ShapeDtypeStruct((t_total, WIDTH), BF16),
        jax.ShapeDtypeStruct((t_total, WIDTH), BF16),
        jax.ShapeDtypeStruct((nt, WIDTH, tm), BF16),
        jax.ShapeDtypeStruct((t_total, HEADS, HEAD_DIM), F32),
        jax.ShapeDtypeStruct((t_total, HEADS, HEAD_DIM), F32),
        jax.ShapeDtypeStruct((t_total, CONV_DIM), F32),
        jax.ShapeDtypeStruct((t_total, WIDTH), BF16),
        jax.ShapeDtypeStruct((t_total, D_MODEL), BF16),
        jax.ShapeDtypeStruct((t_total, D_MODEL), BF16),
        jax.ShapeDtypeStruct((t_total, LANES), F32),
        jax.ShapeDtypeStruct((t_total, LANES), F32),
    )
    per_head = pl.BlockSpec((tm, HEADS, HEAD_DIM), lambda i: (i, 0, 0))
    out_specs = (row(WIDTH), row(WIDTH), pl.BlockSpec((1, WIDTH, tm), lambda i: (i, 0, 0)),
                 per_head, per_head, row(CONV_DIM), row(WIDTH), row(D_MODEL), row(D_MODEL),
                 row(LANES), row(LANES))
    return pl.pallas_call(
        functools.partial(_inproj_kernel, tiles_per_seg=tiles_per_seg),
        grid=(nt,),
        in_specs=[row(D_MODEL), _const_spec((1, D_MODEL)), _const_spec((D_MODEL, IN_COLS)),
                  _const_spec((1, LANES)), _const_spec((1, LANES)), _const_spec((tm, tm))],
        out_specs=out_specs,
        out_shape=out_shape,
        scratch_shapes=[pltpu.VMEM((SUBLANES, LANES), F32)],
        compiler_params=pltpu.CompilerParams(dimension_semantics=("arbitrary",), vmem_limit_bytes=VMEM_LIMIT),
        name="in_proj",
    )(x, gain, w_r, padd, palog, tri)


def _fox_prompt_kernel(q_ref, k_ref, vt_ref, cum_ref, o_ref, bias_ref, s0_ref, s1_ref, p0_ref, p1_ref, acc_ref,
                       *, seq, tq, tk, nh):
    grp = pl.program_id(1)
    qi = pl.program_id(2)
    rows = PREP_ROWS
    vt_tile = vt_ref.shape[2]
    sub = tk // vt_tile
    heads = range(nh)
    s_slots = (s0_ref, s1_ref)
    p_slots = (p0_ref, p1_ref)

    @pl.when(qi == 0)
    def _prep():
        def fill(r, carry):
            c = cum_ref[pl.ds(r * rows, rows), :]
            lane = lax.broadcasted_iota(jnp.int32, c.shape, 1)
            slab = jnp.zeros(c.shape, F32)
            for hh in heads:
                col = jnp.sum(jnp.where(lane == nh * grp + hh, c, 0.0), axis=-1, keepdims=True) * (-LOG2E)
                for t, part in enumerate(_split3(col)):
                    slab = jnp.where(lane == 3 * hh + t, part.astype(F32), slab)
            bias_ref[pl.ds(r * rows, rows), :] = slab.astype(BF16)
            return carry
        lax.fori_loop(0, seq // rows, fill, 0)

    lane_q = lax.broadcasted_iota(jnp.int32, (tq, LANES), 1)
    qh = []
    for hh in heads:
        q2 = q_ref[:, (hh // 2) * LANES:(hh // 2 + 1) * LANES]
        half = hh % 2
        q_m = jnp.where((lane_q >= HEAD_DIM * half) & (lane_q < HEAD_DIM * (half + 1)), q2, jnp.zeros_like(q2))
        ones = jnp.where((lane_q >= 3 * hh) & (lane_q < 3 * hh + 3), 1.0, 0.0)
        qh.append(jnp.concatenate([q_m.astype(F32), ones], axis=1).T.astype(BF16))

    def scores_to(slot, j):
        bias = bias_ref[pl.ds(j * tk, tk), :]
        kjs = [jnp.concatenate([k_ref[pl.ds(j * tk, tk), sl * LANES:(sl + 1) * LANES], bias], axis=1)
               for sl in range(nh // 2)]
        s_ts = [_dot(kjs[hh // 2], qh[hh]) for hh in heads]
        for hh in heads:
            s_slots[slot][hh] = s_ts[hh]
        return tuple(jnp.max(s_t, axis=0, keepdims=True) for s_t in s_ts)

    ones_rows = jnp.ones((2 * SUBLANES, vt_tile), BF16)

    def add_values(slot, j, alphas):
        pvs = [sum(_dot(jnp.concatenate([vt_ref[j * sub + t, HEAD_DIM * hh:HEAD_DIM * (hh + 1), :], ones_rows], axis=0),
                        p_slots[slot][hh, t * vt_tile:(t + 1) * vt_tile, :]) for t in range(sub)) for hh in heads]
        for hh in heads:
            acc_ref[hh] = alphas[hh] * acc_ref[hh] + pvs[hh]

    def rescale(ms, bms):
        m_new = tuple(jnp.maximum(ms[hh], bms[hh]) for hh in heads)
        return m_new, tuple(jnp.exp2(ms[hh] - m_new[hh]) for hh in heads)

    def probabilities(slot, hh, m_new):
        p_slots[slot][hh] = jnp.exp2(s_slots[slot][hh] - m_new).astype(BF16)

    def trip(j, rd, carry):
        ms, a_prev, bms = carry
        wr = 1 - rd
        add_values(wr, jnp.maximum(j - 1, 0), a_prev)
        m_new, alphas = rescale(ms, bms)
        for hh in heads:
            probabilities(rd, hh, m_new[hh])
        bm_next = scores_to(wr, j + 1)
        return m_new, alphas, bm_next

    n_full = (qi * tq) // tk

    def last(rd, carry):
        ms, a_prev, _ = carry
        add_values(1 - rd, jnp.maximum(n_full - 1, 0), a_prev)
        kr = lax.broadcasted_iota(jnp.int32, (tk, tq), 0)
        qc = lax.broadcasted_iota(jnp.int32, (tk, tq), 1)
        visible = kr <= qc + (qi * tq - n_full * tk)
        for hh in heads:
            s_slots[rd][hh] = jnp.where(visible, s_slots[rd][hh], -1e30)
        bms = tuple(jnp.max(s_slots[rd][hh], axis=0, keepdims=True) for hh in heads)
        m_new, alphas = rescale(ms, bms)
        for hh in heads:
            probabilities(rd, hh, m_new[hh])
        add_values(rd, n_full, alphas)
        return jnp.concatenate([acc_ref[hh, 0:HEAD_DIM, :] / acc_ref[hh, HEAD_DIM:HEAD_DIM + 1, :] for hh in heads],
                               axis=0)

    each = lambda f: tuple(f() for _ in heads)
    p1_ref[...] = jnp.zeros(p1_ref.shape, BF16)
    acc_ref[...] = jnp.zeros(acc_ref.shape, F32)
    init = (each(lambda: jnp.full((1, tq), -1e30, F32)), each(lambda: jnp.ones((1, tq), F32)), scores_to(0, 0))
    carry = lax.fori_loop(
        0, n_full, lambda j, c: lax.cond(j % 2 == 0, lambda c: trip(j, 0, c), lambda c: trip(j, 1, c), c), init)
    o_t = lax.cond(n_full % 2 == 0, lambda c: last(0, c), lambda c: last(1, c), carry)
    o_ref[...] = o_t.T.astype(BF16)


def _fox_prompt(q, k, vt, cum, batch, seq):
    tq, tk, nh = ATTN_TILE, KEY_TILE, ATTN_HEADS
    assert seq % tk == 0 and tk % tq == 0 and tk % TOKEN_TILE == 0 and HEADS % nh == 0 and nh % 2 == 0
    assert seq % PREP_ROWS == 0
    nq = seq // tq
    width = nh * HEAD_DIM
    return pl.pallas_call(
        functools.partial(_fox_prompt_kernel, seq=seq, tq=tq, tk=tk, nh=nh),
        grid=(batch, HEADS // nh, nq),
        in_specs=[pl.BlockSpec((tq, width), lambda b, g, i: (b * nq + i, g)),
                  pl.BlockSpec((seq, width), lambda b, g, i: (b, g), pipeline_mode=pl.Buffered(1)),
                  pl.BlockSpec((seq // TOKEN_TILE, width, TOKEN_TILE), lambda b, g, i: (b, g, 0),
                               pipeline_mode=pl.Buffered(1)),
                  pl.BlockSpec((seq, LANES), lambda b, g, i: (b, 0), pipeline_mode=pl.Buffered(1))],
        out_specs=pl.BlockSpec((tq, width), lambda b, g, i: (b * nq + i, g)),
        out_shape=jax.ShapeDtypeStruct((batch * seq, WIDTH), BF16),
        scratch_shapes=[pltpu.VMEM((seq, LANES), BF16),
                        pltpu.VMEM((nh, tk, tq), F32), pltpu.VMEM((nh, tk, tq), F32),
                        pltpu.VMEM((nh, tk, tq), BF16), pltpu.VMEM((nh, tk, tq), BF16),
                        pltpu.VMEM((nh, HEAD_DIM + 2 * SUBLANES, tq), F32)],
        compiler_params=pltpu.CompilerParams(dimension_semantics=("arbitrary",) * 3, vmem_limit_bytes=VMEM_LIMIT),
        name="fox_prompt",
    )(q, k, vt, cum)


def _fox_sample_kernel(q_ref, kn_ref, vn_ref, small_ref, kc_ref, vc_ref, lfc_ref, tri_ref, trin_ref, o_ref,
                       *, past, steps):
    blk = tri_ref.shape[0]
    carry = jnp.zeros((1, HEADS), F32)
    cums = []
    for r in range(past // blk):
        c = _sel_dot(tri_ref[...], lfc_ref[0, r * blk:(r + 1) * blk, :]) + carry
        cums.append(c)
        carry = c[-1:, :]
    cum_c = jnp.concatenate(cums, axis=0)
    cum_n = _sel_dot(trin_ref[...], small_ref[:, L_LOGF:L_LOGF + HEADS]) + carry

    q = q_ref[...]
    kc = kc_ref[0].astype(BF16)
    vc = vc_ref[0].astype(BF16)
    kn = kn_ref[...].astype(BF16)
    vn = vn_ref[...].astype(BF16)
    lane_c = lax.broadcasted_iota(jnp.int32, (past, LANES), 1)
    lane_n = lax.broadcasted_iota(jnp.int32, (steps, LANES), 1)
    kr = lax.broadcasted_iota(jnp.int32, (steps, steps), 0)
    qc = lax.broadcasted_iota(jnp.int32, (steps, steps), 1)
    ones_c = jnp.ones((past, LANES), BF16)
    ones_n = jnp.ones((steps, LANES), BF16)
    heads = range(HEADS)
    slab = [slice((hd // 2) * LANES, (hd // 2 + 1) * LANES) for hd in heads]
    in_c = [(lane_c >= HEAD_DIM * (hd % 2)) & (lane_c < HEAD_DIM * (hd % 2 + 1)) for hd in heads]
    in_n = [(lane_n >= HEAD_DIM * (hd % 2)) & (lane_n < HEAD_DIM * (hd % 2 + 1)) for hd in heads]
    qms = [jnp.where(in_n[hd], q[:, slab[hd]], jnp.zeros((steps, LANES), BF16)) for hd in heads]
    s1s = [_dot_nt(kc[:, slab[hd]], qms[hd]) - LOG2E * cum_c[:, hd:hd + 1] for hd in heads]
    s2s = [jnp.where(kr <= qc, _dot_nt(kn[:, slab[hd]], qms[hd]) - LOG2E * cum_n[:, hd:hd + 1], -1e30)
           for hd in heads]
    ms = [jnp.maximum(jnp.max(s1, axis=0, keepdims=True), jnp.max(s2, axis=0, keepdims=True))
          for s1, s2 in zip(s1s, s2s)]
    p1s = [jnp.exp2(s1 - m).astype(BF16) for s1, m in zip(s1s, ms)]
    p2s = [jnp.exp2(s2 - m).astype(BF16) for s2, m in zip(s2s, ms)]
    nums = [_dot_tn(p1s[hd], jnp.where(in_c[hd], vc[:, slab[hd]], jnp.zeros((past, LANES), BF16)))
            + _dot_tn(p2s[hd], jnp.where(in_n[hd], vn[:, slab[hd]], jnp.zeros((steps, LANES), BF16))) for hd in heads]
    dens = [_dot_tn(p1s[hd], ones_c) + _dot_tn(p2s[hd], ones_n) for hd in heads]
    outs = [n / d for n, d in zip(nums, dens)]
    o_ref[...] = jnp.concatenate([outs[2 * pr] + outs[2 * pr + 1] for pr in range(HEADS // 2)], axis=1).astype(BF16)


def _fox_sample(q, kn, vn, small, kc, vc, lfc, batch, steps, past):
    blk = CUM_BLOCK
    assert past % blk == 0
    r = np.arange(blk)
    tri = jnp.asarray(r[None, :] <= r[:, None], BF16)
    rn = np.arange(steps)
    trin = jnp.asarray(rn[None, :] <= rn[:, None], BF16)
    row = lambda w: pl.BlockSpec((steps, w), lambda b: (b, 0))
    return pl.pallas_call(
        functools.partial(_fox_sample_kernel, past=past, steps=steps),
        grid=(batch,),
        in_specs=[row(WIDTH), row(WIDTH), row(WIDTH), row(LANES),
                  pl.BlockSpec((1, past, WIDTH), lambda b: (b, 0, 0)),
                  pl.BlockSpec((1, past, WIDTH), lambda b: (b, 0, 0)),
                  pl.BlockSpec((1, past, HEADS), lambda b: (b, 0, 0)),
                  _const_spec((blk, blk)), _const_spec((steps, steps))],
        out_specs=row(WIDTH),
        out_shape=jax.ShapeDtypeStruct((batch * steps, WIDTH), BF16),
        compiler_params=pltpu.CompilerParams(dimension_semantics=("arbitrary",), vmem_limit_bytes=VMEM_LIMIT),
        name="fox_sample",
    )(q, kn, vn, small, kc, vc, lfc, tri, trin)


def _gdn_kernel(gq_ref, small_ref, hist_ref, s0_ref, convw_ref, tri_ref, hsum_ref, hexp_g_ref, hexp_b_ref,
                o_ref, sfin_ref, xbuf, state, *, chunk):
    c_idx = pl.program_id(1)
    n_chunks = pl.num_programs(1)
    hist_rows = SUBLANES

    @pl.when(c_idx == 0)
    def _():
        xbuf[0:hist_rows, :] = hist_ref[0]
        state[...] = s0_ref[0]

    xbuf[hist_rows:hist_rows + chunk, :] = gq_ref[...]
    w = convw_ref[...]
    base = hist_rows - (CONV_WIDTH - 1)
    y = xbuf[base:base + chunk, :] * w[0:1, :]
    for i in range(1, CONV_WIDTH):
        y = y + xbuf[base + i:base + i + chunk, :] * w[i:i + 1, :]
    xbuf[0:hist_rows, :] = xbuf[chunk:chunk + hist_rows, :]
    y = y * _sigmoid(y)

    hsum = hsum_ref[...]
    hexp_g = hexp_g_ref[...]
    hexp_b = hexp_b_ref[...]

    def l2n(x):
        inv = lax.rsqrt(_dot_sel(x * x, hsum, terms=2) + EPS)
        return x * _dot_sel(inv, hexp_g, terms=2)

    q = l2n(y[:, 0:WIDTH]) * (HEAD_DIM ** -0.5)
    k = l2n(y[:, WIDTH:2 * WIDTH])
    v = y[:, 2 * WIDTH:3 * WIDTH]

    small = small_ref[...]
    gc = _sel_dot(tri_ref[...], small)
    gc_x = _dot_sel(gc, hexp_g)
    beta_x = _dot_sel(small, hexp_b, terms=2)
    egc_x = jnp.exp(gc_x)
    glast_x = gc_x[chunk - 1:chunk, :]
    eglast_x = jnp.exp(glast_x)
    kb = k * beta_x
    vb = v * beta_x
    kbe = kb * egc_x
    qe = q * egc_x
    kdec = k * jnp.exp(glast_x - gc_x)

    pad_rows = LANES - chunk
    gc_sq = jnp.concatenate([gc, jnp.zeros((pad_rows, LANES), F32)], axis=0) if pad_rows else gc
    gc_t = gc_sq.T

    ri = lax.broadcasted_iota(jnp.int32, (chunk, chunk), 0)
    ci = lax.broadcasted_iota(jnp.int32, (chunk, chunk), 1)
    eye = jnp.where(ri == ci, 1.0, 0.0).astype(F32)
    n_double = int(np.log2(chunk)) - 1
    assert 2 ** (n_double + 1) == chunk

    heads = range(HEADS)
    sls = [slice(hd * HEAD_DIM, (hd + 1) * HEAD_DIM) for hd in heads]
    decs = [jnp.where(ri >= ci, jnp.exp(gc[:, L_G + hd:L_G + hd + 1] - gc_t[L_G + hd:L_G + hd + 1, 0:chunk]), 0.0)
            for hd in heads]
    k_hs = [k[:, sl].astype(BF16) for sl in sls]
    pws = [-(_dot_nt(kb[:, sl].astype(BF16), k_h) * jnp.where(ri > ci, dec, 0.0))
           for sl, k_h, dec in zip(sls, k_hs, decs)]
    qks = [(_dot_nt(q[:, sl].astype(BF16), k_h) * dec).astype(BF16) for sl, k_h, dec in zip(sls, k_hs, decs)]
    tinvs = [eye + pw for pw in pws]
    for _ in range(n_double):
        pws = [pw.astype(BF16) for pw in pws]
        pws = [_dot(pw, pw) for pw in pws]
        tinvs = [tinv + _dot(tinv.astype(BF16), pw.astype(BF16)) for tinv, pw in zip(tinvs, pws)]
    uws = [_dot(tinv.astype(BF16), jnp.concatenate([vb[:, sl], kbe[:, sl]], axis=1).astype(BF16))
           for tinv, sl in zip(tinvs, sls)]
    s_fs = [state[hd] for hd in heads]
    s_bs = [s_f.astype(BF16) for s_f in s_fs]
    v_news = [(uw[:, 0:HEAD_DIM] - _dot(uw[:, HEAD_DIM:2 * HEAD_DIM].astype(BF16), s_b)).astype(BF16)
              for uw, s_b in zip(uws, s_bs)]
    outs = [_dot(qe[:, sl].astype(BF16), s_b) + _dot(qk, v_new)
            for sl, s_b, qk, v_new in zip(sls, s_bs, qks, v_news)]
    for hd in heads:
        state[hd] = s_fs[hd] * eglast_x[:, sls[hd]] + _dot_tn(kdec[:, sls[hd]].astype(BF16), v_news[hd])
    o_ref[...] = jnp.concatenate(outs, axis=1)

    @pl.when(c_idx == n_chunks - 1)
    def _():
        sfin_ref[0] = state[...]


def _gdn(gqkv, small, hist, s0, conv_w, batch, seq, chunk):
    assert seq % chunk == 0 and chunk % SUBLANES == 0 and chunk <= LANES
    nc = seq // chunk
    r = np.arange(chunk)
    tri = jnp.asarray(r[None, :] <= r[:, None], BF16)
    head_of = np.arange(WIDTH) // HEAD_DIM
    hsum = jnp.asarray(head_of[:, None] == (np.arange(LANES)[None, :] - L_G), BF16)
    hexp_g = jnp.asarray((np.arange(LANES)[:, None] - L_G) == head_of[None, :], BF16)
    hexp_b = jnp.asarray((np.arange(LANES)[:, None] - L_BETA) == head_of[None, :], BF16)
    conv_w8 = jnp.concatenate([conv_w, jnp.zeros((SUBLANES - CONV_WIDTH, CONV_DIM), F32)], axis=0)
    return pl.pallas_call(
        functools.partial(_gdn_kernel, chunk=chunk),
        grid=(batch, nc),
        in_specs=[pl.BlockSpec((chunk, CONV_DIM), lambda b, c: (b * nc + c, 0)),
                  pl.BlockSpec((chunk, LANES), lambda b, c: (b * nc + c, 0)),
                  pl.BlockSpec((1, SUBLANES, CONV_DIM), lambda b, c: (b, 0, 0)),
                  pl.BlockSpec((1, HEADS, HEAD_DIM, HEAD_DIM), lambda b, c: (b, 0, 0, 0)),
                  _const_spec((SUBLANES, CONV_DIM)), _const_spec((chunk, chunk)),
                  _const_spec((WIDTH, LANES)), _const_spec((LANES, WIDTH)), _const_spec((LANES, WIDTH))],
        out_specs=(pl.BlockSpec((chunk, WIDTH), lambda b, c: (b * nc + c, 0)),
                   pl.BlockSpec((1, HEADS, HEAD_DIM, HEAD_DIM), lambda b, c: (b, 0, 0, 0))),
        out_shape=(jax.ShapeDtypeStruct((batch * seq, WIDTH), F32),
                   jax.ShapeDtypeStruct((batch, HEADS, HEAD_DIM, HEAD_DIM), F32)),
        scratch_shapes=[pltpu.VMEM((chunk + SUBLANES, CONV_DIM), F32),
                        pltpu.VMEM((HEADS, HEAD_DIM, HEAD_DIM), F32)],
        compiler_params=pltpu.CompilerParams(dimension_semantics=("arbitrary", "arbitrary"),
                                             vmem_limit_bytes=VMEM_LIMIT),
        name="gdn",
    )(gqkv, small, hist, s0, conv_w8, tri, hsum, hexp_g, hexp_b)


GROUP = 4
GROUP_W = GROUP * HEAD_DIM


def _stack_heads(x, lane):
    return jnp.concatenate([jnp.where(lane == hh, x, jnp.zeros_like(x)) for hh in range(GROUP)], axis=0)


def _fold_heads(x, rows):
    out = x[0:rows]
    for hh in range(1, GROUP):
        out = out + x[hh * rows:(hh + 1) * rows]
    return out


def _gdn_local_kernel(gq_ref, prev_ref, small_ref, convw_ref, tri_ref, hsum_ref, hexp_g_ref, hexp_b_ref, dexp_ref,
                      u_ref, w_ref, qe_ref, kdec_ref, qkw_ref, egl_ref, xbuf, *, chunk):
    i = pl.program_id(1)
    tm = gq_ref.shape[0]
    hist_rows = SUBLANES
    prev = prev_ref[...]
    xbuf[0:hist_rows, :] = jnp.where(i == 0, jnp.zeros_like(prev), prev)
    xbuf[hist_rows:hist_rows + tm, :] = gq_ref[...]
    w = convw_ref[...]
    base = hist_rows - (CONV_WIDTH - 1)
    y = xbuf[base:base + tm, :] * w[0:1, :]
    for t in range(1, CONV_WIDTH):
        y = y + xbuf[base + t:base + t + tm, :] * w[t:t + 1, :]
    y = y * _sigmoid(y)

    hsum = hsum_ref[...]
    hexp_g = hexp_g_ref[...]
    hexp_b = hexp_b_ref[...]

    def l2n(x):
        inv = lax.rsqrt(_dot_sel(x * x, hsum, terms=2) + EPS)
        return x * _dot_sel(inv, hexp_g, terms=2)

    q = l2n(y[:, 0:WIDTH]) * (HEAD_DIM ** -0.5)
    k = l2n(y[:, WIDTH:2 * WIDTH])
    v = y[:, 2 * WIDTH:3 * WIDTH]

    small = small_ref[...]
    gc = _sel_dot(tri_ref[...], small)
    gc_x = _dot_sel(gc, hexp_g)
    beta_x = _dot_sel(small, hexp_b, terms=2)
    n_chunks = tm // chunk
    lasts = [gc_x[(c + 1) * chunk - 1:(c + 1) * chunk, :] for c in range(n_chunks)]
    glast_x = jnp.concatenate([jnp.broadcast_to(r, (chunk, WIDTH)) for r in lasts], axis=0)
    egc_x = jnp.exp(gc_x)
    kb = k * beta_x
    vb = (v * beta_x).astype(BF16)
    kbe = (kb * egc_x).astype(BF16)
    kb = kb.astype(BF16)
    qe_ref[...] = (q * egc_x).astype(BF16)
    kdec_ref[...] = (k * jnp.exp(glast_x - gc_x)).astype(BF16)
    for c in range(n_chunks):
        egl_ref[c] = jnp.broadcast_to(jnp.exp(lasts[c]), (SUBLANES, WIDTH))
    q = q.astype(BF16)
    k = k.astype(BF16)

    rows = GROUP * chunk
    lane = lax.broadcasted_iota(jnp.int32, (chunk, GROUP_W), 1)
    head_of_lane = lane // HEAD_DIM
    ri = lax.broadcasted_iota(jnp.int32, (rows, rows), 0)
    ci = lax.broadcasted_iota(jnp.int32, (rows, rows), 1)
    same = (ri // chunk) == (ci // chunk)
    incl = same & (ri >= ci)
    strict = same & (ri > ci)
    eye = jnp.where(ri == ci, 1.0, 0.0).astype(F32)
    n_double = int(np.log2(chunk)) - 1

    placed = None
    for t, part in enumerate(_split3(gc)):
        term = _dot(part, dexp_ref[t])
        placed = term if placed is None else placed + term
    lane_d = lax.broadcasted_iota(jnp.int32, (chunk, LANES), 1)
    head_d = jnp.where(lane_d < DECAY_LANES * HEADS, lane_d // DECAY_LANES, -1)
    used = lane_d[0:1, :] < DECAY_LANES * HEADS
    hi_half = lane_d[0:1, :] % DECAY_LANES >= DECAY_LANES // 2
    diff_l = (placed[:, 0:LANES] + jnp.where(used & hi_half, 1.0, 0.0)).astype(BF16)
    diff_r = (jnp.where(used & ~hi_half, 1.0, 0.0) - placed[:, LANES:2 * LANES]).astype(BF16)

    probs = [(slice(c * chunk, (c + 1) * chunk), slice(g * GROUP_W, (g + 1) * GROUP_W))
             for c in range(n_chunks) for g in range(HEADS // GROUP)]
    stack = lambda x, pr: _stack_heads(x[pr[0], pr[1]], head_of_lane)
    stack_d = lambda x, c, g: jnp.concatenate(
        [jnp.where(head_d == GROUP * g + hh, x[c * chunk:(c + 1) * chunk], jnp.zeros((chunk, LANES), BF16))
         for hh in range(GROUP)], axis=0)
    diffs = [_dot_nt(stack_d(diff_l, c, g), stack_d(diff_r, c, g))
             for c in range(n_chunks) for g in range(HEADS // GROUP)]
    aqs = [_dot_nt(jnp.concatenate([stack(kb, pr), stack(q, pr)], axis=0), stack(k, pr)) for pr in probs]
    decays = [jnp.exp(jnp.where(incl, d, -1e30)) for d in diffs]
    for pr, aq, decay in zip(probs, aqs, decays):
        qkw_ref[pr[0], pr[1]] = _fold_heads(aq[rows:2 * rows] * decay, chunk).astype(BF16)
    pws = [-(aq[0:rows] * jnp.where(strict, decay, 0.0)) for aq, decay in zip(aqs, decays)]
    half = rows // 2
    zero_q = jnp.zeros((half, half), BF16)
    quads = lambda x, r0=0: (x[r0:r0 + half, 0:half], x[r0 + half:r0 + rows, half:rows])
    full = lambda qa, qb: jnp.concatenate([jnp.concatenate([qa.astype(BF16), zero_q], axis=1),
                                           jnp.concatenate([zero_q, qb.astype(BF16)], axis=1)], axis=0)
    tinvs = [quads(eye + pw) for pw in pws]
    pws = [pw.astype(BF16) for pw in pws]
    pws = [full(*quads(_dot(pw, pw))) for pw in pws]
    for it in range(n_double):
        if it < n_double - 1:
            both = [_dot(jnp.concatenate([pw, full(*tinv)], axis=0), pw) for pw, tinv in zip(pws, tinvs)]
            pws = [full(*quads(b)) for b in both]
            tinvs = [tuple(t + q for t, q in zip(tinv, quads(b, rows))) for tinv, b in zip(tinvs, both)]
        else:
            tinvs = [tuple(t + q for t, q in zip(tinv, quads(_dot(full(*tinv), pw)))) for pw, tinv in zip(pws, tinvs)]
    uws = [_dot(full(*tinv), jnp.concatenate([stack(vb, pr), stack(kbe, pr)], axis=1))
           for pr, tinv in zip(probs, tinvs)]
    for pr, uw in zip(probs, uws):
        u_ref[pr[0], pr[1]] = _fold_heads(uw[:, 0:GROUP_W], chunk).astype(BF16)
        w_ref[pr[0], pr[1]] = _fold_heads(uw[:, GROUP_W:2 * GROUP_W], chunk).astype(BF16)


def _gdn_scan_kernel(u_ref, w_ref, qe_ref, kdec_ref, qkw_ref, egl_ref, o_ref, sfin_ref, state, *, chunk):
    c_idx = pl.program_id(0)
    batch = u_ref.shape[0]

    @pl.when(c_idx == 0)
    def _():
        state[...] = jnp.zeros_like(state)

    rows = GROUP * chunk
    lane = lax.broadcasted_iota(jnp.int32, (chunk, GROUP_W), 1) // HEAD_DIM
    ri = lax.broadcasted_iota(jnp.int32, (GROUP_W, GROUP_W), 0)
    ci = lax.broadcasted_iota(jnp.int32, (GROUP_W, GROUP_W), 1)
    same = (ri // HEAD_DIM) == (ci // HEAD_DIM)
    probs = [(b, g, slice(g * GROUP_W, (g + 1) * GROUP_W)) for b in range(batch) for g in range(HEADS // GROUP)]
    s_fs = [state[b, g] for b, g, _ in probs]
    s_bs = [s_f.astype(BF16) for s_f in s_fs]
    v_news = [(u_ref[b, :, ls].astype(F32) - _dot(w_ref[b, :, ls], s_b)).astype(BF16)
              for (b, _, ls), s_b in zip(probs, s_bs)]
    for (b, _, ls), s_b, v_new in zip(probs, s_bs, v_news):
        o_ref[b, :, ls] = _dot(jnp.concatenate([qe_ref[b, :, ls], qkw_ref[b, :, ls]], axis=1),
                               jnp.concatenate([s_b, _stack_heads(v_new, lane)], axis=0))
    kvs = [_dot_tn(kdec_ref[b, :, ls], v_new) for (b, _, ls), v_new in zip(probs, v_news)]
    for (b, g, ls), s_f, kv in zip(probs, s_fs, kvs):
        state[b, g] = s_f * egl_ref[b, 0, 0:1, ls] + jnp.where(same, kv, 0.0)

    @pl.when(c_idx == pl.num_programs(0) - 1)
    def _():
        for b in range(batch):
            for hd in range(HEADS):
                g, hh = divmod(hd, GROUP)
                sl = slice(hh * HEAD_DIM, (hh + 1) * HEAD_DIM)
                sfin_ref[b, hd] = state[b, g, sl, sl]


def _gdn_long(gqkv, small, conv_w, batch, seq, chunk):
    tm = TOKEN_TILE
    assert seq % tm == 0 and tm % chunk == 0 and GROUP * chunk == GROUP_W
    nt = seq // tm
    nc = seq // chunk
    r = np.arange(tm)
    tri = jnp.asarray((r[None, :] <= r[:, None]) & ((r[None, :] // chunk) == (r[:, None] // chunk)), BF16)
    head_of = np.arange(WIDTH) // HEAD_DIM
    hsum = jnp.asarray(head_of[:, None] == (np.arange(LANES)[None, :] - L_G), BF16)
    hexp_g = jnp.asarray((np.arange(LANES)[:, None] - L_G) == head_of[None, :], BF16)
    hexp_b = jnp.asarray((np.arange(LANES)[:, None] - L_BETA) == head_of[None, :], BF16)
    conv_w8 = jnp.concatenate([conv_w, jnp.zeros((SUBLANES - CONV_WIDTH, CONV_DIM), F32)], axis=0)
    dexp = np.zeros((3, LANES, 2 * LANES), np.float32)
    for t in range(3):
        for hd in range(HEADS):
            dexp[t, L_G + hd, DECAY_LANES * hd + t] = 1.0
            dexp[t, L_G + hd, LANES + DECAY_LANES * hd + DECAY_LANES // 2 + t] = 1.0
    dexp = jnp.asarray(dexp, BF16)
    row = lambda w: pl.BlockSpec((tm, w), lambda b, i: (b * nt + i, 0))
    per_tile = tm // chunk
    tok = jax.ShapeDtypeStruct((batch * seq, WIDTH), BF16)
    u, w, qe, kdec, qkw, egl = pl.pallas_call(
        functools.partial(_gdn_local_kernel, chunk=chunk),
        grid=(batch, nt),
        in_specs=[row(CONV_DIM),
                  pl.BlockSpec((SUBLANES, CONV_DIM),
                               lambda b, i: (jnp.maximum((b * nt + i) * (tm // SUBLANES) - 1, 0), 0)),
                  row(LANES), _const_spec((SUBLANES, CONV_DIM)), _const_spec((tm, tm)),
                  _const_spec((WIDTH, LANES)), _const_spec((LANES, WIDTH)), _const_spec((LANES, WIDTH)),
                  _const_spec((3, LANES, 2 * LANES))],
        out_specs=(row(WIDTH), row(WIDTH), row(WIDTH), row(WIDTH), row(WIDTH),
                   pl.BlockSpec((per_tile, SUBLANES, WIDTH), lambda b, i: (b * nt + i, 0, 0))),
        out_shape=(tok, tok, tok, tok, tok, jax.ShapeDtypeStruct((batch * nc, SUBLANES, WIDTH), F32)),
        scratch_shapes=[pltpu.VMEM((tm + SUBLANES, CONV_DIM), F32)],
        compiler_params=pltpu.CompilerParams(dimension_semantics=("arbitrary", "arbitrary"),
                                             vmem_limit_bytes=VMEM_LIMIT),
        name="gdn_local",
    )(gqkv, gqkv, small, conv_w8, tri, hsum, hexp_g, hexp_b, dexp)

    blk = pl.BlockSpec((batch, chunk, WIDTH), lambda c: (0, c, 0))
    as3 = lambda a: a.reshape(batch, seq, WIDTH)
    o, s_fin = pl.pallas_call(
        functools.partial(_gdn_scan_kernel, chunk=chunk),
        grid=(nc,),
        in_specs=[blk, blk, blk, blk, blk,
                  pl.BlockSpec((batch, 1, SUBLANES, WIDTH), lambda c: (0, c, 0, 0))],
        out_specs=(blk, pl.BlockSpec((batch, HEADS, HEAD_DIM, HEAD_DIM), lambda c: (0, 0, 0, 0))),
        out_shape=(jax.ShapeDtypeStruct((batch, seq, WIDTH), F32),
                   jax.ShapeDtypeStruct((batch, HEADS, HEAD_DIM, HEAD_DIM), F32)),
        scratch_shapes=[pltpu.VMEM((batch, HEADS // GROUP, GROUP_W, GROUP_W), F32)],
        compiler_params=pltpu.CompilerParams(dimension_semantics=("arbitrary",), vmem_limit_bytes=VMEM_LIMIT),
        name="gdn_scan",
    )(as3(u), as3(w), as3(qe), as3(kdec), as3(qkw), egl.reshape(batch, nc, SUBLANES, WIDTH))
    return o.reshape(batch * seq, WIDTH), s_fin


def _post_kernel(x_ref, fox_ref, gdn_ref, sgg_ref, sga_ref, sgb_ref,
                 wpa_ref, wpb_ref, wout_ref, wup_ref, wdown_ref,
                 ng_ref, gpost_ref, gpre2_ref, gpost2_ref, hsum_ref, hexp_ref, y_ref):
    ya = _dot(fox_ref[...], wpa_ref[...])
    o = gdn_ref[...]
    ms = _dot_sel(o * o, hsum_ref[...], terms=2) * (1.0 / HEAD_DIM)
    o = o * _dot_sel(lax.rsqrt(ms + EPS), hexp_ref[...], terms=2) * ng_ref[...] * sgg_ref[...].astype(F32)
    yb = _dot(o.astype(BF16), wpb_ref[...])
    m = sga_ref[...].astype(F32) * ya + sgb_ref[...].astype(F32) * yb
    mix = _dot(m.astype(BF16), wout_ref[...])
    y1 = x_ref[...] + _rms(mix, gpost_ref[...])
    h2 = _rms(y1, gpre2_ref[...]).astype(BF16)
    acc = jnp.zeros(y1.shape, F32)
    for c in range(D_FF // FF_TILE):
        u = jnp.maximum(_dot(h2, wup_ref[:, c * FF_TILE:(c + 1) * FF_TILE]), 0.0)
        acc = acc + _dot((u * u).astype(BF16), wdown_ref[c * FF_TILE:(c + 1) * FF_TILE, :])
    y_ref[...] = y1 + _rms(acc, gpost2_ref[...])


def _post(x, fox, gdn_o, sgg, sga, sgb, w_pa, w_pb, w_out, w_up, w_down, ng, gpost, gpre2, gpost2):
    t_total = x.shape[0]
    tm = min(POST_TILE, t_total)
    assert t_total % tm == 0
    row = lambda w: pl.BlockSpec((tm, w), lambda i: (i, 0))
    head_of = np.arange(WIDTH) // HEAD_DIM
    hsum = jnp.asarray(head_of[:, None] == (np.arange(LANES)[None, :] - L_G), BF16)
    hexp = jnp.asarray((np.arange(LANES)[:, None] - L_G) == head_of[None, :], BF16)
    return pl.pallas_call(
        _post_kernel,
        grid=(t_total // tm,),
        in_specs=[row(D_MODEL), row(WIDTH), row(WIDTH), row(WIDTH), row(D_MODEL), row(D_MODEL),
                  _const_spec((WIDTH, D_MODEL)), _const_spec((WIDTH, D_MODEL)), _const_spec((D_MODEL, D_MODEL)),
                  _const_spec((D_MODEL, D_FF)), _const_spec((D_FF, D_MODEL)),
                  _const_spec((1, WIDTH)), _const_spec((1, D_MODEL)), _const_spec((1, D_MODEL)),
                  _const_spec((1, D_MODEL)), _const_spec((WIDTH, LANES)), _const_spec((LANES, WIDTH))],
        out_specs=row(D_MODEL),
        out_shape=jax.ShapeDtypeStruct((t_total, D_MODEL), F32),
        compiler_params=pltpu.CompilerParams(dimension_semantics=("arbitrary",), vmem_limit_bytes=VMEM_LIMIT),
        name="post",
    )(x, fox, gdn_o, sgg, sga, sgb, w_pa, w_pb, w_out, w_up, w_down, ng, gpost, gpre2, gpost2, hsum, hexp)


def _rearrange_w_in(w_in):
    o_ff = 3 * WIDTH
    o_gqkv = o_ff + HEADS
    o_ga = o_gqkv + CONV_DIM
    o_gb = o_ga + HEADS
    o_gg = o_gb + HEADS
    o_gate_a = o_gg + WIDTH
    o_gate_b = o_gate_a + D_MODEL
    w_in = w_in.astype(BF16)
    cols = [w_in[:, 0:o_ff], w_in[:, o_gqkv:o_ga], w_in[:, o_gg:o_gate_a], w_in[:, o_gate_a:o_gate_b],
            w_in[:, o_gate_b:o_gate_b + D_MODEL], w_in[:, o_ff:o_gqkv], w_in[:, o_ga:o_gb], w_in[:, o_gb:o_gg],
            jnp.zeros((D_MODEL, LANES - 3 * HEADS), BF16)]
    return jnp.concatenate(cols, axis=1)


def _lane_row(*pieces):
    v = jnp.concatenate([p.astype(F32) for p in pieces])
    return jnp.concatenate([v, jnp.zeros((LANES - v.shape[0],), F32)])[None, :]


def _pad_hist(conv_cache):
    b = conv_cache.shape[0]
    return jnp.concatenate([jnp.zeros((b, SUBLANES - (CONV_WIDTH - 1), CONV_DIM), F32), conv_cache.astype(F32)], axis=1)


def kernel(x_prompt, x_sample, cache_fox_k, cache_fox_v, cache_fox_logf, state_gdn, state_gdn_conv,
           w_in, fox_forget_bias, gdn_conv_w, gdn_a_log, gdn_dt_bias, gdn_norm_g,
           w_proj_fox, w_proj_gdn, w_out, norm_mix_pre, norm_mix_post, norm_mlp_pre, norm_mlp_post,
           w_up, w_down):
    depth = w_in.shape[0]
    bp, sp, _ = x_prompt.shape
    bs, ss, _ = x_sample.shape
    past = cache_fox_k.shape[2]
    y_p = x_prompt.reshape(bp * sp, D_MODEL)
    y_s = x_sample.reshape(bs * ss, D_MODEL)
    st_p, st_s = [], []
    for l in range(depth):
        w_r = _rearrange_w_in(w_in[l])
        padd = _lane_row(fox_forget_bias[l], gdn_dt_bias[l])
        palog = _lane_row(jnp.zeros((HEADS,), F32), gdn_a_log[l])
        gain = norm_mix_pre[l][None, :]
        post_w = (w_proj_fox[l].astype(BF16), w_proj_gdn[l].astype(BF16), w_out[l].astype(BF16),
                  w_up[l].astype(BF16), w_down[l].astype(BF16),
                  jnp.tile(gdn_norm_g[l], HEADS)[None, :], norm_mix_post[l][None, :],
                  norm_mlp_pre[l][None, :], norm_mlp_post[l][None, :])

        q, k, vt, fk, fv, gqkv, sgg, sga, sgb, small, cum = _in_proj(y_p, gain, w_r, padd, palog, sp)
        fox = _fox_prompt(q, k, vt, cum, bp, sp)
        gdn_o, s_fin = _gdn_long(gqkv, small, gdn_conv_w[l], bp, sp, GDN_CHUNK)
        y_p_new = _post(y_p, fox, gdn_o, sgg, sga, sgb, *post_w)
        st_p.append((fk.reshape(bp, sp, HEADS, HEAD_DIM), fv.reshape(bp, sp, HEADS, HEAD_DIM),
                     small[:, L_LOGF:L_LOGF + HEADS].reshape(bp, sp, HEADS), s_fin,
                     gqkv.reshape(bp, sp, CONV_DIM)[:, sp - (CONV_WIDTH - 1):, :]))
        y_p = y_p_new

        q, k, vt, fk, fv, gqkv, sgg, sga, sgb, small, cum = _in_proj(y_s, gain, w_r, padd, palog, ss)
        fox = _fox_sample(q, fk.reshape(bs * ss, WIDTH), fv.reshape(bs * ss, WIDTH), small,
                          cache_fox_k[l].reshape(bs, past, WIDTH), cache_fox_v[l].reshape(bs, past, WIDTH),
                          cache_fox_logf[l], bs, ss, past)
        gdn_o, s_new = _gdn(gqkv, small, _pad_hist(state_gdn_conv[l]), state_gdn[l].astype(F32),
                            gdn_conv_w[l], bs, ss, ss)
        y_s_new = _post(y_s, fox, gdn_o, sgg, sga, sgb, *post_w)
        conv_ext = jnp.concatenate([state_gdn_conv[l].astype(F32), gqkv.reshape(bs, ss, CONV_DIM)], axis=1)
        st_s.append((fk.reshape(bs, ss, HEADS, HEAD_DIM), fv.reshape(bs, ss, HEADS, HEAD_DIM),
                     small[:, L_LOGF:L_LOGF + HEADS].reshape(bs, ss, HEADS), s_new,
                     conv_ext[:, conv_ext.shape[1] - (CONV_WIDTH - 1):, :]))
        y_s = y_s_new

    fk_p, fv_p, lf_p, sg_p, cv_p = [jnp.stack(a) for a in zip(*st_p)]
    fk_s, fv_s, lf_s, sg_s, cv_s = [jnp.stack(a) for a in zip(*st_s)]
    return (y_p.reshape(bp, sp, D_MODEL), y_s.reshape(bs, ss, D_MODEL),
            fk_p, fv_p, lf_p, sg_p, cv_p, fk_s, fv_s, lf_s, sg_s, cv_s)
```
